```python
import math
import jax
import jax.numpy as jnp
from jax import lax
import numpy as np

D_MODEL = 1024
BATCH = 4
SEQ = 8192
DEPTH = 4

N_Q_HEADS = 16
N_KV_HEADS = 4
HEAD_DIM = D_MODEL // N_Q_HEADS
Q_PER_KV = N_Q_HEADS // N_KV_HEADS
WINDOW = 128
ATTN_BLOCK = 128
QKV_WIDTH = (N_Q_HEADS + 2 * N_KV_HEADS) * HEAD_DIM
N_BUCKETS = 32
MAX_DISTANCE = 128
SSM_EXPAND = 2
D_INNER = SSM_EXPAND * D_MODEL
SSM_HEAD_DIM = 64
N_SSM_HEADS = D_INNER // SSM_HEAD_DIM
N_SSM_GROUPS = 4
SSM_HEADS_PER_GROUP = N_SSM_HEADS // N_SSM_GROUPS
D_STATE = 128
CONV_WIDTH = 4
SSM_CHUNK = 128
D_XBC = D_INNER + 2 * N_SSM_GROUPS * D_STATE
D_IN_PROJ = D_INNER + D_XBC + N_SSM_HEADS
D_FF = 7 * D_MODEL // 2
N_EXPERTS = 8
TOP_K = 2
N_ATTN_LAYERS = (DEPTH + 1) // 2
N_SSM_LAYERS = DEPTH // 2
NORM_EPS = 1e-6

kernel_name = 'swa_sink_ssd_moe_hybrid'


def rms_norm(x, w, eps=NORM_EPS):
    xf = x.astype(jnp.float32)
    y = xf * lax.rsqrt(jnp.mean(xf * xf, axis=-1, keepdims=True) + eps)
    return (y * w.astype(jnp.float32)).astype(x.dtype)


def modulate(x, gain, shift, scale):
    return rms_norm(x, gain) * (1 + scale[:, None, :]) + shift[:, None, :]


def t5_causal_bucket(dist):
    max_exact = N_BUCKETS // 2
    d = jnp.maximum(dist, 0)
    df = jnp.maximum(d, 1).astype(jnp.float32)
    large = max_exact + (jnp.log(df / max_exact) / math.log(MAX_DISTANCE / max_exact)
                         * (N_BUCKETS - max_exact)).astype(jnp.int32)
    large = jnp.minimum(large, N_BUCKETS - 1)
    return jnp.where(d < max_exact, d, large)


def band_bias_and_mask(rel_bias, n_blocks):
    qi = jnp.arange(ATTN_BLOCK)[:, None]
    kj = jnp.arange(2 * ATTN_BLOCK)[None, :]
    dist = qi + ATTN_BLOCK - kj
    bias = rel_bias.astype(jnp.float32)[t5_causal_bucket(dist)]
    bias = jnp.transpose(bias, (2, 0, 1)).reshape(N_KV_HEADS, Q_PER_KV, ATTN_BLOCK, 2 * ATTN_BLOCK)
    key_pos = jnp.arange(n_blocks)[:, None, None] * ATTN_BLOCK - ATTN_BLOCK + kj[None]
    mask = (dist >= 0) & (dist < WINDOW) & (key_pos >= 0)
    return bias, mask


def swa_sink_attention(h, w_qkv, q_gain, k_gain, sinks, w_o, bias, mask):
    b, s, _ = h.shape
    nb = s // ATTN_BLOCK
    qkv = h @ w_qkv
    q, k, v = jnp.split(qkv, [N_Q_HEADS * HEAD_DIM, (N_Q_HEADS + N_KV_HEADS) * HEAD_DIM], axis=-1)
    q = rms_norm(q.reshape(b, s, N_Q_HEADS, HEAD_DIM), q_gain)
    k = rms_norm(k.reshape(b, s, N_KV_HEADS, HEAD_DIM), k_gain)
    v = v.reshape(b, s, N_KV_HEADS, HEAD_DIM)
    q = q.reshape(b, nb, ATTN_BLOCK, N_KV_HEADS, Q_PER_KV, HEAD_DIM)

    def band(t):
        t = t.reshape(b, nb, ATTN_BLOCK, N_KV_HEADS, HEAD_DIM)
        prev = jnp.pad(t, ((0, 0), (1, 0), (0, 0), (0, 0), (0, 0)))[:, :-1]
        return jnp.concatenate([prev, t], axis=2)

    kb, vb = band(k), band(v)
    scores = jnp.einsum('bnqhgd,bnkhd->bnhgqk', q, kb).astype(jnp.float32) * (HEAD_DIM ** -0.5) + bias
    scores = jnp.where(mask[None, :, None, None], scores, -jnp.inf)
    sink = sinks.astype(jnp.float32).reshape(1, 1, N_KV_HEADS, Q_PER_KV, 1, 1)
    m = jnp.maximum(jnp.max(scores, axis=-1, keepdims=True), sink)
    p = jnp.exp(scores - m)
    probs = p / (jnp.sum(p, axis=-1, keepdims=True) + jnp.exp(sink - m))
    out = jnp.einsum('bnhgqk,bnkhd->bnqhgd', probs.astype(vb.dtype), vb)
    return out.reshape(b, s, N_Q_HEADS * HEAD_DIM) @ w_o


def causal_depthwise_conv(u, w, bias):
    ch = u.shape[-1]
    y = lax.conv_general_dilated(u, w[:, None, :].astype(u.dtype), window_strides=(1,),
                                 padding=[(CONV_WIDTH - 1, 0)],
                                 dimension_numbers=('NWC', 'WIO', 'NWC'),
                                 feature_group_count=ch)
    return y + bias


def ssd_chunked_scan(x, dt, a, bmat, cmat):
    b, s = x.shape[:2]
    L = SSM_CHUNK
    nc = s // L
    G, HG = N_SSM_GROUPS, SSM_HEADS_PER_GROUP
    xd = (x * dt[..., None]).reshape(b, nc, L, G, HG, SSM_HEAD_DIM)
    a_cs = jnp.cumsum((dt * a).reshape(b, nc, L, G, HG), axis=2)
    bc = bmat.reshape(b, nc, L, G, D_STATE)
    cc = cmat.reshape(b, nc, L, G, D_STATE)
    causal = jnp.tril(jnp.ones((L, L), dtype=bool))
    a_t = jnp.moveaxis(a_cs, 2, -1)
    decay = jnp.exp(jnp.where(causal, a_t[..., :, None] - a_t[..., None, :], -jnp.inf))
    cb = jnp.einsum('bclgn,bcsgn->bcgls', cc, bc)
    y_diag = jnp.einsum('bcghls,bcsghp->bclghp', cb[:, :, :, None] * decay, xd)
    decay_to_end = jnp.exp(a_cs[:, :, -1:] - a_cs)
    states = jnp.einsum('bclgn,bclghp->bcghpn', bc, xd * decay_to_end[..., None])
    chunk_decay = jnp.exp(a_cs[:, :, -1])

    def step(hstate, inp):
        st, dec = inp
        return dec[..., None, None] * hstate + st, hstate

    h0 = jnp.zeros((b, G, HG, SSM_HEAD_DIM, D_STATE), states.dtype)
    _, prev = lax.scan(step, h0, (jnp.moveaxis(states, 1, 0), jnp.moveaxis(chunk_decay, 1, 0)))
    prev = jnp.moveaxis(prev, 0, 1)
    y_off = jnp.einsum('bclgn,bcghpn->bclghp', cc, prev) * jnp.exp(a_cs)[..., None]
    return (y_diag + y_off).reshape(b, s, N_SSM_HEADS, SSM_HEAD_DIM)


def mamba2_mixer(h, w_in, conv_w, conv_b, dt_bias, a_log, d_skip, norm_w, w_out):
    b, s, _ = h.shape
    f32 = jnp.float32
    zxbcdt = h @ w_in
    z, xbc, dt = jnp.split(zxbcdt, [D_INNER, D_INNER + D_XBC], axis=-1)
    xbc = jax.nn.silu(causal_depthwise_conv(xbc, conv_w, conv_b))
    xs, bmat, cmat = jnp.split(xbc, [D_INNER, D_INNER + N_SSM_GROUPS * D_STATE], axis=-1)
    xs = xs.reshape(b, s, N_SSM_HEADS, SSM_HEAD_DIM).astype(f32)
    bmat = bmat.reshape(b, s, N_SSM_GROUPS, D_STATE).astype(f32)
    cmat = cmat.reshape(b, s, N_SSM_GROUPS, D_STATE).astype(f32)
    dt = jax.nn.softplus((dt + dt_bias).astype(f32))
    a = -jnp.exp(a_log.astype(f32))
    y = ssd_chunked_scan(xs, dt, a, bmat, cmat) + d_skip.astype(f32)[:, None] * xs
    y = y.reshape(b, s, D_INNER).astype(h.dtype)
    y = rms_norm(y * jax.nn.silu(z), norm_w)
    return y @ w_out


def swiglu(h, w_gu, w_down):
    g, u = jnp.split(h @ w_gu, 2, axis=-1)
    return (jax.nn.silu(g) * u) @ w_down


def moe_swiglu(h, w_router, w_gu, w_down):
    logits = (h @ w_router).astype(jnp.float32)
    top_vals, top_idx = lax.top_k(logits, TOP_K)
    top_w = jax.nn.softmax(top_vals, axis=-1)
    combine = jnp.sum(jax.nn.one_hot(top_idx, N_EXPERTS, dtype=jnp.float32) * top_w[..., None], axis=-2)
    combine = combine.astype(h.dtype)
    out = jnp.zeros_like(h)
    for e in range(N_EXPERTS):
        out = out + combine[..., e:e + 1] * swiglu(h, w_gu[e], w_down[e])
    return out


def setup_inputs(seed: int = 0) -> dict:
    key = jax.random.key(seed)
    ks = jax.random.split(key, 24)
    f32 = jnp.float32
    nrm = lambda k, shape, sc: jax.random.normal(k, shape, f32) * sc
    na, nm = N_ATTN_LAYERS, N_SSM_LAYERS
    dt0 = jnp.exp(jax.random.uniform(ks[13], (nm, N_SSM_HEADS), f32)
                  * (math.log(0.1) - math.log(0.001)) + math.log(0.001))
    dt0 = jnp.maximum(dt0, 1e-4)
    return {
        'x': nrm(ks[0], (BATCH, SEQ, D_MODEL), 1.0),
        'c': nrm(ks[1], (BATCH, D_MODEL), 1.0),
        'ada_w': nrm(ks[2], (DEPTH, D_MODEL, 6 * D_MODEL), D_MODEL ** -0.5),
        'ada_b': nrm(ks[3], (DEPTH, 6 * D_MODEL), 0.02),
        'norm_gain': 1.0 + nrm(ks[4], (DEPTH, 2, D_MODEL), 0.1),
        'rel_bias': nrm(ks[5], (N_BUCKETS, N_Q_HEADS), 0.5),
        'attn_w_qkv': nrm(ks[6], (na, D_MODEL, QKV_WIDTH), D_MODEL ** -0.5),
        'attn_q_gain': 1.0 + nrm(ks[7], (na, HEAD_DIM), 0.1),
        'attn_k_gain': 1.0 + nrm(ks[8], (na, HEAD_DIM), 0.1),
        'attn_sinks': nrm(ks[9], (na, N_Q_HEADS), 0.5),
        'attn_w_o': nrm(ks[10], (na, N_Q_HEADS * HEAD_DIM, D_MODEL), (N_Q_HEADS * HEAD_DIM) ** -0.5),
        'ssm_w_in': nrm(ks[11], (nm, D_MODEL, D_IN_PROJ), D_MODEL ** -0.5),
        'ssm_conv_w': nrm(ks[12], (nm, CONV_WIDTH, D_XBC), CONV_WIDTH ** -0.5),
        'ssm_conv_b': nrm(ks[14], (nm, D_XBC), 0.02),
        'ssm_dt_bias': dt0 + jnp.log(-jnp.expm1(-dt0)),
        'ssm_a_log': jnp.log(jax.random.uniform(ks[15], (nm, N_SSM_HEADS), f32, 1.0, 16.0)),
        'ssm_d': 1.0 + nrm(ks[16], (nm, N_SSM_HEADS), 0.1),
        'ssm_norm_w': 1.0 + nrm(ks[17], (nm, D_INNER), 0.1),
        'ssm_w_out': nrm(ks[18], (nm, D_INNER, D_MODEL), D_INNER ** -0.5),
        'ffn_w_gu': nrm(ks[19], (na, D_MODEL, 2 * D_FF), D_MODEL ** -0.5),
        'ffn_w_down': nrm(ks[20], (na, D_FF, D_MODEL), D_FF ** -0.5),
        'moe_w_router': nrm(ks[21], (nm, D_MODEL, N_EXPERTS), D_MODEL ** -0.5),
        'moe_w_gu': nrm(ks[22], (nm, N_EXPERTS, D_MODEL, 2 * D_FF), D_MODEL ** -0.5),
        'moe_w_down': nrm(ks[23], (nm, N_EXPERTS, D_FF, D_MODEL), D_FF ** -0.5),
    }


def reference(x, c, ada_w, ada_b, norm_gain, rel_bias, attn_w_qkv, attn_q_gain, attn_k_gain,
              attn_sinks, attn_w_o, ssm_w_in, ssm_conv_w, ssm_conv_b, ssm_dt_bias, ssm_a_log,
              ssm_d, ssm_norm_w, ssm_w_out, ffn_w_gu, ffn_w_down, moe_w_router, moe_w_gu,
              moe_w_down):
    s = x.shape[1]
    bias, mask = band_bias_and_mask(rel_bias, s // ATTN_BLOCK)
    cond = jax.nn.silu(c)
    for i in range(DEPTH):
        j = i // 2
        mod = cond @ ada_w[i] + ada_b[i]
        sh1, sc1, g1, sh2, sc2, g2 = jnp.split(mod, 6, axis=-1)
        hm = modulate(x, norm_gain[i, 0], sh1, sc1)
        if i % 2 == 0:
            mix = swa_sink_attention(hm, attn_w_qkv[j], attn_q_gain[j], attn_k_gain[j],
                                     attn_sinks[j], attn_w_o[j], bias, mask)
        else:
            mix = mamba2_mixer(hm, ssm_w_in[j], ssm_conv_w[j], ssm_conv_b[j], ssm_dt_bias[j],
                               ssm_a_log[j], ssm_d[j], ssm_norm_w[j], ssm_w_out[j])
        x = x + g1[:, None, :] * mix
        hf = modulate(x, norm_gain[i, 1], sh2, sc2)
        if i % 2 == 0:
            ffn = swiglu(hf, ffn_w_gu[j], ffn_w_down[j])
        else:
            ffn = moe_swiglu(hf, moe_w_router[j], moe_w_gu[j], moe_w_down[j])
        x = x + g2[:, None, :] * ffn
    return x
```

```python
import functools
import math

import numpy as np
import jax
import jax.numpy as jnp
from jax import lax
from jax.experimental import pallas as pl
from jax.experimental.pallas import tpu as pltpu

F32 = jnp.float32
BF16 = jnp.bfloat16

D_MODEL = 1024
DEPTH = 4
N_Q_HEADS = 16
N_KV_HEADS = 4
HEAD_DIM = 64
Q_PER_KV = 4
ATTN_BLOCK = 128
N_BUCKETS = 32
MAX_DISTANCE = 128
D_INNER = 2048
N_SSM_HEADS = 32
N_SSM_GROUPS = 4
D_STATE = 128
CONV_WIDTH = 4
SSM_CHUNK = 128
D_BC = 2 * N_SSM_GROUPS * D_STATE
D_FF = 3584
N_EXPERTS = 8
TOP_K = 2
NORM_EPS = 1e-6

LANES = 128
SUBLANES = 8
VMEM_LIMIT = 56 * 1024 * 1024
NEG_INF = float("-inf")


def _cparams(sem):
    return pltpu.CompilerParams(dimension_semantics=sem, vmem_limit_bytes=VMEM_LIMIT)


def _sigmoid(v):
    return 1.0 / (1.0 + jnp.exp(-v))


def _rms_mod(x, gain, scale, shift):
    ms = jnp.mean(x * x, axis=-1, keepdims=True)
    y = x * lax.rsqrt(ms + NORM_EPS)
    return (y * gain) * (1.0 + scale) + shift


def _split2(v):
    hi = v.astype(BF16)
    lo = (v - hi.astype(F32)).astype(BF16)
    return hi, lo


def _split3(v):
    hi = v.astype(BF16)
    r = v - hi.astype(F32)
    mid = r.astype(BF16)
    lo = (r - mid.astype(F32)).astype(BF16)
    return hi, mid, lo


def _dot(a, b):
    return jnp.dot(a, b, preferred_element_type=F32)


def _mod_kernel(c_ref, w_ref, b_ref, o_ref):
    c = c_ref[...]
    cond = c * _sigmoid(c)
    o_ref[0] = _dot(cond.astype(BF16), w_ref[0].astype(BF16)) + b_ref[0]


def _adaln_mod(c, ada_w, ada_b):
    depth, d, n = ada_w.shape
    b = c.shape[0]
    tn = 1024
    c_pad = jnp.zeros((SUBLANES, d), F32).at[:b].set(c)
    out = pl.pallas_call(
        _mod_kernel,
        out_shape=jax.ShapeDtypeStruct((depth, SUBLANES, n), F32),
        grid=(depth, n // tn),
        in_specs=[
            pl.BlockSpec((SUBLANES, d), lambda i, j: (0, 0)),
            pl.BlockSpec((1, d, tn), lambda i, j: (i, 0, j)),
            pl.BlockSpec((1, 1, tn), lambda i, j: (i, 0, j)),
        ],
        out_specs=pl.BlockSpec((1, SUBLANES, tn), lambda i, j: (i, 0, j)),
        compiler_params=_cparams(("arbitrary", "arbitrary")),
        name="adaln_mod",
    )(c_pad, ada_w, ada_b.reshape(depth, 1, n))
    return out[:, :b]


def _t5_bucket_table():
    qi = np.arange(ATTN_BLOCK)[:, None]
    kj = np.arange(2 * ATTN_BLOCK)[None, :]
    dist = qi + ATTN_BLOCK - kj
    max_exact = N_BUCKETS // 2
    d = np.maximum(dist, 0)
    df = np.maximum(d, 1).astype(np.float64)
    large = max_exact + (np.log(df / max_exact) / math.log(MAX_DISTANCE / max_exact)
                         * (N_BUCKETS - max_exact)).astype(np.int32)
    large = np.minimum(large, N_BUCKETS - 1)
    bucket = np.where(d < max_exact, d, large)
    valid = (dist >= 0) & (dist < ATTN_BLOCK)
    return np.where(valid, bucket, -1).astype(np.int32)


def _bias_kernel(rb_ref, bkt_ref, o_ref):
    j = pl.program_id(0)
    gp = j // Q_PER_KV
    qi = j % Q_PER_KV
    bkt = bkt_ref[...]
    for a in range(2):
        h = (2 * gp + a) * Q_PER_KV + qi
        acc = jnp.full(bkt.shape, NEG_INF, F32)
        for b in range(N_BUCKETS):
            acc = jnp.where(bkt == b, rb_ref[b, h], acc)
        o_ref[0, a * ATTN_BLOCK:(a + 1) * ATTN_BLOCK, :] = acc


def _bias_table(rel_bias):
    bkt = jnp.asarray(_t5_bucket_table())
    n_slots = N_Q_HEADS // 2
    return pl.pallas_call(
        _bias_kernel,
        out_shape=jax.ShapeDtypeStruct((n_slots, 2 * ATTN_BLOCK, 2 * ATTN_BLOCK), F32),
        grid=(n_slots,),
        in_specs=[
            pl.BlockSpec(memory_space=pltpu.SMEM),
            pl.BlockSpec((ATTN_BLOCK, 2 * ATTN_BLOCK), lambda j: (0, 0)),
        ],
        out_specs=pl.BlockSpec((1, 2 * ATTN_BLOCK, 2 * ATTN_BLOCK), lambda j: (j, 0, 0)),
        compiler_params=_cparams(("arbitrary",)),
        name="t5_bias_table",
    )(rel_bias.astype(F32), bkt)


def _qkv_kernel(x_ref, ms_ref, gain_ref, w_ref, qg_ref, kg_ref, q_ref, k_ref, v_ref):
    tm = x_ref.shape[0]
    ms = ms_ref[0]
    hm = _rms_mod(x_ref[...], gain_ref[...], ms[1:2], ms[0:1]).astype(BF16)
    lo = lax.broadcasted_iota(jnp.int32, (tm, LANES), 1) < HEAD_DIM

    def head_norm(slab, g):
        sq = slab * slab
        s_lo = jnp.sum(jnp.where(lo, sq, 0.0), axis=-1, keepdims=True)
        s_hi = jnp.sum(jnp.where(lo, 0.0, sq), axis=-1, keepdims=True)
        msq = jnp.where(lo, s_lo, s_hi) * (1.0 / HEAD_DIM)
        return slab * lax.rsqrt(msq + NORM_EPS) * g

    nq = N_Q_HEADS * HEAD_DIM
    nk = N_KV_HEADS * HEAD_DIM
    q_all = _dot(hm, w_ref[:, :nq])
    for j in range(nq // LANES):
        sl = slice(j * LANES, (j + 1) * LANES)
        q_ref[:, sl] = (head_norm(q_all[:, sl], qg_ref[...]) * (HEAD_DIM ** -0.5)).astype(BF16)
    k_all = _dot(hm, w_ref[:, nq:nq + nk])
    for j in range(nk // LANES):
        sl = slice(j * LANES, (j + 1) * LANES)
        k_ref[:, sl] = head_norm(k_all[:, sl], kg_ref[...]).astype(BF16)
    v_ref[...] = _dot(hm, w_ref[:, nq + nk:]).astype(BF16)


def _qkv_proj(x2, ms, gain, w_qkv_b, qg2, kg2, seq, tm=512):
    t, d = x2.shape
    nq = N_Q_HEADS * HEAD_DIM
    nk = N_KV_HEADS * HEAD_DIM
    tpb = seq // tm
    return pl.pallas_call(
        _qkv_kernel,
        out_shape=(jax.ShapeDtypeStruct((t, nq), BF16),
                   jax.ShapeDtypeStruct((t, nk), BF16),
                   jax.ShapeDtypeStruct((t, nk), BF16)),
        grid=(t // tm,),
        in_specs=[
            pl.BlockSpec((tm, d), lambda i: (i, 0)),
            pl.BlockSpec((1, 2, d), lambda i: (i // tpb, 0, 0)),
            pl.BlockSpec((1, d), lambda i: (0, 0)),
            pl.BlockSpec((d, nq + 2 * nk), lambda i: (0, 0)),
            pl.BlockSpec((1, LANES), lambda i: (0, 0)),
            pl.BlockSpec((1, LANES), lambda i: (0, 0)),
        ],
        out_specs=(pl.BlockSpec((tm, nq), lambda i: (i, 0)),
                   pl.BlockSpec((tm, nk), lambda i: (i, 0)),
                   pl.BlockSpec((tm, nk), lambda i: (i, 0))),
        compiler_params=_cparams(("arbitrary",)),
        name="attn_qkv",
    )(x2, ms, gain, w_qkv_b, qg2, kg2)


def _attn_kernel(sink_ref, q_ref, kc_ref, kp_ref, vc_ref, vp_ref, bias_ref, o_ref):
    blk = ATTN_BLOCK
    first = pl.program_id(1) == 0
    lane = lax.broadcasted_iota(jnp.int32, (blk, LANES), 1)
    lo = lane < HEAD_DIM
    mask_lo = jnp.where(lo, 1.0, 0.0).astype(BF16)
    mask_hi = jnp.where(lo, 0.0, 1.0).astype(BF16)
    row = lax.broadcasted_iota(jnp.int32, (2 * blk, 1), 0)
    nt = (((1,), (1,)), ((), ()))
    for gp in range(N_KV_HEADS // 2):
        ksl = slice(gp * LANES, (gp + 1) * LANES)
        kc = kc_ref[:, ksl]
        kp = kp_ref[:, ksl]
        vc = vc_ref[:, ksl]
        vp = vp_ref[:, ksl]
        for qi in range(Q_PER_KV):
            j = gp * Q_PER_KV + qi
            sl = slice(j * LANES, (j + 1) * LANES)
            qs = q_ref[:, sl]
            q2 = jnp.concatenate([qs * mask_lo, qs * mask_hi], axis=0)
            s_p = lax.dot_general(q2, kp, nt, preferred_element_type=F32)
            s_c = lax.dot_general(q2, kc, nt, preferred_element_type=F32)
            bias = bias_ref[j]
            s_p = jnp.where(first, NEG_INF, s_p + bias[:, :blk])
            s_c = s_c + bias[:, blk:]
            h_a = (2 * gp) * Q_PER_KV + qi
            h_b = (2 * gp + 1) * Q_PER_KV + qi
            sink = jnp.where(row < blk, sink_ref[h_a], sink_ref[h_b])
            m = jnp.maximum(jnp.maximum(jnp.max(s_p, axis=-1, keepdims=True),
                                        jnp.max(s_c, axis=-1, keepdims=True)), sink)
            p_p = jnp.exp(s_p - m)
            p_c = jnp.exp(s_c - m)
            den = (jnp.sum(p_p, axis=-1, keepdims=True) + jnp.sum(p_c, axis=-1, keepdims=True)
                   + jnp.exp(sink - m))
            pv = _dot(p_p.astype(BF16), vp) + _dot(p_c.astype(BF16), vc)
            pv = pv / den
            o_ref[:, sl] = jnp.where(lo, pv[:blk], pv[blk:]).astype(BF16)


def _attention(q, k, v, bias, sinks, batch, seq):
    t, nq = q.shape
    nk = k.shape[1]
    blk = ATTN_BLOCK
    nb = seq // blk
    cur = lambda b, n, *_: (b * nb + n, 0)
    prev = lambda b, n, *_: (b * nb + jnp.maximum(n - 1, 0), 0)
    return pl.pallas_call(
        _attn_kernel,
        out_shape=jax.ShapeDtypeStruct((t, nq), BF16),
        grid=(batch, nb),
        in_specs=[
            pl.BlockSpec(memory_space=pltpu.SMEM),
            pl.BlockSpec((blk, nq), cur),
            pl.BlockSpec((blk, nk), cur),
            pl.BlockSpec((blk, nk), prev),
            pl.BlockSpec((blk, nk), cur),
            pl.BlockSpec((blk, nk), prev),
            pl.BlockSpec(bias.shape, lambda b, n: (0, 0, 0)),
        ],
        out_specs=pl.BlockSpec((blk, nq), cur),
        compiler_params=_cparams(("arbitrary", "arbitrary")),
        name="swa_attention",
    )(sinks.astype(F32), q, k, k, v, v, bias)


def _swiglu_step(hf, wg_ref, wu_ref, wd_ref, acc_s):
    g = _dot(hf, wg_ref[...])
    u = _dot(hf, wu_ref[...])
    a = (g * _sigmoid(g) * u).astype(BF16)
    acc_s[...] += _dot(a, wd_ref[...])


def _ffn_dense_kernel(x_ref, o_ref, wo_ref, mv_ref, gain_ref, wg_ref, wu_ref, wd_ref,
                      out_ref, x1_s, hf_s, acc_s):
    j = pl.program_id(1)

    @pl.when(j == 0)
    def _():
        mv = mv_ref[0]
        x1 = x_ref[...] + mv[0:1] * _dot(o_ref[...], wo_ref[...])
        x1_s[...] = x1
        hf_s[...] = _rms_mod(x1, gain_ref[...], mv[2:3], mv[1:2]).astype(BF16)
        acc_s[...] = jnp.zeros_like(acc_s)

    _swiglu_step(hf_s[...], wg_ref, wu_ref, wd_ref, acc_s)

    @pl.when(j == pl.num_programs(1) - 1)
    def _():
        out_ref[...] = x1_s[...] + mv_ref[0][3:4] * acc_s[...]


def _ffn_dense(x2, o, wo_b, mv, gain, w_gu_b, w_down_b, seq, tm=512, tf=512):
    t, d = x2.shape
    f = w_down_b.shape[0]
    nj = f // tf
    tpb = seq // tm
    return pl.pallas_call(
        _ffn_dense_kernel,
        out_shape=jax.ShapeDtypeStruct((t, d), F32),
        grid=(t // tm, nj),
        in_specs=[
            pl.BlockSpec((tm, d), lambda i, j: (i, 0)),
            pl.BlockSpec((tm, o.shape[1]), lambda i, j: (i, 0)),
            pl.BlockSpec(wo_b.shape, lambda i, j: (0, 0)),
            pl.BlockSpec((1, 4, d), lambda i, j: (i // tpb, 0, 0)),
            pl.BlockSpec((1, d), lambda i, j: (0, 0)),
            pl.BlockSpec((d, tf), lambda i, j: (0, j)),
            pl.BlockSpec((d, tf), lambda i, j: (0, nj + j)),
            pl.BlockSpec((tf, d), lambda i, j: (j, 0)),
        ],
        out_specs=pl.BlockSpec((tm, d), lambda i, j: (i, 0)),
        scratch_shapes=[pltpu.VMEM((tm, d), F32), pltpu.VMEM((tm, d), BF16),
                        pltpu.VMEM((tm, d), F32)],
        compiler_params=_cparams(("arbitrary", "arbitrary")),
        name="wo_ffn_dense",
    )(x2, o, wo_b, mv, gain, w_gu_b, w_gu_b, w_down_b)


def _ffn_moe_kernel(te_ref, xs_ref, wg_ref, wu_ref, wd_ref, out_ref, hf_s, acc_s):
    j = pl.program_id(1)

    @pl.when(j == 0)
    def _():
        hf_s[...] = xs_ref[...].astype(BF16)
        acc_s[...] = jnp.zeros_like(acc_s)

    _swiglu_step(hf_s[...], wg_ref, wu_ref, wd_ref, acc_s)

    @pl.when(j == pl.num_programs(1) - 1)
    def _():
        out_ref[...] = acc_s[...]


def _ffn_moe(xs, tile_expert, w_gu_b, w_down_b, tm, tf=512):
    r, d = xs.shape
    f = w_down_b.shape[1]
    nj = f // tf
    return pl.pallas_call(
        _ffn_moe_kernel,
        out_shape=jax.ShapeDtypeStruct((r, d), F32),
        grid_spec=pltpu.PrefetchScalarGridSpec(
            num_scalar_prefetch=1,
            grid=(r // tm, nj),
            in_specs=[
                pl.BlockSpec((tm, d), lambda i, j, te: (i, 0)),
                pl.BlockSpec((None, d, tf), lambda i, j, te: (te[i], 0, j)),
                pl.BlockSpec((None, d, tf), lambda i, j, te: (te[i], 0, nj + j)),
                pl.BlockSpec((None, tf, d), lambda i, j, te: (te[i], j, 0)),
            ],
            out_specs=pl.BlockSpec((tm, d), lambda i, j, te: (i, 0)),
            scratch_shapes=[pltpu.VMEM((tm, d), BF16), pltpu.VMEM((tm, d), F32)],
        ),
        compiler_params=_cparams(("arbitrary", "arbitrary")),
        name="moe_expert_ffn",
    )(tile_expert, xs, w_gu_b, w_gu_b, w_down_b)


def _inproj_kernel(x_ref, ms_ref, gain_ref, w_ref, wdt_ref, zx_ref, dt_ref, hm_s):
    j = pl.program_id(1)

    @pl.when(j == 0)
    def _():
        ms = ms_ref[0]
        hm = _rms_mod(x_ref[...], gain_ref[...], ms[1:2], ms[0:1]).astype(BF16)
        hm_s[...] = hm
        dt_ref[...] = _dot(hm, wdt_ref[...])

    zx_ref[...] = _dot(hm_s[...], w_ref[...]).astype(BF16)


def _ssm_in_proj(x2, ms, gain, w_zx_b, w_dt_b, seq, tm=512, tn=1024):
    t, d = x2.shape
    n = w_zx_b.shape[1]
    tpb = seq // tm
    return pl.pallas_call(
        _inproj_kernel,
        out_shape=(jax.ShapeDtypeStruct((t, n), BF16), jax.ShapeDtypeStruct((t, LANES), F32)),
        grid=(t // tm, n // tn),
        in_specs=[
            pl.BlockSpec((tm, d), lambda i, j: (i, 0)),
            pl.BlockSpec((1, 2, d), lambda i, j: (i // tpb, 0, 0)),
            pl.BlockSpec((1, d), lambda i, j: (0, 0)),
            pl.BlockSpec((d, tn), lambda i, j: (0, j)),
            pl.BlockSpec((d, LANES), lambda i, j: (0, 0)),
        ],
        out_specs=(pl.BlockSpec((tm, tn), lambda i, j: (i, j)),
                   pl.BlockSpec((tm, LANES), lambda i, j: (i, 0))),
        scratch_shapes=[pltpu.VMEM((tm, d), BF16)],
        compiler_params=_cparams(("arbitrary", "arbitrary")),
        name="ssm_in_proj",
    )(x2, ms, gain, w_zx_b, w_dt_b)


def _conv_silu(cur_ref, tail_s, w_ref, b_ref, out_s, width):
    rows, cols = cur_ref.shape
    rowi = lax.broadcasted_iota(jnp.int32, (SUBLANES, width), 0)
    for c0 in range(0, cols, width):
        cs = slice(c0, c0 + width)
        cur = cur_ref[:, cs].astype(F32)
        tail = tail_s[:, cs]
        acc = cur * w_ref[CONV_WIDTH - 1:CONV_WIDTH, cs] + b_ref[:, cs]
        for s in range(1, CONV_WIDTH):
            r = pltpu.roll(cur, s, 0)
            rt = pltpu.roll(tail, s, 0)
            head = jnp.where(rowi < s, rt, r[:SUBLANES])
            sh = jnp.concatenate([head, r[SUBLANES:]], axis=0)
            acc = acc + sh * w_ref[CONV_WIDTH - 1 - s:CONV_WIDTH - s, cs]
        tail_s[:, cs] = cur[rows - SUBLANES:]
        out_s[:, cs] = acc * _sigmoid(acc)


def _ssd_kernel(z_ref, xs_ref, bc_ref, dt_ref, cwx_ref, cbx_ref, cwbc_ref, cbbc_ref, dtb_ref,
                alog_ref, dskip_ref, nw_ref, r64_ref, y_ref,
                tailx_s, tailbc_s, state_s, xact_s, bcact_s, ybuf_s):
    L = SSM_CHUNK
    gw = D_INNER // N_SSM_GROUPS
    hpg = N_SSM_HEADS // N_SSM_GROUPS

    @pl.when(pl.program_id(1) == 0)
    def _():
        tailx_s[...] = jnp.zeros_like(tailx_s)
        tailbc_s[...] = jnp.zeros_like(tailbc_s)
        state_s[...] = jnp.zeros_like(state_s)

    _conv_silu(xs_ref, tailx_s, cwx_ref, cbx_ref, xact_s, 512)
    _conv_silu(bc_ref, tailbc_s, cwbc_ref, cbbc_ref, bcact_s, 512)

    dtv = jax.nn.softplus(dt_ref[...] + dtb_ref[...])
    a = -jnp.exp(alog_ref[...])
    dta = dtv * a
    ri = lax.broadcasted_iota(jnp.int32, (L, L), 0)
    ci = lax.broadcasted_iota(jnp.int32, (L, L), 1)
    causal = ri >= ci
    tri = jnp.where(causal, 1.0, 0.0).astype(BF16)
    p0, p1, p2 = _split3(dta)
    acs = _dot(tri, p0) + _dot(tri, p1) + _dot(tri, p2)
    acs_t = acs.T
    eacs = jnp.exp(acs)
    dte = jnp.exp(acs[L - 1:L, :] - acs)
    stacked = jnp.concatenate([dtv, eacs, dte], axis=0)
    s_hi, s_lo = _split2(stacked)
    expd = _dot(s_hi, r64_ref[...]) + _dot(s_lo, r64_ref[...])

    lo = lax.broadcasted_iota(jnp.int32, (L, LANES), 1) < (LANES // 2)
    for g in range(N_SSM_GROUPS):
        gs = slice(g * gw, (g + 1) * gw)
        b_g = bcact_s[:, g * D_STATE:(g + 1) * D_STATE]
        c_g = bcact_s[:, (N_SSM_GROUPS + g) * D_STATE:(N_SSM_GROUPS + g + 1) * D_STATE]
        b_t = b_g.T.astype(BF16)
        c_b = c_g.astype(BF16)
        cb = _dot(c_b, b_t)
        xs_g = xact_s[:, gs]
        xd = xs_g * expd[0:L, gs]
        xd_b = xd.astype(BF16)
        st_prev = state_s[g]
        y_off = _dot(c_b, st_prev.astype(BF16)) * expd[L:2 * L, gs]
        xdw = (xd * expd[2 * L:3 * L, gs]).astype(BF16)
        state_s[g] = expd[2 * L - 1:2 * L, gs] * st_prev + _dot(b_t, xdw)
        for jp in range(hpg // 2):
            ps = slice(jp * LANES, (jp + 1) * LANES)
            slab = xd_b[:, ps]
            halves = []
            for half in range(2):
                hh = g * hpg + 2 * jp + half
                diff = acs[:, hh:hh + 1] - acs_t[hh:hh + 1, :]
                dec = jnp.exp(jnp.where(causal, diff, NEG_INF))
                halves.append(_dot((cb * dec).astype(BF16), slab))
            y = jnp.where(lo, halves[0], halves[1]) + y_off[:, ps]
            cols = slice(g * gw + jp * LANES, g * gw + (jp + 1) * LANES)
            ybuf_s[:, cols] = y + dskip_ref[:, cols] * xs_g[:, ps]

    z = z_ref[...].astype(F32)
    yg = ybuf_s[...] * (z * _sigmoid(z))
    ms = jnp.mean(yg * yg, axis=-1, keepdims=True)
    y_ref[...] = (yg * lax.rsqrt(ms + NORM_EPS) * nw_ref[...]).astype(BF16)


def _ssd(zx, dt, conv_w, conv_b, dt_bias, a_log, d_skip, norm_w, batch, seq):
    t = zx.shape[0]
    L = SSM_CHUNK
    nc = seq // L
    di, dbc = D_INNER, D_BC
    pad = LANES - N_SSM_HEADS
    cwx, cwbc = conv_w[:, :di].astype(F32), conv_w[:, di:].astype(F32)
    cbx, cbbc = conv_b[None, :di].astype(F32), conv_b[None, di:].astype(F32)
    dtb = jnp.pad(dt_bias.astype(F32), (0, pad))[None]
    alog = jnp.pad(a_log.astype(F32), (0, pad))[None]
    dskip = jnp.repeat(d_skip.astype(F32), di // N_SSM_HEADS)[None]
    nw = norm_w.astype(F32)[None]
    r64 = jnp.asarray((np.arange(LANES)[:, None] == (np.arange(di)[None, :] // (di // N_SSM_HEADS)))
                      .astype(np.float32)).astype(BF16)
    row = lambda b, c: (b * nc + c, 0)
    const = lambda b, c: (0, 0)
    return pl.pallas_call(
        _ssd_kernel,
        out_shape=jax.ShapeDtypeStruct((t, di), BF16),
        grid=(batch, nc),
        in_specs=[
            pl.BlockSpec((L, di), lambda b, c: (b * nc + c, 0)),
            pl.BlockSpec((L, di), lambda b, c: (b * nc + c, 1)),
            pl.BlockSpec((L, dbc), lambda b, c: (b * nc + c, 2 * di // dbc)),
            pl.BlockSpec((L, LANES), row),
            pl.BlockSpec((CONV_WIDTH, di), const),
            pl.BlockSpec((1, di), const),
            pl.BlockSpec((CONV_WIDTH, dbc), const),
            pl.BlockSpec((1, dbc), const),
            pl.BlockSpec((1, LANES), const),
            pl.BlockSpec((1, LANES), const),
            pl.BlockSpec((1, di), const),
            pl.BlockSpec((1, di), const),
            pl.BlockSpec((LANES, di), const),
        ],
        out_specs=pl.BlockSpec((L, di), row),
        scratch_shapes=[
            pltpu.VMEM((SUBLANES, di), F32), pltpu.VMEM((SUBLANES, dbc), F32),
            pltpu.VMEM((N_SSM_GROUPS, D_STATE, di // N_SSM_GROUPS), F32),
            pltpu.VMEM((L, di), F32), pltpu.VMEM((L, dbc), F32), pltpu.VMEM((L, di), F32),
        ],
        compiler_params=_cparams(("arbitrary", "arbitrary")),
        name="ssd_scan",
    )(zx, zx, zx, dt, cwx, cbx, cwbc, cbbc, dtb, alog, dskip, nw, r64)


def _router_kernel(x_ref, y_ref, wout_ref, mv_ref, gain_ref, wrh_ref, wrl_ref,
                   x1_ref, hf_ref, rinfo_ref, cnt_ref, carry_s):
    tm = x_ref.shape[0]

    @pl.when(pl.program_id(0) == 0)
    def _():
        carry_s[...] = jnp.zeros_like(carry_s)

    mv = mv_ref[0]
    x1 = x_ref[...] + mv[0:1] * _dot(y_ref[...], wout_ref[...])
    x1_ref[...] = x1
    hf = _rms_mod(x1, gain_ref[...], mv[2:3], mv[1:2])
    hf_ref[...] = hf
    h_hi, h_lo = _split2(hf)
    logits = _dot(h_hi, wrh_ref[...]) + _dot(h_lo, wrh_ref[...]) + _dot(h_hi, wrl_ref[...])
    lane = lax.broadcasted_iota(jnp.int32, (tm, LANES), 1)
    lanef = lane.astype(F32)
    lg = jnp.where(lane < N_EXPERTS, logits, NEG_INF)
    v0 = jnp.max(lg, axis=-1, keepdims=True)
    i0 = jnp.min(jnp.where(lg == v0, lanef, float(LANES)), axis=-1, keepdims=True)
    lg1 = jnp.where(lanef == i0, NEG_INF, lg)
    v1 = jnp.max(lg1, axis=-1, keepdims=True)
    i1 = jnp.min(jnp.where(lg1 == v1, lanef, float(LANES)), axis=-1, keepdims=True)
    e1 = jnp.exp(v1 - v0)
    w0 = 1.0 / (1.0 + e1)
    w1 = e1 / (1.0 + e1)
    oh0 = jnp.where(lanef == i0, 1.0, 0.0)
    oh1 = jnp.where(lanef == i1, 1.0, 0.0)
    ri = lax.broadcasted_iota(jnp.int32, (tm, tm), 0)
    ci = lax.broadcasted_iota(jnp.int32, (tm, tm), 1)
    before = jnp.where(ri > ci, 1.0, 0.0).astype(BF16)
    pre0 = _dot(before, oh0.astype(BF16))
    pre1 = _dot(before, oh1.astype(BF16))
    cnt0 = jnp.sum(oh0, axis=0, keepdims=True)
    cnt1 = jnp.sum(oh1, axis=0, keepdims=True)
    carry = carry_s[...]
    rank0 = jnp.sum(oh0 * (pre0 + carry), axis=-1, keepdims=True)
    rank1 = jnp.sum(oh1 * (pre1 + carry + cnt0), axis=-1, keepdims=True)
    total = carry + cnt0 + cnt1
    carry_s[...] = total
    cnt_ref[...] = total
    rinfo = jnp.where(lane == 0, i0, jnp.where(lane == 1, i1, jnp.where(
        lane == 2, rank0, jnp.where(lane == 3, rank1, jnp.where(
            lane == 4, w0, jnp.where(lane == 5, w1, 0.0))))))
    rinfo_ref[...] = rinfo


def _router(x2, y, wout_b, mv, gain, wr_hi, wr_lo, seq, tm=512):
    t, d = x2.shape
    tpb = seq // tm
    return pl.pallas_call(
        _router_kernel,
        out_shape=(jax.ShapeDtypeStruct((t, d), F32), jax.ShapeDtypeStruct((t, d), F32),
                   jax.ShapeDtypeStruct((t, LANES), F32), jax.ShapeDtypeStruct((1, LANES), F32)),
        grid=(t // tm,),
        in_specs=[
            pl.BlockSpec((tm, d), lambda i: (i, 0)),
            pl.BlockSpec((tm, y.shape[1]), lambda i: (i, 0)),
            pl.BlockSpec(wout_b.shape, lambda i: (0, 0)),
            pl.BlockSpec((1, 3, d), lambda i: (i // tpb, 0, 0)),
            pl.BlockSpec((1, d), lambda i: (0, 0)),
            pl.BlockSpec((d, LANES), lambda i: (0, 0)),
            pl.BlockSpec((d, LANES), lambda i: (0, 0)),
        ],
        out_specs=(pl.BlockSpec((tm, d), lambda i: (i, 0)),
                   pl.BlockSpec((tm, d), lambda i: (i, 0)),
                   pl.BlockSpec((tm, LANES), lambda i: (i, 0)),
                   pl.BlockSpec((1, LANES), lambda i: (0, 0))),
        scratch_shapes=[pltpu.VMEM((1, LANES), F32)],
        compiler_params=_cparams(("arbitrary",)),
        name="ssm_out_router",
    )(x2, y, wout_b, mv, gain, wr_hi, wr_lo)


def _row_copy(src, src_row, dst, dst_row, sem):
    return pltpu.make_async_copy(src.at[pl.ds(src_row, 1)], dst.at[pl.ds(dst_row, 1)], sem)


def _dispatch_kernel(pos_ref, hf_ref, sorted_in_ref, sorted_ref, sem):
    del sorted_in_ref
    tm = hf_ref.shape[0]
    base = pl.program_id(0) * (TOP_K * tm)

    def issue(t, carry):
        for k in range(TOP_K):
            _row_copy(hf_ref, t, sorted_ref, pos_ref[base + TOP_K * t + k], sem).start()
        return carry

    lax.fori_loop(0, tm, issue, 0)

    def drain(t, carry):
        for k in range(TOP_K):
            _row_copy(hf_ref, t, sorted_ref, pos_ref[base + TOP_K * t + k], sem).wait()
        return carry

    lax.fori_loop(0, tm, drain, 0)


def _dispatch(pos_flat, hf, n_rows, tm=512):
    t, d = hf.shape
    zeros = jnp.zeros((n_rows, d), hf.dtype)
    return pl.pallas_call(
        _dispatch_kernel,
        out_shape=jax.ShapeDtypeStruct((n_rows, d), hf.dtype),
        grid_spec=pltpu.PrefetchScalarGridSpec(
            num_scalar_prefetch=1,
            grid=(t // tm,),
            in_specs=[pl.BlockSpec((tm, d), lambda i, pos: (i, 0)),
                      pl.BlockSpec(memory_space=pl.ANY)],
            out_specs=pl.BlockSpec(memory_space=pl.ANY),
            scratch_shapes=[pltpu.SemaphoreType.DMA],
        ),
        input_output_aliases={2: 0},
        compiler_params=_cparams(("arbitrary",)),
        name="moe_dispatch",
    )(pos_flat, hf, zeros)


def _combine_kernel(pos_ref, x1_ref, rinfo_ref, g2_ref, ys_ref, out_ref, buf, sem):
    tm = x1_ref.shape[0]
    base = pl.program_id(0) * (TOP_K * tm)

    def issue(t, carry):
        for k in range(TOP_K):
            _row_copy(ys_ref, pos_ref[base + TOP_K * t + k], buf.at[k], t, sem).start()
        return carry

    lax.fori_loop(0, tm, issue, 0)

    def drain(t, carry):
        for k in range(TOP_K):
            _row_copy(ys_ref, pos_ref[base + TOP_K * t + k], buf.at[k], t, sem).wait()
        return carry

    lax.fori_loop(0, tm, drain, 0)
    rinfo = rinfo_ref[...]
    ffn = rinfo[:, 4:5] * buf[0] + rinfo[:, 5:6] * buf[1]
    out_ref[...] = x1_ref[...] + g2_ref[0] * ffn


def _combine(pos_flat, x1, rinfo, g2, ys, seq, tm=256):
    t, d = x1.shape
    tpb = seq // tm
    return pl.pallas_call(
        _combine_kernel,
        out_shape=jax.ShapeDtypeStruct((t, d), F32),
        grid_spec=pltpu.PrefetchScalarGridSpec(
            num_scalar_prefetch=1,
            grid=(t // tm,),
            in_specs=[pl.BlockSpec((tm, d), lambda i, pos: (i, 0)),
                      pl.BlockSpec((tm, LANES), lambda i, pos: (i, 0)),
                      pl.BlockSpec((1, 1, d), lambda i, pos: (i // tpb, 0, 0)),
                      pl.BlockSpec(memory_space=pl.ANY)],
            out_specs=pl.BlockSpec((tm, d), lambda i, pos: (i, 0)),
            scratch_shapes=[pltpu.VMEM((TOP_K, tm, d), F32), pltpu.SemaphoreType.DMA],
        ),
        compiler_params=_cparams(("arbitrary",)),
        name="moe_combine",
    )(pos_flat, x1, rinfo, g2, ys)


def _attn_layer(x2, mod_i, gain_i, bias, w_qkv, q_gain, k_gain, sinks, w_o, w_gu, w_down,
                batch, seq):
    d = D_MODEL
    sh1, sc1, g1, sh2, sc2, g2 = [mod_i[:, k * d:(k + 1) * d] for k in range(6)]
    perm = np.zeros((N_Q_HEADS,), np.int32)
    for gp in range(N_KV_HEADS // 2):
        for qi in range(Q_PER_KV):
            for a in range(2):
                perm[(gp * Q_PER_KV + qi) * 2 + a] = (2 * gp + a) * Q_PER_KV + qi
    col = (perm[:, None] * HEAD_DIM + np.arange(HEAD_DIM)[None, :]).reshape(-1)
    nq = N_Q_HEADS * HEAD_DIM
    w_qkv_b = jnp.concatenate([w_qkv[:, :nq][:, col], w_qkv[:, nq:]], axis=1).astype(BF16)
    w_o_b = w_o[col, :].astype(BF16)
    qg2 = jnp.tile(q_gain.astype(F32), 2)[None]
    kg2 = jnp.tile(k_gain.astype(F32), 2)[None]
    q, k, v = _qkv_proj(x2, jnp.stack([sh1, sc1], axis=1), gain_i[0:1], w_qkv_b, qg2, kg2, seq)
    o = _attention(q, k, v, bias, sinks, batch, seq)
    mv = jnp.stack([g1, sh2, sc2, g2], axis=1)
    return _ffn_dense(x2, o, w_o_b, mv, gain_i[1:2], w_gu.astype(BF16), w_down.astype(BF16), seq)


def _ssm_layer(x2, mod_i, gain_i, w_in, conv_w, conv_b, dt_bias, a_log, d_skip, norm_w, w_out,
               w_router, w_gu, w_down, batch, seq, moe_tm=512):
    d = D_MODEL
    t = x2.shape[0]
    sh1, sc1, g1, sh2, sc2, g2 = [mod_i[:, k * d:(k + 1) * d] for k in range(6)]
    n_zx = 2 * D_INNER + D_BC
    w_zx_b = w_in[:, :n_zx].astype(BF16)
    w_dt_b = jnp.pad(w_in[:, n_zx:], ((0, 0), (0, LANES - N_SSM_HEADS))).astype(BF16)
    zx, dt = _ssm_in_proj(x2, jnp.stack([sh1, sc1], axis=1), gain_i[0:1], w_zx_b, w_dt_b, seq)
    y = _ssd(zx, dt, conv_w, conv_b, dt_bias, a_log, d_skip, norm_w, batch, seq)

    wr = jnp.pad(w_router.astype(F32), ((0, 0), (0, LANES - N_EXPERTS)))
    wr_hi = wr.astype(BF16)
    wr_lo = (wr - wr_hi.astype(F32)).astype(BF16)
    x1, hf, rinfo, cnt = _router(x2, y, w_out.astype(BF16), jnp.stack([g1, sh2, sc2], axis=1),
                                 gain_i[1:2], wr_hi, wr_lo, seq)

    counts = cnt[0, :N_EXPERTS].astype(jnp.int32)
    padded = ((counts + moe_tm - 1) // moe_tm) * moe_tm
    ends = jnp.cumsum(padded)
    base = ends - padded
    eidx = rinfo[:, 0:TOP_K].astype(jnp.int32)
    rank = rinfo[:, 2:2 + TOP_K].astype(jnp.int32)
    pos = jnp.sum(jnp.where(eidx[..., None] == jnp.arange(N_EXPERTS), base, 0), axis=-1) + rank
    pos_flat = pos.reshape(-1)
    n_tiles = (TOP_K * t) // moe_tm + N_EXPERTS
    tile_start = jnp.arange(n_tiles, dtype=jnp.int32) * moe_tm
    tile_expert = jnp.minimum(jnp.sum(tile_start[:, None] >= ends[None, :], axis=-1),
                              N_EXPERTS - 1).astype(jnp.int32)

    xs = _dispatch(pos_flat, hf, n_tiles * moe_tm)
    ys = _ffn_moe(xs, tile_expert, w_gu.astype(BF16), w_down.astype(BF16), moe_tm)
    return _combine(pos_flat, x1, rinfo, g2[:, None, :], ys, seq)


def kernel(x, c, ada_w, ada_b, norm_gain, rel_bias, attn_w_qkv, attn_q_gain, attn_k_gain, attn_sinks, attn_w_o, ssm_w_in, ssm_conv_w, ssm_conv_b, ssm_dt_bias, ssm_a_log, ssm_d, ssm_norm_w, ssm_w_out, ffn_w_gu, ffn_w_down, moe_w_router, moe_w_gu, moe_w_down):
    batch, seq, d = x.shape
    mod = _adaln_mod(c.astype(F32), ada_w.astype(F32), ada_b.astype(F32))
    bias = _bias_table(rel_bias)
    x2 = x.reshape(batch * seq, d).astype(F32)
    for i in range(DEPTH):
        j = i // 2
        gain_i = norm_gain[i].astype(F32)
        if i % 2 == 0:
            x2 = _attn_layer(x2, mod[i], gain_i, bias, attn_w_qkv[j], attn_q_gain[j],
                             attn_k_gain[j], attn_sinks[j], attn_w_o[j], ffn_w_gu[j],
                             ffn_w_down[j], batch, seq)
        else:
            x2 = _ssm_layer(x2, mod[i], gain_i, ssm_w_in[j], ssm_conv_w[j], ssm_conv_b[j],
                            ssm_dt_bias[j], ssm_a_log[j], ssm_d[j], ssm_norm_w[j], ssm_w_out[j],
                            moe_w_router[j], moe_w_gu[j], moe_w_down[j], batch, seq)
    return x2.reshape(batch, seq, d).astype(x.dtype)
```

```python
import functools
import math

import numpy as np
import jax
import jax.numpy as jnp
from jax import lax
from jax.experimental import pallas as pl
from jax.experimental.pallas import tpu as pltpu

F32 = jnp.float32
BF16 = jnp.bfloat16

D_MODEL = 1024
DEPTH = 4
N_Q_HEADS = 16
N_KV_HEADS = 4
HEAD_DIM = 64
Q_PER_KV = 4
ATTN_BLOCK = 128
N_BUCKETS = 32
MAX_DISTANCE = 128
D_INNER = 2048
N_SSM_HEADS = 32
N_SSM_GROUPS = 4
D_STATE = 128
CONV_WIDTH = 4
SSM_CHUNK = 128
D_BC = 2 * N_SSM_GROUPS * D_STATE
D_FF = 3584
N_EXPERTS = 8
TOP_K = 2
NORM_EPS = 1e-6

LANES = 128
SUBLANES = 8
VMEM_LIMIT = 56 * 1024 * 1024
NEG_INF = float("-inf")


def _cparams(sem):
    return pltpu.CompilerParams(dimension_semantics=sem, vmem_limit_bytes=VMEM_LIMIT)


def _sigmoid(v):
    return 1.0 / (1.0 + jnp.exp(-v))


def _rms_mod(x, gain, scale, shift):
    ms = jnp.mean(x * x, axis=-1, keepdims=True)
    y = x * lax.rsqrt(ms + NORM_EPS)
    return (y * gain) * (1.0 + scale) + shift


def _split2(v):
    hi = v.astype(BF16)
    lo = (v - hi.astype(F32)).astype(BF16)
    return hi, lo


def _split3(v):
    hi = v.astype(BF16)
    r = v - hi.astype(F32)
    mid = r.astype(BF16)
    lo = (r - mid.astype(F32)).astype(BF16)
    return hi, mid, lo


def _dot(a, b):
    return jnp.dot(a, b, preferred_element_type=F32)


def _mod_kernel(c_ref, w_ref, b_ref, o_ref):
    c = c_ref[...]
    cond = c * _sigmoid(c)
    o_ref[0] = _dot(cond.astype(BF16), w_ref[0].astype(BF16)) + b_ref[0]


def _adaln_mod(c, ada_w, ada_b):
    depth, d, n = ada_w.shape
    b = c.shape[0]
    tn = 1024
    c_pad = jnp.zeros((SUBLANES, d), F32).at[:b].set(c)
    out = pl.pallas_call(
        _mod_kernel,
        out_shape=jax.ShapeDtypeStruct((depth, SUBLANES, n), F32),
        grid=(depth, n // tn),
        in_specs=[
            pl.BlockSpec((SUBLANES, d), lambda i, j: (0, 0)),
            pl.BlockSpec((1, d, tn), lambda i, j: (i, 0, j)),
            pl.BlockSpec((1, 1, tn), lambda i, j: (i, 0, j)),
        ],
        out_specs=pl.BlockSpec((1, SUBLANES, tn), lambda i, j: (i, 0, j)),
        compiler_params=_cparams(("arbitrary", "arbitrary")),
        name="adaln_mod",
    )(c_pad, ada_w, ada_b.reshape(depth, 1, n))
    return out[:, :b]


def _t5_bucket_table():
    qi = np.arange(ATTN_BLOCK)[:, None]
    kj = np.arange(2 * ATTN_BLOCK)[None, :]
    dist = qi + ATTN_BLOCK - kj
    max_exact = N_BUCKETS // 2
    d = np.maximum(dist, 0)
    df = np.maximum(d, 1).astype(np.float64)
    large = max_exact + (np.log(df / max_exact) / math.log(MAX_DISTANCE / max_exact)
                         * (N_BUCKETS - max_exact)).astype(np.int32)
    large = np.minimum(large, N_BUCKETS - 1)
    bucket = np.where(d < max_exact, d, large)
    valid = (dist >= 0) & (dist < ATTN_BLOCK)
    return np.where(valid, bucket, -1).astype(np.int32)


def _bias_kernel(rb_ref, bkt_ref, o_ref):
    j = pl.program_id(0)
    gp = j // Q_PER_KV
    qi = j % Q_PER_KV
    bkt = bkt_ref[...]
    for a in range(2):
        h = (2 * gp + a) * Q_PER_KV + qi
        acc = jnp.full(bkt.shape, NEG_INF, F32)
        for b in range(N_BUCKETS):
            acc = jnp.where(bkt == b, rb_ref[b, h], acc)
        o_ref[0, a * ATTN_BLOCK:(a + 1) * ATTN_BLOCK, :] = acc


def _bias_table(rel_bias):
    bkt = jnp.asarray(_t5_bucket_table())
    n_slots = N_Q_HEADS // 2
    return pl.pallas_call(
        _bias_kernel,
        out_shape=jax.ShapeDtypeStruct((n_slots, 2 * ATTN_BLOCK, 2 * ATTN_BLOCK), F32),
        grid=(n_slots,),
        in_specs=[
            pl.BlockSpec(memory_space=pltpu.SMEM),
            pl.BlockSpec((ATTN_BLOCK, 2 * ATTN_BLOCK), lambda j: (0, 0)),
        ],
        out_specs=pl.BlockSpec((1, 2 * ATTN_BLOCK, 2 * ATTN_BLOCK), lambda j: (j, 0, 0)),
        compiler_params=_cparams(("arbitrary",)),
        name="t5_bias_table",
    )(rel_bias.astype(F32), bkt)


def _qkv_kernel(x_ref, ms_ref, gain_ref, w_ref, qg_ref, kg_ref, q_ref, k_ref, v_ref):
    tm = x_ref.shape[0]
    ms = ms_ref[0]
    hm = _rms_mod(x_ref[...], gain_ref[...], ms[1:2], ms[0:1]).astype(BF16)
    lo = lax.broadcasted_iota(jnp.int32, (tm, LANES), 1) < HEAD_DIM

    def head_norm(slab, g):
        sq = slab * slab
        s_lo = jnp.sum(jnp.where(lo, sq, 0.0), axis=-1, keepdims=True)
        s_hi = jnp.sum(jnp.where(lo, 0.0, sq), axis=-1, keepdims=True)
        msq = jnp.where(lo, s_lo, s_hi) * (1.0 / HEAD_DIM)
        return slab * lax.rsqrt(msq + NORM_EPS) * g

    nq = N_Q_HEADS * HEAD_DIM
    nk = N_KV_HEADS * HEAD_DIM
    q_all = _dot(hm, w_ref[:, :nq])
    for j in range(nq // LANES):
        sl = slice(j * LANES, (j + 1) * LANES)
        q_ref[:, sl] = (head_norm(q_all[:, sl], qg_ref[...]) * (HEAD_DIM ** -0.5)).astype(BF16)
    k_all = _dot(hm, w_ref[:, nq:nq + nk])
    for j in range(nk // LANES):
        sl = slice(j * LANES, (j + 1) * LANES)
        k_ref[:, sl] = head_norm(k_all[:, sl], kg_ref[...]).astype(BF16)
    v_ref[...] = _dot(hm, w_ref[:, nq + nk:]).astype(BF16)


def _qkv_proj(x2, ms, gain, w_qkv_b, qg2, kg2, seq, tm=512):
    t, d = x2.shape
    nq = N_Q_HEADS * HEAD_DIM
    nk = N_KV_HEADS * HEAD_DIM
    tpb = seq // tm
    return pl.pallas_call(
        _qkv_kernel,
        out_shape=(jax.ShapeDtypeStruct((t, nq), BF16),
                   jax.ShapeDtypeStruct((t, nk), BF16),
                   jax.ShapeDtypeStruct((t, nk), BF16)),
        grid=(t // tm,),
        in_specs=[
            pl.BlockSpec((tm, d), lambda i: (i, 0)),
            pl.BlockSpec((1, 2, d), lambda i: (i // tpb, 0, 0)),
            pl.BlockSpec((1, d), lambda i: (0, 0)),
            pl.BlockSpec((d, nq + 2 * nk), lambda i: (0, 0)),
            pl.BlockSpec((1, LANES), lambda i: (0, 0)),
            pl.BlockSpec((1, LANES), lambda i: (0, 0)),
        ],
        out_specs=(pl.BlockSpec((tm, nq), lambda i: (i, 0)),
                   pl.BlockSpec((tm, nk), lambda i: (i, 0)),
                   pl.BlockSpec((tm, nk), lambda i: (i, 0))),
        compiler_params=_cparams(("arbitrary",)),
        name="attn_qkv",
    )(x2, ms, gain, w_qkv_b, qg2, kg2)


def _attn_kernel(sink_ref, q_ref, kc_ref, kp_ref, vc_ref, vp_ref, bias_ref, o_ref):
    blk = ATTN_BLOCK
    first = pl.program_id(1) == 0
    lane = lax.broadcasted_iota(jnp.int32, (blk, LANES), 1)
    lo = lane < HEAD_DIM
    mask_lo = jnp.where(lo, 1.0, 0.0).astype(BF16)
    mask_hi = jnp.where(lo, 0.0, 1.0).astype(BF16)
    row = lax.broadcasted_iota(jnp.int32, (2 * blk, 1), 0)
    nt = (((1,), (1,)), ((), ()))
    for gp in range(N_KV_HEADS // 2):
        ksl = slice(gp * LANES, (gp + 1) * LANES)
        kc = kc_ref[:, ksl]
        kp = kp_ref[:, ksl]
        vc = vc_ref[:, ksl]
        vp = vp_ref[:, ksl]
        for qi in range(Q_PER_KV):
            j = gp * Q_PER_KV + qi
            sl = slice(j * LANES, (j + 1) * LANES)
            qs = q_ref[:, sl]
            q2 = jnp.concatenate([qs * mask_lo, qs * mask_hi], axis=0)
            s_p = lax.dot_general(q2, kp, nt, preferred_element_type=F32)
            s_c = lax.dot_general(q2, kc, nt, preferred_element_type=F32)
            bias = bias_ref[j]
            s_p = jnp.where(first, NEG_INF, s_p + bias[:, :blk])
            s_c = s_c + bias[:, blk:]
            h_a = (2 * gp) * Q_PER_KV + qi
            h_b = (2 * gp + 1) * Q_PER_KV + qi
            sink = jnp.where(row < blk, sink_ref[h_a], sink_ref[h_b])
            m = jnp.maximum(jnp.max(jnp.maximum(s_p, s_c), axis=-1, keepdims=True), sink)
            p_p = jnp.exp(s_p - m)
            p_c = jnp.exp(s_c - m)
            den = jnp.sum(p_p + p_c, axis=-1, keepdims=True) + jnp.exp(sink - m)
            pv = _dot(p_p.astype(BF16), vp) + _dot(p_c.astype(BF16), vc)
            pv = pv / den
            o_ref[:, sl] = jnp.where(lo, pv[:blk], pv[blk:]).astype(BF16)


def _attention(q, k, v, bias, sinks, batch, seq):
    t, nq = q.shape
    nk = k.shape[1]
    blk = ATTN_BLOCK
    nb = seq // blk
    cur = lambda b, n, *_: (b * nb + n, 0)
    prev = lambda b, n, *_: (b * nb + jnp.maximum(n - 1, 0), 0)
    return pl.pallas_call(
        _attn_kernel,
        out_shape=jax.ShapeDtypeStruct((t, nq), BF16),
        grid=(batch, nb),
        in_specs=[
            pl.BlockSpec(memory_space=pltpu.SMEM),
            pl.BlockSpec((blk, nq), cur),
            pl.BlockSpec((blk, nk), cur),
            pl.BlockSpec((blk, nk), prev),
            pl.BlockSpec((blk, nk), cur),
            pl.BlockSpec((blk, nk), prev),
            pl.BlockSpec(bias.shape, lambda b, n: (0, 0, 0)),
        ],
        out_specs=pl.BlockSpec((blk, nq), cur),
        compiler_params=_cparams(("arbitrary", "arbitrary")),
        name="swa_attention",
    )(sinks.astype(F32), q, k, k, v, v, bias)


def _swiglu_step(hf, wg_ref, wu_ref, wd_ref, acc_s):
    g = _dot(hf, wg_ref[...])
    u = _dot(hf, wu_ref[...])
    a = (g * _sigmoid(g) * u).astype(BF16)
    acc_s[...] += _dot(a, wd_ref[...])


def _ffn_dense_kernel(x_ref, o_ref, wo_ref, mv_ref, gain_ref, wg_ref, wu_ref, wd_ref,
                      out_ref, hf_s, acc_s):
    j = pl.program_id(1)

    @pl.when(j == 0)
    def _():
        mv = mv_ref[0]
        x1 = x_ref[...] + mv[0:1] * _dot(o_ref[...], wo_ref[...])
        out_ref[...] = x1
        hf_s[...] = _rms_mod(x1, gain_ref[...], mv[2:3], mv[1:2]).astype(BF16)
        acc_s[...] = jnp.zeros_like(acc_s)

    _swiglu_step(hf_s[...], wg_ref, wu_ref, wd_ref, acc_s)

    @pl.when(j == pl.num_programs(1) - 1)
    def _():
        out_ref[...] = out_ref[...] + mv_ref[0][3:4] * acc_s[...]


def _ffn_dense(x2, o, wo_b, mv, gain, w_gu_b, w_down_b, seq, tm=1024, tf=512):
    t, d = x2.shape
    f = w_down_b.shape[0]
    nj = f // tf
    tpb = seq // tm
    return pl.pallas_call(
        _ffn_dense_kernel,
        out_shape=jax.ShapeDtypeStruct((t, d), F32),
        grid=(t // tm, nj),
        in_specs=[
            pl.BlockSpec((tm, d), lambda i, j: (i, 0)),
            pl.BlockSpec((tm, o.shape[1]), lambda i, j: (i, 0)),
            pl.BlockSpec(wo_b.shape, lambda i, j: (0, 0)),
            pl.BlockSpec((1, 4, d), lambda i, j: (i // tpb, 0, 0)),
            pl.BlockSpec((1, d), lambda i, j: (0, 0)),
            pl.BlockSpec((d, tf), lambda i, j: (0, j)),
            pl.BlockSpec((d, tf), lambda i, j: (0, nj + j)),
            pl.BlockSpec((tf, d), lambda i, j: (j, 0)),
        ],
        out_specs=pl.BlockSpec((tm, d), lambda i, j: (i, 0)),
        scratch_shapes=[pltpu.VMEM((tm, d), BF16), pltpu.VMEM((tm, d), F32)],
        compiler_params=_cparams(("arbitrary", "arbitrary")),
        name="wo_ffn_dense",
    )(x2, o, wo_b, mv, gain, w_gu_b, w_gu_b, w_down_b)


def _ffn_moe_kernel(te_ref, xs_ref, wg_ref, wu_ref, wd_ref, out_ref, hf_s, acc_s):
    i = pl.program_id(0)
    j = pl.program_id(1)
    used = i < te_ref[pl.num_programs(0)]

    @pl.when(used & (j == 0))
    def _():
        hf_s[...] = xs_ref[...].astype(BF16)
        acc_s[...] = jnp.zeros_like(acc_s)

    @pl.when(used)
    def _():
        _swiglu_step(hf_s[...], wg_ref, wu_ref, wd_ref, acc_s)

    @pl.when(j == pl.num_programs(1) - 1)
    def _():
        @pl.when(used)
        def _():
            out_ref[...] = acc_s[...]

        @pl.when(jnp.logical_not(used))
        def _():
            out_ref[...] = jnp.zeros_like(out_ref)


def _ffn_moe(xs, tile_expert, w_gu_b, w_down_b, tm, tf=896):
    r, d = xs.shape
    f = w_down_b.shape[1]
    nj = f // tf
    nt = r // tm

    def jj(i, j, te):
        return jnp.where(i < te[nt], j, 0)

    return pl.pallas_call(
        _ffn_moe_kernel,
        out_shape=jax.ShapeDtypeStruct((r, d), F32),
        grid_spec=pltpu.PrefetchScalarGridSpec(
            num_scalar_prefetch=1,
            grid=(nt, nj),
            in_specs=[
                pl.BlockSpec((tm, d), lambda i, j, te: (i, 0)),
                pl.BlockSpec((None, d, tf), lambda i, j, te: (te[i], 0, jj(i, j, te))),
                pl.BlockSpec((None, d, tf), lambda i, j, te: (te[i], 0, nj + jj(i, j, te))),
                pl.BlockSpec((None, tf, d), lambda i, j, te: (te[i], jj(i, j, te), 0)),
            ],
            out_specs=pl.BlockSpec((tm, d), lambda i, j, te: (i, 0)),
            scratch_shapes=[pltpu.VMEM((tm, d), BF16), pltpu.VMEM((tm, d), F32)],
        ),
        compiler_params=_cparams(("arbitrary", "arbitrary")),
        name="moe_expert_ffn",
    )(tile_expert, xs, w_gu_b, w_gu_b, w_down_b)


def _inproj_kernel(x_ref, ms_ref, gain_ref, w_ref, wdt_ref, zx_ref, dt_ref, hm_s):
    j = pl.program_id(1)

    @pl.when(j == 0)
    def _():
        ms = ms_ref[0]
        hm = _rms_mod(x_ref[...], gain_ref[...], ms[1:2], ms[0:1]).astype(BF16)
        hm_s[...] = hm
        dt_ref[...] = _dot(hm, wdt_ref[...])

    zx_ref[...] = _dot(hm_s[...], w_ref[...]).astype(BF16)


def _ssm_in_proj(x2, ms, gain, w_zx_b, w_dt_b, seq, tm=1024, tn=1024):
    t, d = x2.shape
    n = w_zx_b.shape[1]
    tpb = seq // tm
    return pl.pallas_call(
        _inproj_kernel,
        out_shape=(jax.ShapeDtypeStruct((t, n), BF16), jax.ShapeDtypeStruct((t, LANES), F32)),
        grid=(t // tm, n // tn),
        in_specs=[
            pl.BlockSpec((tm, d), lambda i, j: (i, 0)),
            pl.BlockSpec((1, 2, d), lambda i, j: (i // tpb, 0, 0)),
            pl.BlockSpec((1, d), lambda i, j: (0, 0)),
            pl.BlockSpec((d, tn), lambda i, j: (0, j)),
            pl.BlockSpec((d, LANES), lambda i, j: (0, 0)),
        ],
        out_specs=(pl.BlockSpec((tm, tn), lambda i, j: (i, j)),
                   pl.BlockSpec((tm, LANES), lambda i, j: (i, 0))),
        scratch_shapes=[pltpu.VMEM((tm, d), BF16)],
        compiler_params=_cparams(("arbitrary", "arbitrary")),
        name="ssm_in_proj",
    )(x2, ms, gain, w_zx_b, w_dt_b)


def _conv_shift_matrix(rows):
    m = np.zeros(((CONV_WIDTH - 1) * rows, 2 * rows), np.float32)
    for s in range(1, CONV_WIDTH):
        for t in range(rows):
            m[(s - 1) * rows + t, rows + t - s] = 1.0
    return m


def _conv_silu(cur_ref, prev_s, shift_ref, w_ref, b_ref, out_s, width):
    rows, cols = cur_ref.shape
    for c0 in range(0, cols, width):
        cs = slice(c0, c0 + width)
        cur_b = cur_ref[:, cs]
        ext = jnp.concatenate([prev_s[:, cs], cur_b], axis=0)
        shifted = _dot(shift_ref[...], ext)
        acc = cur_b.astype(F32) * w_ref[CONV_WIDTH - 1:CONV_WIDTH, cs] + b_ref[:, cs]
        for s in range(1, CONV_WIDTH):
            acc = acc + shifted[(s - 1) * rows:s * rows] * w_ref[CONV_WIDTH - 1 - s:CONV_WIDTH - s, cs]
        prev_s[:, cs] = cur_b
        out_s[:, cs] = acc * _sigmoid(acc)


def _ssd_kernel(z_ref, xs_ref, bc_ref, dt_ref, cwx_ref, cbx_ref, cwbc_ref, cbbc_ref, dtb_ref,
                alog_ref, dskip_ref, nw_ref, r64_ref, shift_ref, y_ref,
                tailx_s, tailbc_s, state_s, xact_s, bcact_s, ybuf_s):
    L = SSM_CHUNK
    gw = D_INNER // N_SSM_GROUPS
    hpg = N_SSM_HEADS // N_SSM_GROUPS

    @pl.when(pl.program_id(1) == 0)
    def _():
        tailx_s[...] = jnp.zeros_like(tailx_s)
        tailbc_s[...] = jnp.zeros_like(tailbc_s)
        state_s[...] = jnp.zeros_like(state_s)

    _conv_silu(xs_ref, tailx_s, shift_ref, cwx_ref, cbx_ref, xact_s, 512)
    _conv_silu(bc_ref, tailbc_s, shift_ref, cwbc_ref, cbbc_ref, bcact_s, 512)

    dtv = jax.nn.softplus(dt_ref[...] + dtb_ref[...])
    a = -jnp.exp(alog_ref[...])
    dta = dtv * a
    ri = lax.broadcasted_iota(jnp.int32, (L, L), 0)
    ci = lax.broadcasted_iota(jnp.int32, (L, L), 1)
    causal = ri >= ci
    tri = jnp.where(causal, 1.0, 0.0).astype(BF16)
    p0, p1, p2 = _split3(dta)
    acs = _dot(tri, p0) + _dot(tri, p1) + _dot(tri, p2)
    acs_t = acs.T
    eacs = jnp.exp(acs)
    dte = jnp.exp(acs[L - 1:L, :] - acs)
    stacked = jnp.concatenate([dtv, eacs, dte], axis=0)
    s_hi, s_lo = _split2(stacked)
    expd = _dot(s_hi, r64_ref[...]) + _dot(s_lo, r64_ref[...])

    lo = lax.broadcasted_iota(jnp.int32, (L, LANES), 1) < (LANES // 2)
    for g in range(N_SSM_GROUPS):
        gs = slice(g * gw, (g + 1) * gw)
        b_g = bcact_s[:, g * D_STATE:(g + 1) * D_STATE]
        c_g = bcact_s[:, (N_SSM_GROUPS + g) * D_STATE:(N_SSM_GROUPS + g + 1) * D_STATE]
        b_t = b_g.T.astype(BF16)
        c_b = c_g.astype(BF16)
        cb = _dot(c_b, b_t)
        xs_g = xact_s[:, gs]
        xd = xs_g * expd[0:L, gs]
        xd_b = xd.astype(BF16)
        st_prev = state_s[g]
        y_off = _dot(c_b, st_prev.astype(BF16)) * expd[L:2 * L, gs]
        xdw = (xd * expd[2 * L:3 * L, gs]).astype(BF16)
        state_s[g] = expd[2 * L - 1:2 * L, gs] * st_prev + _dot(b_t, xdw)
        for jp in range(hpg // 2):
            ps = slice(jp * LANES, (jp + 1) * LANES)
            slab = xd_b[:, ps]
            halves = []
            for half in range(2):
                hh = g * hpg + 2 * jp + half
                diff = acs[:, hh:hh + 1] - acs_t[hh:hh + 1, :]
                dec = jnp.exp(jnp.where(causal, diff, NEG_INF))
                halves.append(_dot((cb * dec).astype(BF16), slab))
            y = jnp.where(lo, halves[0], halves[1]) + y_off[:, ps]
            cols = slice(g * gw + jp * LANES, g * gw + (jp + 1) * LANES)
            ybuf_s[:, cols] = y + dskip_ref[:, cols] * xs_g[:, ps]

    z = z_ref[...].astype(F32)
    yg = ybuf_s[...] * (z * _sigmoid(z))
    ms = jnp.mean(yg * yg, axis=-1, keepdims=True)
    y_ref[...] = (yg * lax.rsqrt(ms + NORM_EPS) * nw_ref[...]).astype(BF16)


def _ssd(zx, dt, conv_w, conv_b, dt_bias, a_log, d_skip, norm_w, batch, seq):
    t = zx.shape[0]
    L = SSM_CHUNK
    nc = seq // L
    di, dbc = D_INNER, D_BC
    pad = LANES - N_SSM_HEADS
    cwx, cwbc = conv_w[:, :di].astype(F32), conv_w[:, di:].astype(F32)
    cbx, cbbc = conv_b[None, :di].astype(F32), conv_b[None, di:].astype(F32)
    dtb = jnp.pad(dt_bias.astype(F32), (0, pad))[None]
    alog = jnp.pad(a_log.astype(F32), (0, pad))[None]
    dskip = jnp.repeat(d_skip.astype(F32), di // N_SSM_HEADS)[None]
    nw = norm_w.astype(F32)[None]
    r64 = jnp.asarray((np.arange(LANES)[:, None] == (np.arange(di)[None, :] // (di // N_SSM_HEADS)))
                      .astype(np.float32)).astype(BF16)
    shift = jnp.asarray(_conv_shift_matrix(L)).astype(BF16)
    row = lambda b, c: (b * nc + c, 0)
    const = lambda b, c: (0, 0)
    return pl.pallas_call(
        _ssd_kernel,
        out_shape=jax.ShapeDtypeStruct((t, di), BF16),
        grid=(batch, nc),
        in_specs=[
            pl.BlockSpec((L, di), lambda b, c: (b * nc + c, 0)),
            pl.BlockSpec((L, di), lambda b, c: (b * nc + c, 1)),
            pl.BlockSpec((L, dbc), lambda b, c: (b * nc + c, 2 * di // dbc)),
            pl.BlockSpec((L, LANES), row),
            pl.BlockSpec((CONV_WIDTH, di), const),
            pl.BlockSpec((1, di), const),
            pl.BlockSpec((CONV_WIDTH, dbc), const),
            pl.BlockSpec((1, dbc), const),
            pl.BlockSpec((1, LANES), const),
            pl.BlockSpec((1, LANES), const),
            pl.BlockSpec((1, di), const),
            pl.BlockSpec((1, di), const),
            pl.BlockSpec((LANES, di), const),
            pl.BlockSpec(((CONV_WIDTH - 1) * L, 2 * L), const),
        ],
        out_specs=pl.BlockSpec((L, di), row),
        scratch_shapes=[
            pltpu.VMEM((L, di), BF16), pltpu.VMEM((L, dbc), BF16),
            pltpu.VMEM((N_SSM_GROUPS, D_STATE, di // N_SSM_GROUPS), F32),
            pltpu.VMEM((L, di), F32), pltpu.VMEM((L, dbc), F32), pltpu.VMEM((L, di), F32),
        ],
        compiler_params=_cparams(("arbitrary", "arbitrary")),
        name="ssd_scan",
    )(zx, zx, zx, dt, cwx, cbx, cwbc, cbbc, dtb, alog, dskip, nw, r64, shift)


def _router_kernel(x_ref, y_ref, wout_ref, mv_ref, gain_ref, wrh_ref, wrl_ref,
                   x1_ref, hf_ref, rinfo_ref, cnt_ref, carry_s):
    tm = x_ref.shape[0]

    @pl.when(pl.program_id(0) == 0)
    def _():
        carry_s[...] = jnp.zeros_like(carry_s)

    mv = mv_ref[0]
    x1 = x_ref[...] + mv[0:1] * _dot(y_ref[...], wout_ref[...])
    x1_ref[...] = x1
    hf = _rms_mod(x1, gain_ref[...], mv[2:3], mv[1:2])
    hf_ref[...] = hf
    h_hi, h_lo = _split2(hf)
    logits = _dot(h_hi, wrh_ref[...]) + _dot(h_lo, wrh_ref[...]) + _dot(h_hi, wrl_ref[...])
    lane = lax.broadcasted_iota(jnp.int32, (tm, LANES), 1)
    lanef = lane.astype(F32)
    lg = jnp.where(lane < N_EXPERTS, logits, NEG_INF)
    v0 = jnp.max(lg, axis=-1, keepdims=True)
    i0 = jnp.min(jnp.where(lg == v0, lanef, float(LANES)), axis=-1, keepdims=True)
    lg1 = jnp.where(lanef == i0, NEG_INF, lg)
    v1 = jnp.max(lg1, axis=-1, keepdims=True)
    i1 = jnp.min(jnp.where(lg1 == v1, lanef, float(LANES)), axis=-1, keepdims=True)
    e1 = jnp.exp(v1 - v0)
    w0 = 1.0 / (1.0 + e1)
    w1 = e1 / (1.0 + e1)
    oh0 = jnp.where(lanef == i0, 1.0, 0.0)
    oh1 = jnp.where(lanef == i1, 1.0, 0.0)
    ri = lax.broadcasted_iota(jnp.int32, (tm, tm), 0)
    ci = lax.broadcasted_iota(jnp.int32, (tm, tm), 1)
    before = jnp.where(ri > ci, 1.0, 0.0).astype(BF16)
    pre0 = _dot(before, oh0.astype(BF16))
    pre1 = _dot(before, oh1.astype(BF16))
    cnt0 = jnp.sum(oh0, axis=0, keepdims=True)
    cnt1 = jnp.sum(oh1, axis=0, keepdims=True)
    carry = carry_s[...]
    rank0 = jnp.sum(oh0 * (pre0 + carry), axis=-1, keepdims=True)
    rank1 = jnp.sum(oh1 * (pre1 + carry + cnt0), axis=-1, keepdims=True)
    total = carry + cnt0 + cnt1
    carry_s[...] = total
    cnt_ref[...] = total
    rinfo = jnp.where(lane == 0, i0, jnp.where(lane == 1, i1, jnp.where(
        lane == 2, rank0, jnp.where(lane == 3, rank1, jnp.where(
            lane == 4, w0, jnp.where(lane == 5, w1, 0.0))))))
    rinfo_ref[...] = rinfo


def _router(x2, y, wout_b, mv, gain, wr_hi, wr_lo, seq, tm=512):
    t, d = x2.shape
    tpb = seq // tm
    return pl.pallas_call(
        _router_kernel,
        out_shape=(jax.ShapeDtypeStruct((t, d), F32), jax.ShapeDtypeStruct((t, d), F32),
                   jax.ShapeDtypeStruct((t, LANES), F32), jax.ShapeDtypeStruct((1, LANES), F32)),
        grid=(t // tm,),
        in_specs=[
            pl.BlockSpec((tm, d), lambda i: (i, 0)),
            pl.BlockSpec((tm, y.shape[1]), lambda i: (i, 0)),
            pl.BlockSpec(wout_b.shape, lambda i: (0, 0)),
            pl.BlockSpec((1, 3, d), lambda i: (i // tpb, 0, 0)),
            pl.BlockSpec((1, d), lambda i: (0, 0)),
            pl.BlockSpec((d, LANES), lambda i: (0, 0)),
            pl.BlockSpec((d, LANES), lambda i: (0, 0)),
        ],
        out_specs=(pl.BlockSpec((tm, d), lambda i: (i, 0)),
                   pl.BlockSpec((tm, d), lambda i: (i, 0)),
                   pl.BlockSpec((tm, LANES), lambda i: (i, 0)),
                   pl.BlockSpec((1, LANES), lambda i: (0, 0))),
        scratch_shapes=[pltpu.VMEM((1, LANES), F32)],
        compiler_params=_cparams(("arbitrary",)),
        name="ssm_out_router",
    )(x2, y, wout_b, mv, gain, wr_hi, wr_lo)


DMA_UNROLL = 8


def _row_copy(src, src_row, dst, dst_row, sem):
    return pltpu.make_async_copy(src.at[pl.ds(src_row, 1)], dst.at[pl.ds(dst_row, 1)], sem)


def _dispatch_kernel(pos_ref, hf_ref, sorted_in_ref, sorted_ref, sem):
    del sorted_in_ref
    tm = hf_ref.shape[0]
    base = pl.program_id(0) * (TOP_K * tm)

    def issue(blk, carry):
        for u in range(DMA_UNROLL):
            t = blk * DMA_UNROLL + u
            for k in range(TOP_K):
                _row_copy(hf_ref, t, sorted_ref, pos_ref[base + TOP_K * t + k], sem).start(priority=k)
        return carry

    lax.fori_loop(0, tm // DMA_UNROLL, issue, 0)
    for k in range(TOP_K):
        pltpu.make_async_copy(hf_ref, sorted_ref.at[pl.ds(0, tm)], sem).wait()


def _dispatch(pos_flat, hf, n_rows, tm=512):
    t, d = hf.shape
    zeros = jnp.zeros((n_rows, d), hf.dtype)
    return pl.pallas_call(
        _dispatch_kernel,
        out_shape=jax.ShapeDtypeStruct((n_rows, d), hf.dtype),
        grid_spec=pltpu.PrefetchScalarGridSpec(
            num_scalar_prefetch=1,
            grid=(t // tm,),
            in_specs=[pl.BlockSpec((tm, d), lambda i, pos: (i, 0)),
                      pl.BlockSpec(memory_space=pl.ANY)],
            out_specs=pl.BlockSpec(memory_space=pl.ANY),
            scratch_shapes=[pltpu.SemaphoreType.DMA],
        ),
        input_output_aliases={2: 0},
        compiler_params=_cparams(("arbitrary",)),
        name="moe_dispatch",
    )(pos_flat, hf, zeros)


def _combine_kernel(pos_ref, x1_ref, rinfo_ref, g2_ref, ys_ref, out_ref, buf, sem):
    tm = x1_ref.shape[0]
    base = pl.program_id(0) * (TOP_K * tm)

    def issue(blk, carry):
        for u in range(DMA_UNROLL):
            t = blk * DMA_UNROLL + u
            for k in range(TOP_K):
                _row_copy(ys_ref, pos_ref[base + TOP_K * t + k], buf.at[k], t, sem).start(priority=k)
        return carry

    lax.fori_loop(0, tm // DMA_UNROLL, issue, 0)
    for k in range(TOP_K):
        pltpu.make_async_copy(ys_ref.at[pl.ds(0, tm)], buf.at[k], sem).wait()
    rinfo = rinfo_ref[...]
    ffn = rinfo[:, 4:5] * buf[0] + rinfo[:, 5:6] * buf[1]
    out_ref[...] = x1_ref[...] + g2_ref[0] * ffn


def _combine(pos_flat, x1, rinfo, g2, ys, seq, tm=512):
    t, d = x1.shape
    tpb = seq // tm
    return pl.pallas_call(
        _combine_kernel,
        out_shape=jax.ShapeDtypeStruct((t, d), F32),
        grid_spec=pltpu.PrefetchScalarGridSpec(
            num_scalar_prefetch=1,
            grid=(t // tm,),
            in_specs=[pl.BlockSpec((tm, d), lambda i, pos: (i, 0)),
                      pl.BlockSpec((tm, LANES), lambda i, pos: (i, 0)),
                      pl.BlockSpec((1, 1, d), lambda i, pos: (i // tpb, 0, 0)),
                      pl.BlockSpec(memory_space=pl.ANY)],
            out_specs=pl.BlockSpec((tm, d), lambda i, pos: (i, 0)),
            scratch_shapes=[pltpu.VMEM((TOP_K, tm, d), F32), pltpu.SemaphoreType.DMA],
        ),
        compiler_params=_cparams(("arbitrary",)),
        name="moe_combine",
    )(pos_flat, x1, rinfo, g2, ys)


def _attn_layer(x2, mod_i, gain_i, bias, w_qkv, q_gain, k_gain, sinks, w_o, w_gu, w_down,
                batch, seq):
    d = D_MODEL
    sh1, sc1, g1, sh2, sc2, g2 = [mod_i[:, k * d:(k + 1) * d] for k in range(6)]
    perm = np.zeros((N_Q_HEADS,), np.int32)
    for gp in range(N_KV_HEADS // 2):
        for qi in range(Q_PER_KV):
            for a in range(2):
                perm[(gp * Q_PER_KV + qi) * 2 + a] = (2 * gp + a) * Q_PER_KV + qi
    col = (perm[:, None] * HEAD_DIM + np.arange(HEAD_DIM)[None, :]).reshape(-1)
    nq = N_Q_HEADS * HEAD_DIM
    w_qkv_b = jnp.concatenate([w_qkv[:, :nq][:, col], w_qkv[:, nq:]], axis=1).astype(BF16)
    w_o_b = w_o[col, :].astype(BF16)
    qg2 = jnp.tile(q_gain.astype(F32), 2)[None]
    kg2 = jnp.tile(k_gain.astype(F32), 2)[None]
    q, k, v = _qkv_proj(x2, jnp.stack([sh1, sc1], axis=1), gain_i[0:1], w_qkv_b, qg2, kg2, seq)
    o = _attention(q, k, v, bias, sinks, batch, seq)
    mv = jnp.stack([g1, sh2, sc2, g2], axis=1)
    return _ffn_dense(x2, o, w_o_b, mv, gain_i[1:2], w_gu.astype(BF16), w_down.astype(BF16), seq)


def _ssm_layer(x2, mod_i, gain_i, w_in, conv_w, conv_b, dt_bias, a_log, d_skip, norm_w, w_out,
               w_router, w_gu, w_down, batch, seq, moe_tm=1024):
    d = D_MODEL
    t = x2.shape[0]
    sh1, sc1, g1, sh2, sc2, g2 = [mod_i[:, k * d:(k + 1) * d] for k in range(6)]
    n_zx = 2 * D_INNER + D_BC
    w_zx_b = w_in[:, :n_zx].astype(BF16)
    w_dt_b = jnp.pad(w_in[:, n_zx:], ((0, 0), (0, LANES - N_SSM_HEADS))).astype(BF16)
    zx, dt = _ssm_in_proj(x2, jnp.stack([sh1, sc1], axis=1), gain_i[0:1], w_zx_b, w_dt_b, seq)
    y = _ssd(zx, dt, conv_w, conv_b, dt_bias, a_log, d_skip, norm_w, batch, seq)

    wr = jnp.pad(w_router.astype(F32), ((0, 0), (0, LANES - N_EXPERTS)))
    wr_hi = wr.astype(BF16)
    wr_lo = (wr - wr_hi.astype(F32)).astype(BF16)
    x1, hf, rinfo, cnt = _router(x2, y, w_out.astype(BF16), jnp.stack([g1, sh2, sc2], axis=1),
                                 gain_i[1:2], wr_hi, wr_lo, seq)

    counts = cnt[0, :N_EXPERTS].astype(jnp.int32)
    padded = ((counts + moe_tm - 1) // moe_tm) * moe_tm
    ends = jnp.cumsum(padded)
    base = ends - padded
    eidx = rinfo[:, 0:TOP_K].astype(jnp.int32)
    rank = rinfo[:, 2:2 + TOP_K].astype(jnp.int32)
    pos = jnp.sum(jnp.where(eidx[..., None] == jnp.arange(N_EXPERTS), base, 0), axis=-1) + rank
    pos_flat = pos.reshape(-1)
    n_tiles = (TOP_K * t) // moe_tm + N_EXPERTS
    tile_start = jnp.arange(n_tiles, dtype=jnp.int32) * moe_tm
    tile_expert = jnp.minimum(jnp.sum(tile_start[:, None] >= ends[None, :], axis=-1),
                              N_EXPERTS - 1).astype(jnp.int32)
    n_used = ends[-1] // moe_tm
    tile_expert = jnp.where(tile_start < ends[-1], tile_expert, tile_expert[n_used - 1])
    tile_expert = jnp.concatenate([tile_expert, n_used[None].astype(jnp.int32)])

    xs = _dispatch(pos_flat, hf, n_tiles * moe_tm)
    ys = _ffn_moe(xs, tile_expert, w_gu.astype(BF16), w_down.astype(BF16), moe_tm)
    return _combine(pos_flat, x1, rinfo, g2[:, None, :], ys, seq)


def kernel(x, c, ada_w, ada_b, norm_gain, rel_bias, attn_w_qkv, attn_q_gain, attn_k_gain, attn_sinks, attn_w_o, ssm_w_in, ssm_conv_w, ssm_conv_b, ssm_dt_bias, ssm_a_log, ssm_d, ssm_norm_w, ssm_w_out, ffn_w_gu, ffn_w_down, moe_w_router, moe_w_gu, moe_w_down):
    batch, seq, d = x.shape
    mod = _adaln_mod(c.astype(F32), ada_w.astype(F32), ada_b.astype(F32))
    bias = _bias_table(rel_bias)
    x2 = x.reshape(batch * seq, d).astype(F32)
    for i in range(DEPTH):
        j = i // 2
        gain_i = norm_gain[i].astype(F32)
        if i % 2 == 0:
            x2 = _attn_layer(x2, mod[i], gain_i, bias, attn_w_qkv[j], attn_q_gain[j],
                             attn_k_gain[j], attn_sinks[j], attn_w_o[j], ffn_w_gu[j],
                             ffn_w_down[j], batch, seq)
        else:
            x2 = _ssm_layer(x2, mod[i], gain_i, ssm_w_in[j], ssm_conv_w[j], ssm_conv_b[j],
                            ssm_dt_bias[j], ssm_a_log[j], ssm_d[j], ssm_norm_w[j], ssm_w_out[j],
                            moe_w_router[j], moe_w_gu[j], moe_w_down[j], batch, seq)
    return x2.reshape(batch, seq, d).astype(x.dtype)
```

```python
import functools
import math

import numpy as np
import jax
import jax.numpy as jnp
from jax import lax
from jax.experimental import pallas as pl
from jax.experimental.pallas import tpu as pltpu

F32 = jnp.float32
BF16 = jnp.bfloat16

D_MODEL = 1024
DEPTH = 4
N_Q_HEADS = 16
N_KV_HEADS = 4
HEAD_DIM = 64
Q_PER_KV = 4
ATTN_BLOCK = 128
N_BUCKETS = 32
MAX_DISTANCE = 128
D_INNER = 2048
N_SSM_HEADS = 32
N_SSM_GROUPS = 4
D_STATE = 128
CONV_WIDTH = 4
SSM_CHUNK = 128
D_BC = 2 * N_SSM_GROUPS * D_STATE
D_FF = 3584
N_EXPERTS = 8
TOP_K = 2
NORM_EPS = 1e-6

LANES = 128
SUBLANES = 8
VMEM_LIMIT = 56 * 1024 * 1024
NEG_INF = float("-inf")


def _cparams(sem):
    return pltpu.CompilerParams(dimension_semantics=sem, vmem_limit_bytes=VMEM_LIMIT)


def _sigmoid(v):
    return 1.0 / (1.0 + jnp.exp(-v))


def _rms_mod(x, gain, scale, shift):
    ms = jnp.mean(x * x, axis=-1, keepdims=True)
    y = x * lax.rsqrt(ms + NORM_EPS)
    return (y * gain) * (1.0 + scale) + shift


def _split2(v):
    hi = v.astype(BF16)
    lo = (v - hi.astype(F32)).astype(BF16)
    return hi, lo


def _split3(v):
    hi = v.astype(BF16)
    r = v - hi.astype(F32)
    mid = r.astype(BF16)
    lo = (r - mid.astype(F32)).astype(BF16)
    return hi, mid, lo


def _dot(a, b):
    return jnp.dot(a, b, preferred_element_type=F32)


def _store_tile_rows(ref, row0, val):
    n = val.shape[0]
    for c in range(val.shape[1] // LANES):
        ref[pl.ds(row0 * SUBLANES + c, n, stride=SUBLANES), :] = val[:, c * LANES:(c + 1) * LANES]


def _load_tile_rows(ref, n):
    return [ref[pl.ds(c, n, stride=SUBLANES), :] for c in range(SUBLANES)]


def _mod_kernel(c_ref, w_ref, b_ref, o_ref):
    c = c_ref[...]
    cond = c * _sigmoid(c)
    o_ref[0] = _dot(cond.astype(BF16), w_ref[0].astype(BF16)) + b_ref[0]


def _adaln_mod(c, ada_w, ada_b):
    depth, d, n = ada_w.shape
    b = c.shape[0]
    tn = 1024
    c_pad = jnp.zeros((SUBLANES, d), F32).at[:b].set(c)
    out = pl.pallas_call(
        _mod_kernel,
        out_shape=jax.ShapeDtypeStruct((depth, SUBLANES, n), F32),
        grid=(depth, n // tn),
        in_specs=[
            pl.BlockSpec((SUBLANES, d), lambda i, j: (0, 0)),
            pl.BlockSpec((1, d, tn), lambda i, j: (i, 0, j)),
            pl.BlockSpec((1, 1, tn), lambda i, j: (i, 0, j)),
        ],
        out_specs=pl.BlockSpec((1, SUBLANES, tn), lambda i, j: (i, 0, j)),
        compiler_params=_cparams(("arbitrary", "arbitrary")),
        name="adaln_mod",
    )(c_pad, ada_w, ada_b.reshape(depth, 1, n))
    return out[:, :b]


def _t5_bucket_table():
    qi = np.arange(ATTN_BLOCK)[:, None]
    kj = np.arange(2 * ATTN_BLOCK)[None, :]
    dist = qi + ATTN_BLOCK - kj
    max_exact = N_BUCKETS // 2
    d = np.maximum(dist, 0)
    df = np.maximum(d, 1).astype(np.float64)
    large = max_exact + (np.log(df / max_exact) / math.log(MAX_DISTANCE / max_exact)
                         * (N_BUCKETS - max_exact)).astype(np.int32)
    large = np.minimum(large, N_BUCKETS - 1)
    bucket = np.where(d < max_exact, d, large)
    valid = (dist >= 0) & (dist < ATTN_BLOCK)
    return np.where(valid, bucket, -1).astype(np.int32)


def _bias_kernel(rb_ref, bkt_ref, o_ref):
    j = pl.program_id(0)
    gp = j // Q_PER_KV
    qi = j % Q_PER_KV
    bkt = bkt_ref[...]
    for a in range(2):
        h = (2 * gp + a) * Q_PER_KV + qi
        acc = jnp.full(bkt.shape, NEG_INF, F32)
        for b in range(N_BUCKETS):
            acc = jnp.where(bkt == b, rb_ref[b, h], acc)
        o_ref[0, a * ATTN_BLOCK:(a + 1) * ATTN_BLOCK, :] = acc


def _bias_table(rel_bias):
    bkt = jnp.asarray(_t5_bucket_table())
    n_slots = N_Q_HEADS // 2
    return pl.pallas_call(
        _bias_kernel,
        out_shape=jax.ShapeDtypeStruct((n_slots, 2 * ATTN_BLOCK, 2 * ATTN_BLOCK), F32),
        grid=(n_slots,),
        in_specs=[
            pl.BlockSpec(memory_space=pltpu.SMEM),
            pl.BlockSpec((ATTN_BLOCK, 2 * ATTN_BLOCK), lambda j: (0, 0)),
        ],
        out_specs=pl.BlockSpec((1, 2 * ATTN_BLOCK, 2 * ATTN_BLOCK), lambda j: (j, 0, 0)),
        compiler_params=_cparams(("arbitrary",)),
        name="t5_bias_table",
    )(rel_bias.astype(F32), bkt)


def _qkv_kernel(x_ref, ms_ref, gain_ref, w_ref, qg_ref, kg_ref, q_ref, k_ref, v_ref):
    tm = x_ref.shape[0]
    ms = ms_ref[0]
    hm = _rms_mod(x_ref[...], gain_ref[...], ms[1:2], ms[0:1]).astype(BF16)
    lo = lax.broadcasted_iota(jnp.int32, (tm, LANES), 1) < HEAD_DIM

    def head_norm(slab, g):
        sq = slab * slab
        s_lo = jnp.sum(jnp.where(lo, sq, 0.0), axis=-1, keepdims=True)
        s_hi = jnp.sum(jnp.where(lo, 0.0, sq), axis=-1, keepdims=True)
        msq = jnp.where(lo, s_lo, s_hi) * (1.0 / HEAD_DIM)
        return slab * lax.rsqrt(msq + NORM_EPS) * g

    nq = N_Q_HEADS * HEAD_DIM
    nk = N_KV_HEADS * HEAD_DIM
    q_all = _dot(hm, w_ref[:, :nq])
    for j in range(nq // LANES):
        sl = slice(j * LANES, (j + 1) * LANES)
        q_ref[:, sl] = (head_norm(q_all[:, sl], qg_ref[...]) * (HEAD_DIM ** -0.5)).astype(BF16)
    k_all = _dot(hm, w_ref[:, nq:nq + nk])
    for j in range(nk // LANES):
        sl = slice(j * LANES, (j + 1) * LANES)
        k_ref[:, sl] = head_norm(k_all[:, sl], kg_ref[...]).astype(BF16)
    v_ref[...] = _dot(hm, w_ref[:, nq + nk:]).astype(BF16)


def _qkv_proj(x2, ms, gain, w_qkv_all, li, qg2, kg2, seq, tm=512):
    t, d = x2.shape
    nq = N_Q_HEADS * HEAD_DIM
    nk = N_KV_HEADS * HEAD_DIM
    tpb = seq // tm
    return pl.pallas_call(
        _qkv_kernel,
        out_shape=(jax.ShapeDtypeStruct((t, nq), BF16),
                   jax.ShapeDtypeStruct((t, nk), BF16),
                   jax.ShapeDtypeStruct((t, nk), BF16)),
        grid=(t // tm,),
        in_specs=[
            pl.BlockSpec((tm, d), lambda i: (i, 0)),
            pl.BlockSpec((1, 2, d), lambda i: (i // tpb, 0, 0)),
            pl.BlockSpec((1, d), lambda i: (0, 0)),
            pl.BlockSpec((None, d, nq + 2 * nk), lambda i: (li, 0, 0)),
            pl.BlockSpec((1, LANES), lambda i: (0, 0)),
            pl.BlockSpec((1, LANES), lambda i: (0, 0)),
        ],
        out_specs=(pl.BlockSpec((tm, nq), lambda i: (i, 0)),
                   pl.BlockSpec((tm, nk), lambda i: (i, 0)),
                   pl.BlockSpec((tm, nk), lambda i: (i, 0))),
        compiler_params=_cparams(("arbitrary",)),
        name="attn_qkv",
    )(x2, ms, gain, w_qkv_all, qg2, kg2)


def _attn_kernel(sink_ref, q_ref, kc_ref, kp_ref, vc_ref, vp_ref, bias_ref, o_ref):
    blk = ATTN_BLOCK
    first = pl.program_id(1) == 0
    lane = lax.broadcasted_iota(jnp.int32, (blk, LANES), 1)
    lo = lane < HEAD_DIM
    mask_lo = jnp.where(lo, 1.0, 0.0).astype(BF16)
    mask_hi = jnp.where(lo, 0.0, 1.0).astype(BF16)
    row = lax.broadcasted_iota(jnp.int32, (2 * blk, 1), 0)
    nt = (((1,), (1,)), ((), ()))
    for gp in range(N_KV_HEADS // 2):
        ksl = slice(gp * LANES, (gp + 1) * LANES)
        kc = kc_ref[:, ksl]
        kp = kp_ref[:, ksl]
        vc = vc_ref[:, ksl]
        vp = vp_ref[:, ksl]
        for qi in range(Q_PER_KV):
            j = gp * Q_PER_KV + qi
            sl = slice(j * LANES, (j + 1) * LANES)
            qs = q_ref[:, sl]
            q2 = jnp.concatenate([qs * mask_lo, qs * mask_hi], axis=0)
            s_p = lax.dot_general(q2, kp, nt, preferred_element_type=F32)
            s_c = lax.dot_general(q2, kc, nt, preferred_element_type=F32)
            bias = bias_ref[j]
            s_p = jnp.where(first, NEG_INF, s_p + bias[:, :blk])
            s_c = s_c + bias[:, blk:]
            h_a = (2 * gp) * Q_PER_KV + qi
            h_b = (2 * gp + 1) * Q_PER_KV + qi
            sink = jnp.where(row < blk, sink_ref[h_a], sink_ref[h_b])
            m = jnp.maximum(jnp.max(jnp.maximum(s_p, s_c), axis=-1, keepdims=True), sink)
            p_p = jnp.exp(s_p - m)
            p_c = jnp.exp(s_c - m)
            den = jnp.sum(p_p + p_c, axis=-1, keepdims=True) + jnp.exp(sink - m)
            pv = _dot(p_p.astype(BF16), vp) + _dot(p_c.astype(BF16), vc)
            pv = pv / den
            o_ref[:, sl] = jnp.where(lo, pv[:blk], pv[blk:]).astype(BF16)


def _attention(q, k, v, bias, sinks, batch, seq):
    t, nq = q.shape
    nk = k.shape[1]
    blk = ATTN_BLOCK
    nb = seq // blk
    cur = lambda b, n, *_: (b * nb + n, 0)
    prev = lambda b, n, *_: (b * nb + jnp.maximum(n - 1, 0), 0)
    return pl.pallas_call(
        _attn_kernel,
        out_shape=jax.ShapeDtypeStruct((t, nq), BF16),
        grid=(batch, nb),
        in_specs=[
            pl.BlockSpec(memory_space=pltpu.SMEM),
            pl.BlockSpec((blk, nq), cur),
            pl.BlockSpec((blk, nk), cur),
            pl.BlockSpec((blk, nk), prev),
            pl.BlockSpec((blk, nk), cur),
            pl.BlockSpec((blk, nk), prev),
            pl.BlockSpec(bias.shape, lambda b, n: (0, 0, 0)),
        ],
        out_specs=pl.BlockSpec((blk, nq), cur),
        compiler_params=_cparams(("arbitrary", "arbitrary")),
        name="swa_attention",
    )(sinks.astype(F32), q, k, k, v, v, bias)


def _swiglu_step(hf, wg_ref, wu_ref, wd_ref, acc_s):
    g = _dot(hf, wg_ref[...])
    u = _dot(hf, wu_ref[...])
    a = (g * _sigmoid(g) * u).astype(BF16)
    acc_s[...] += _dot(a, wd_ref[...])


def _ffn_dense_kernel(x_ref, o_ref, wo_ref, mv_ref, gain_ref, wg_ref, wu_ref, wd_ref,
                      out_ref, hf_s, acc_s):
    j = pl.program_id(1)

    @pl.when(j == 0)
    def _():
        mv = mv_ref[0]
        x1 = x_ref[...] + mv[0:1] * _dot(o_ref[...], wo_ref[...])
        out_ref[...] = x1
        hf_s[...] = _rms_mod(x1, gain_ref[...], mv[2:3], mv[1:2]).astype(BF16)
        acc_s[...] = jnp.zeros_like(acc_s)

    _swiglu_step(hf_s[...], wg_ref, wu_ref, wd_ref, acc_s)

    @pl.when(j == pl.num_programs(1) - 1)
    def _():
        out_ref[...] = out_ref[...] + mv_ref[0][3:4] * acc_s[...]


def _ffn_dense(x2, o, wo_all, mv, gain, w_gu_all, w_down_all, li, seq, tm=1024, tf=512):
    t, d = x2.shape
    f = w_down_all.shape[1]
    nj = f // tf
    tpb = seq // tm
    return pl.pallas_call(
        _ffn_dense_kernel,
        out_shape=jax.ShapeDtypeStruct((t, d), F32),
        grid=(t // tm, nj),
        in_specs=[
            pl.BlockSpec((tm, d), lambda i, j: (i, 0)),
            pl.BlockSpec((tm, o.shape[1]), lambda i, j: (i, 0)),
            pl.BlockSpec((None,) + wo_all.shape[1:], lambda i, j: (li, 0, 0)),
            pl.BlockSpec((1, 4, d), lambda i, j: (i // tpb, 0, 0)),
            pl.BlockSpec((1, d), lambda i, j: (0, 0)),
            pl.BlockSpec((None, d, tf), lambda i, j: (li, 0, j)),
            pl.BlockSpec((None, d, tf), lambda i, j: (li, 0, nj + j)),
            pl.BlockSpec((None, tf, d), lambda i, j: (li, j, 0)),
        ],
        out_specs=pl.BlockSpec((tm, d), lambda i, j: (i, 0)),
        scratch_shapes=[pltpu.VMEM((tm, d), BF16), pltpu.VMEM((tm, d), F32)],
        compiler_params=_cparams(("arbitrary", "arbitrary")),
        name="wo_ffn_dense",
    )(x2, o, wo_all, mv, gain, w_gu_all, w_gu_all, w_down_all)


def _ffn_moe_kernel(te_ref, xs_ref, wg_ref, wu_ref, wd_ref, out_ref, hf_s, acc_s):
    i = pl.program_id(0)
    j = pl.program_id(1)
    used = i < te_ref[pl.num_programs(0)]

    tm = hf_s.shape[0]

    @pl.when(used & (j == 0))
    def _():
        for c, part in enumerate(_load_tile_rows(xs_ref, tm)):
            hf_s[:, c * LANES:(c + 1) * LANES] = part.astype(BF16)
        acc_s[...] = jnp.zeros_like(acc_s)

    @pl.when(used)
    def _():
        _swiglu_step(hf_s[...], wg_ref, wu_ref, wd_ref, acc_s)

    @pl.when(j == pl.num_programs(1) - 1)
    def _():
        @pl.when(used)
        def _():
            _store_tile_rows(out_ref, 0, acc_s[...])

        @pl.when(jnp.logical_not(used))
        def _():
            out_ref[...] = jnp.zeros_like(out_ref)


def _ffn_moe(xs, tile_expert, w_gu_all, w_down_all, li, tm, tf=896):
    r = xs.shape[0] // SUBLANES
    d = w_gu_all.shape[2]
    f = w_down_all.shape[2]
    nj = f // tf
    nt = r // tm

    def jj(i, j, te):
        return jnp.where(i < te[nt], j, 0)

    return pl.pallas_call(
        _ffn_moe_kernel,
        out_shape=jax.ShapeDtypeStruct((r * SUBLANES, LANES), F32),
        grid_spec=pltpu.PrefetchScalarGridSpec(
            num_scalar_prefetch=1,
            grid=(nt, nj),
            in_specs=[
                pl.BlockSpec((tm * SUBLANES, LANES), lambda i, j, te: (i, 0)),
                pl.BlockSpec((None, None, d, tf), lambda i, j, te: (li, te[i], 0, jj(i, j, te))),
                pl.BlockSpec((None, None, d, tf), lambda i, j, te: (li, te[i], 0, nj + jj(i, j, te))),
                pl.BlockSpec((None, None, tf, d), lambda i, j, te: (li, te[i], jj(i, j, te), 0)),
            ],
            out_specs=pl.BlockSpec((tm * SUBLANES, LANES), lambda i, j, te: (i, 0)),
            scratch_shapes=[pltpu.VMEM((tm, d), BF16), pltpu.VMEM((tm, d), F32)],
        ),
        compiler_params=_cparams(("arbitrary", "arbitrary")),
        name="moe_expert_ffn",
    )(tile_expert, xs, w_gu_all, w_gu_all, w_down_all)


def _inproj_kernel(x_ref, ms_ref, gain_ref, w_ref, wdt_ref, zx_ref, dt_ref, hm_s):
    j = pl.program_id(1)

    @pl.when(j == 0)
    def _():
        ms = ms_ref[0]
        hm = _rms_mod(x_ref[...], gain_ref[...], ms[1:2], ms[0:1]).astype(BF16)
        hm_s[...] = hm
        dt_ref[...] = _dot(hm, wdt_ref[...])

    zx_ref[...] = _dot(hm_s[...], w_ref[...]).astype(BF16)


def _ssm_in_proj(x2, ms, gain, w_zx_all, w_dt_all, li, seq, tm=1024, tn=1024):
    t, d = x2.shape
    n = w_zx_all.shape[2]
    tpb = seq // tm
    return pl.pallas_call(
        _inproj_kernel,
        out_shape=(jax.ShapeDtypeStruct((t, n), BF16), jax.ShapeDtypeStruct((t, LANES), F32)),
        grid=(t // tm, n // tn),
        in_specs=[
            pl.BlockSpec((tm, d), lambda i, j: (i, 0)),
            pl.BlockSpec((1, 2, d), lambda i, j: (i // tpb, 0, 0)),
            pl.BlockSpec((1, d), lambda i, j: (0, 0)),
            pl.BlockSpec((None, d, tn), lambda i, j: (li, 0, j)),
            pl.BlockSpec((None, d, LANES), lambda i, j: (li, 0, 0)),
        ],
        out_specs=(pl.BlockSpec((tm, tn), lambda i, j: (i, j)),
                   pl.BlockSpec((tm, LANES), lambda i, j: (i, 0))),
        scratch_shapes=[pltpu.VMEM((tm, d), BF16)],
        compiler_params=_cparams(("arbitrary", "arbitrary")),
        name="ssm_in_proj",
    )(x2, ms, gain, w_zx_all, w_dt_all)


def _conv_silu(cur_ref, ext_s, w_ref, b_ref, out_s, width):
    rows, cols = cur_ref.shape
    for c0 in range(0, cols, width):
        cs = slice(c0, c0 + width)
        cur = cur_ref[:, cs].astype(F32)
        ext_s[SUBLANES:, cs] = cur
        acc = cur * w_ref[CONV_WIDTH - 1:CONV_WIDTH, cs] + b_ref[:, cs]
        for s in range(1, CONV_WIDTH):
            acc = acc + ext_s[pl.ds(SUBLANES - s, rows), cs] * w_ref[CONV_WIDTH - 1 - s:CONV_WIDTH - s, cs]
        ext_s[:SUBLANES, cs] = cur[rows - SUBLANES:]
        out_s[:, cs] = acc * _sigmoid(acc)


def _ssd_kernel(z_ref, xs_ref, bc_ref, dt_ref, cwx_ref, cbx_ref, cwbc_ref, cbbc_ref, dtb_ref,
                alog_ref, dskip_ref, nw_ref, r64_ref, y_ref,
                extx_s, extbc_s, state_s, xact_s, bcact_s, ybuf_s):
    L = SSM_CHUNK
    gw = D_INNER // N_SSM_GROUPS
    hpg = N_SSM_HEADS // N_SSM_GROUPS

    @pl.when(pl.program_id(1) == 0)
    def _():
        extx_s[:SUBLANES] = jnp.zeros((SUBLANES, extx_s.shape[1]), F32)
        extbc_s[:SUBLANES] = jnp.zeros((SUBLANES, extbc_s.shape[1]), F32)
        state_s[...] = jnp.zeros_like(state_s)

    _conv_silu(xs_ref, extx_s, cwx_ref, cbx_ref, xact_s, 512)
    _conv_silu(bc_ref, extbc_s, cwbc_ref, cbbc_ref, bcact_s, 512)

    dtv = jax.nn.softplus(dt_ref[...] + dtb_ref[...])
    a = -jnp.exp(alog_ref[...])
    dta = dtv * a
    ri = lax.broadcasted_iota(jnp.int32, (L, L), 0)
    ci = lax.broadcasted_iota(jnp.int32, (L, L), 1)
    causal = ri >= ci
    tri = jnp.where(causal, 1.0, 0.0).astype(BF16)
    p0, p1, p2 = _split3(dta)
    acs = _dot(tri, p0) + _dot(tri, p1) + _dot(tri, p2)
    acs_t = acs.T
    eacs = jnp.exp(acs)
    dte = jnp.exp(acs[L - 1:L, :] - acs)
    stacked = jnp.concatenate([dtv, eacs, dte], axis=0)
    s_hi, s_lo = _split2(stacked)
    expd = _dot(s_hi, r64_ref[...]) + _dot(s_lo, r64_ref[...])

    lo = lax.broadcasted_iota(jnp.int32, (L, LANES), 1) < (LANES // 2)
    for g in range(N_SSM_GROUPS):
        gs = slice(g * gw, (g + 1) * gw)
        b_g = bcact_s[:, g * D_STATE:(g + 1) * D_STATE]
        c_g = bcact_s[:, (N_SSM_GROUPS + g) * D_STATE:(N_SSM_GROUPS + g + 1) * D_STATE]
        b_t = b_g.T.astype(BF16)
        c_b = c_g.astype(BF16)
        cb = _dot(c_b, b_t)
        xs_g = xact_s[:, gs]
        xd = xs_g * expd[0:L, gs]
        xd_b = xd.astype(BF16)
        st_prev = state_s[g]
        y_off = _dot(c_b, st_prev.astype(BF16)) * expd[L:2 * L, gs]
        xdw = (xd * expd[2 * L:3 * L, gs]).astype(BF16)
        state_s[g] = expd[2 * L - 1:2 * L, gs] * st_prev + _dot(b_t, xdw)
        for jp in range(hpg // 2):
            ps = slice(jp * LANES, (jp + 1) * LANES)
            slab = xd_b[:, ps]
            halves = []
            for half in range(2):
                hh = g * hpg + 2 * jp + half
                diff = acs[:, hh:hh + 1] - acs_t[hh:hh + 1, :]
                dec = jnp.exp(jnp.where(causal, diff, NEG_INF))
                halves.append(_dot((cb * dec).astype(BF16), slab))
            y = jnp.where(lo, halves[0], halves[1]) + y_off[:, ps]
            cols = slice(g * gw + jp * LANES, g * gw + (jp + 1) * LANES)
            ybuf_s[:, cols] = y + dskip_ref[:, cols] * xs_g[:, ps]

    z = z_ref[...].astype(F32)
    yg = ybuf_s[...] * (z * _sigmoid(z))
    ms = jnp.mean(yg * yg, axis=-1, keepdims=True)
    y_ref[...] = (yg * lax.rsqrt(ms + NORM_EPS) * nw_ref[...]).astype(BF16)


def _ssd(zx, dt, conv_w, conv_b, dt_bias, a_log, d_skip, norm_w, batch, seq):
    t = zx.shape[0]
    L = SSM_CHUNK
    nc = seq // L
    di, dbc = D_INNER, D_BC
    pad = LANES - N_SSM_HEADS
    cwx, cwbc = conv_w[:, :di].astype(F32), conv_w[:, di:].astype(F32)
    cbx, cbbc = conv_b[None, :di].astype(F32), conv_b[None, di:].astype(F32)
    dtb = jnp.pad(dt_bias.astype(F32), (0, pad))[None]
    alog = jnp.pad(a_log.astype(F32), (0, pad))[None]
    dskip = jnp.repeat(d_skip.astype(F32), di // N_SSM_HEADS)[None]
    nw = norm_w.astype(F32)[None]
    r64 = jnp.asarray((np.arange(LANES)[:, None] == (np.arange(di)[None, :] // (di // N_SSM_HEADS)))
                      .astype(np.float32)).astype(BF16)
    row = lambda b, c: (b * nc + c, 0)
    const = lambda b, c: (0, 0)
    return pl.pallas_call(
        _ssd_kernel,
        out_shape=jax.ShapeDtypeStruct((t, di), BF16),
        grid=(batch, nc),
        in_specs=[
            pl.BlockSpec((L, di), lambda b, c: (b * nc + c, 0)),
            pl.BlockSpec((L, di), lambda b, c: (b * nc + c, 1)),
            pl.BlockSpec((L, dbc), lambda b, c: (b * nc + c, 2 * di // dbc)),
            pl.BlockSpec((L, LANES), row),
            pl.BlockSpec((CONV_WIDTH, di), const),
            pl.BlockSpec((1, di), const),
            pl.BlockSpec((CONV_WIDTH, dbc), const),
            pl.BlockSpec((1, dbc), const),
            pl.BlockSpec((1, LANES), const),
            pl.BlockSpec((1, LANES), const),
            pl.BlockSpec((1, di), const),
            pl.BlockSpec((1, di), const),
            pl.BlockSpec((LANES, di), const),
        ],
        out_specs=pl.BlockSpec((L, di), row),
        scratch_shapes=[
            pltpu.VMEM((SUBLANES + L, di), F32), pltpu.VMEM((SUBLANES + L, dbc), F32),
            pltpu.VMEM((N_SSM_GROUPS, D_STATE, di // N_SSM_GROUPS), F32),
            pltpu.VMEM((L, di), F32), pltpu.VMEM((L, dbc), F32), pltpu.VMEM((L, di), F32),
        ],
        compiler_params=_cparams(("arbitrary", "arbitrary")),
        name="ssd_scan",
    )(zx, zx, zx, dt, cwx, cbx, cwbc, cbbc, dtb, alog, dskip, nw, r64)


ROUTER_SUB = 512


def _router_kernel(x_ref, y_ref, wout_ref, mv_ref, gain_ref, wr_ref,
                   x1_ref, hf_ref, rinfo_ref, cnt_ref, carry_s):
    tm = x_ref.shape[0]
    sub = ROUTER_SUB

    @pl.when(pl.program_id(0) == 0)
    def _():
        carry_s[...] = jnp.zeros_like(carry_s)

    mv = mv_ref[0]
    lane = lax.broadcasted_iota(jnp.int32, (sub, LANES), 1)
    lanef = lane.astype(F32)
    ri = lax.broadcasted_iota(jnp.int32, (sub, sub), 0)
    ci = lax.broadcasted_iota(jnp.int32, (sub, sub), 1)
    before = jnp.where(ri > ci, 1.0, 0.0).astype(BF16)
    carry = carry_s[...]
    for h in range(tm // sub):
        rs = slice(h * sub, (h + 1) * sub)
        x1 = x_ref[rs, :] + mv[0:1] * _dot(y_ref[rs, :], wout_ref[...])
        x1_ref[rs, :] = x1
        hf = _rms_mod(x1, gain_ref[...], mv[2:3], mv[1:2])
        _store_tile_rows(hf_ref, h * sub, hf)
        h_hi, h_lo = _split2(hf)
        hw = _dot(h_hi, wr_ref[...])
        logits = hw[:, :LANES] + hw[:, LANES:] + _dot(h_lo, wr_ref[:, :LANES])
        lg = jnp.where(lane < N_EXPERTS, logits, NEG_INF)
        v0 = jnp.max(lg, axis=-1, keepdims=True)
        i0 = jnp.min(jnp.where(lg == v0, lanef, float(LANES)), axis=-1, keepdims=True)
        lg1 = jnp.where(lanef == i0, NEG_INF, lg)
        v1 = jnp.max(lg1, axis=-1, keepdims=True)
        i1 = jnp.min(jnp.where(lg1 == v1, lanef, float(LANES)), axis=-1, keepdims=True)
        e1 = jnp.exp(v1 - v0)
        w0 = 1.0 / (1.0 + e1)
        w1 = e1 / (1.0 + e1)
        oh0 = jnp.where(lanef == i0, 1.0, 0.0)
        oh1 = jnp.where(lanef == i1, 1.0, 0.0)
        pre = _dot(before, jnp.concatenate([oh0, oh1], axis=1).astype(BF16))
        cnt0 = jnp.sum(oh0, axis=0, keepdims=True)
        cnt1 = jnp.sum(oh1, axis=0, keepdims=True)
        rank0 = jnp.sum(oh0 * (pre[:, :LANES] + carry), axis=-1, keepdims=True)
        rank1 = jnp.sum(oh1 * (pre[:, LANES:] + carry + cnt0), axis=-1, keepdims=True)
        carry = carry + cnt0 + cnt1
        rinfo_ref[rs, :] = jnp.where(lane == 0, i0, jnp.where(lane == 1, i1, jnp.where(
            lane == 2, rank0, jnp.where(lane == 3, rank1, jnp.where(
                lane == 4, w0, jnp.where(lane == 5, w1, 0.0))))))
    carry_s[...] = carry
    cnt_ref[...] = carry


def _router(x2, y, wout_all, li, mv, gain, wr, seq, tm=1024):
    t, d = x2.shape
    tpb = seq // tm
    return pl.pallas_call(
        _router_kernel,
        out_shape=(jax.ShapeDtypeStruct((t, d), F32),
                   jax.ShapeDtypeStruct((t * SUBLANES, LANES), F32),
                   jax.ShapeDtypeStruct((t, LANES), F32), jax.ShapeDtypeStruct((1, LANES), F32)),
        grid=(t // tm,),
        in_specs=[
            pl.BlockSpec((tm, d), lambda i: (i, 0)),
            pl.BlockSpec((tm, y.shape[1]), lambda i: (i, 0)),
            pl.BlockSpec((None,) + wout_all.shape[1:], lambda i: (li, 0, 0)),
            pl.BlockSpec((1, 3, d), lambda i: (i // tpb, 0, 0)),
            pl.BlockSpec((1, d), lambda i: (0, 0)),
            pl.BlockSpec((d, 2 * LANES), lambda i: (0, 0)),
        ],
        out_specs=(pl.BlockSpec((tm, d), lambda i: (i, 0)),
                   pl.BlockSpec((tm * SUBLANES, LANES), lambda i: (i, 0)),
                   pl.BlockSpec((tm, LANES), lambda i: (i, 0)),
                   pl.BlockSpec((1, LANES), lambda i: (0, 0))),
        scratch_shapes=[pltpu.VMEM((1, LANES), F32)],
        compiler_params=_cparams(("arbitrary",)),
        name="ssm_out_router",
    )(x2, y, wout_all, mv, gain, wr)


DMA_UNROLL = 8


def _row_copy(src, src_row, dst, dst_row, sem):
    return pltpu.make_async_copy(
        src.at[pl.ds(pl.multiple_of(src_row * SUBLANES, SUBLANES), SUBLANES)],
        dst.at[pl.ds(pl.multiple_of(dst_row * SUBLANES, SUBLANES), SUBLANES)], sem)


def _dispatch_kernel(pos_ref, hf_ref, sorted_in_ref, sorted_ref, sem):
    del sorted_in_ref
    tm = hf_ref.shape[0] // SUBLANES
    base = pl.program_id(0) * (TOP_K * tm)

    def issue(blk, carry):
        for u in range(DMA_UNROLL):
            t = blk * DMA_UNROLL + u
            for k in range(TOP_K):
                _row_copy(hf_ref, t, sorted_ref, pos_ref[base + TOP_K * t + k], sem).start(priority=k)
        return carry

    lax.fori_loop(0, tm // DMA_UNROLL, issue, 0)
    for k in range(TOP_K):
        pltpu.make_async_copy(hf_ref, sorted_ref.at[pl.ds(0, tm * SUBLANES)], sem).wait()


def _dispatch(pos_flat, hf, sorted_init, tm=512):
    t = hf.shape[0] // SUBLANES
    return pl.pallas_call(
        _dispatch_kernel,
        out_shape=jax.ShapeDtypeStruct(sorted_init.shape, hf.dtype),
        grid_spec=pltpu.PrefetchScalarGridSpec(
            num_scalar_prefetch=1,
            grid=(t // tm,),
            in_specs=[pl.BlockSpec((tm * SUBLANES, LANES), lambda i, pos: (i, 0)),
                      pl.BlockSpec(memory_space=pl.ANY)],
            out_specs=pl.BlockSpec(memory_space=pl.ANY),
            scratch_shapes=[pltpu.SemaphoreType.DMA],
        ),
        input_output_aliases={2: 0},
        compiler_params=_cparams(("arbitrary",)),
        name="moe_dispatch",
    )(pos_flat, hf, sorted_init)


def _combine_kernel(pos_ref, x1_ref, rinfo_ref, g2_ref, ys_ref, out_ref, buf, sem):
    tm, d = x1_ref.shape
    base = pl.program_id(0) * (TOP_K * tm)

    def issue(blk, carry):
        for u in range(DMA_UNROLL):
            t = blk * DMA_UNROLL + u
            for k in range(TOP_K):
                _row_copy(ys_ref, pos_ref[base + TOP_K * t + k], buf.at[k], t, sem).start(priority=k)
        return carry

    lax.fori_loop(0, tm // DMA_UNROLL, issue, 0)
    for k in range(TOP_K):
        pltpu.make_async_copy(ys_ref.at[pl.ds(0, tm * SUBLANES)], buf.at[k], sem).wait()
    rinfo = rinfo_ref[...]
    w0 = rinfo[:, 4:5]
    w1 = rinfo[:, 5:6]
    g2 = g2_ref[0]
    for c in range(d // LANES):
        cs = slice(c * LANES, (c + 1) * LANES)
        ffn = (w0 * buf[0, pl.ds(c, tm, stride=SUBLANES), :]
               + w1 * buf[1, pl.ds(c, tm, stride=SUBLANES), :])
        out_ref[:, cs] = x1_ref[:, cs] + g2[:, cs] * ffn


def _combine(pos_flat, x1, rinfo, g2, ys, seq, tm=512):
    t, d = x1.shape
    tpb = seq // tm
    return pl.pallas_call(
        _combine_kernel,
        out_shape=jax.ShapeDtypeStruct((t, d), F32),
        grid_spec=pltpu.PrefetchScalarGridSpec(
            num_scalar_prefetch=1,
            grid=(t // tm,),
            in_specs=[pl.BlockSpec((tm, d), lambda i, pos: (i, 0)),
                      pl.BlockSpec((tm, LANES), lambda i, pos: (i, 0)),
                      pl.BlockSpec((1, 1, d), lambda i, pos: (i // tpb, 0, 0)),
                      pl.BlockSpec(memory_space=pl.ANY)],
            out_specs=pl.BlockSpec((tm, d), lambda i, pos: (i, 0)),
            scratch_shapes=[pltpu.VMEM((TOP_K, tm * SUBLANES, LANES), F32), pltpu.SemaphoreType.DMA],
        ),
        compiler_params=_cparams(("arbitrary",)),
        name="moe_combine",
    )(pos_flat, x1, rinfo, g2, ys)


def _head_slab_columns():
    perm = np.zeros((N_Q_HEADS,), np.int32)
    for gp in range(N_KV_HEADS // 2):
        for qi in range(Q_PER_KV):
            for a in range(2):
                perm[(gp * Q_PER_KV + qi) * 2 + a] = (2 * gp + a) * Q_PER_KV + qi
    return (perm[:, None] * HEAD_DIM + np.arange(HEAD_DIM)[None, :]).reshape(-1)


def _attn_layer(x2, mod_i, gain_i, bias, w_qkv_all, w_o_all, w_gu_all, w_down_all, li,
                q_gain, k_gain, sinks, batch, seq):
    d = D_MODEL
    sh1, sc1, g1, sh2, sc2, g2 = [mod_i[:, k * d:(k + 1) * d] for k in range(6)]
    qg2 = jnp.tile(q_gain.astype(F32), 2)[None]
    kg2 = jnp.tile(k_gain.astype(F32), 2)[None]
    q, k, v = _qkv_proj(x2, jnp.stack([sh1, sc1], axis=1), gain_i[0:1], w_qkv_all, li, qg2, kg2, seq)
    o = _attention(q, k, v, bias, sinks, batch, seq)
    mv = jnp.stack([g1, sh2, sc2, g2], axis=1)
    return _ffn_dense(x2, o, w_o_all, mv, gain_i[1:2], w_gu_all, w_down_all, li, seq)


def _moe_slots(rinfo, cnt, n_tokens, moe_tm):
    counts = cnt[0, :N_EXPERTS].astype(jnp.int32)
    padded = ((counts + moe_tm - 1) // moe_tm) * moe_tm
    ends = jnp.cumsum(padded)
    base = ends - padded
    eidx = rinfo[:, 0:TOP_K].astype(jnp.int32)
    rank = rinfo[:, 2:2 + TOP_K].astype(jnp.int32)
    pos = jnp.sum(jnp.where(eidx[..., None] == jnp.arange(N_EXPERTS), base, 0), axis=-1) + rank
    n_tiles = (TOP_K * n_tokens) // moe_tm + N_EXPERTS
    tile_start = jnp.arange(n_tiles, dtype=jnp.int32) * moe_tm
    tile_expert = jnp.minimum(jnp.sum(tile_start[:, None] >= ends[None, :], axis=-1),
                              N_EXPERTS - 1).astype(jnp.int32)
    n_used = ends[-1] // moe_tm
    tile_expert = jnp.where(tile_start < ends[-1], tile_expert, tile_expert[n_used - 1])
    return pos.reshape(-1), jnp.concatenate([tile_expert, n_used[None].astype(jnp.int32)])


def _ssm_layer(x2, mod_i, gain_i, w_zx_all, w_dt_all, w_out_all, w_gu_all, w_down_all, li,
               conv_w, conv_b, dt_bias, a_log, d_skip, norm_w, w_router, sorted_init,
               batch, seq, moe_tm):
    d = D_MODEL
    sh1, sc1, g1, sh2, sc2, g2 = [mod_i[:, k * d:(k + 1) * d] for k in range(6)]
    zx, dt = _ssm_in_proj(x2, jnp.stack([sh1, sc1], axis=1), gain_i[0:1], w_zx_all, w_dt_all, li, seq)
    y = _ssd(zx, dt, conv_w, conv_b, dt_bias, a_log, d_skip, norm_w, batch, seq)

    wr = jnp.pad(w_router.astype(F32), ((0, 0), (0, LANES - N_EXPERTS)))
    wr_hi = wr.astype(BF16)
    wr_lo = (wr - wr_hi.astype(F32)).astype(BF16)
    x1, hf, rinfo, cnt = _router(x2, y, w_out_all, li, jnp.stack([g1, sh2, sc2], axis=1),
                                 gain_i[1:2], jnp.concatenate([wr_hi, wr_lo], axis=1), seq)
    pos_flat, tile_expert = _moe_slots(rinfo, cnt, x2.shape[0], moe_tm)
    xs = _dispatch(pos_flat, hf, sorted_init)
    ys = _ffn_moe(xs, tile_expert, w_gu_all, w_down_all, li, moe_tm)
    return _combine(pos_flat, x1, rinfo, g2[:, None, :], ys, seq), xs


def kernel(x, c, ada_w, ada_b, norm_gain, rel_bias, attn_w_qkv, attn_q_gain, attn_k_gain, attn_sinks, attn_w_o, ssm_w_in, ssm_conv_w, ssm_conv_b, ssm_dt_bias, ssm_a_log, ssm_d, ssm_norm_w, ssm_w_out, ffn_w_gu, ffn_w_down, moe_w_router, moe_w_gu, moe_w_down):
    batch, seq, d = x.shape
    moe_tm = 1024
    mod = _adaln_mod(c.astype(F32), ada_w.astype(F32), ada_b.astype(F32))
    bias = _bias_table(rel_bias)

    col = _head_slab_columns()
    nq = N_Q_HEADS * HEAD_DIM
    n_zx = 2 * D_INNER + D_BC
    w_qkv_all = jnp.concatenate([attn_w_qkv[:, :, :nq][:, :, col], attn_w_qkv[:, :, nq:]],
                                axis=2).astype(BF16)
    w_o_all = attn_w_o[:, col, :].astype(BF16)
    ffn_gu_all = ffn_w_gu.astype(BF16)
    ffn_down_all = ffn_w_down.astype(BF16)
    w_zx_all = ssm_w_in[:, :, :n_zx].astype(BF16)
    w_dt_all = jnp.pad(ssm_w_in[:, :, n_zx:], ((0, 0), (0, 0), (0, LANES - N_SSM_HEADS))).astype(BF16)
    w_out_all = ssm_w_out.astype(BF16)
    moe_gu_all = moe_w_gu.astype(BF16)
    moe_down_all = moe_w_down.astype(BF16)

    n_sorted = TOP_K * batch * seq + N_EXPERTS * moe_tm
    sorted_buf = jnp.zeros((n_sorted * SUBLANES, LANES), F32)

    x2 = x.reshape(batch * seq, d).astype(F32)
    for i in range(DEPTH):
        j = i // 2
        gain_i = norm_gain[i].astype(F32)
        if i % 2 == 0:
            x2 = _attn_layer(x2, mod[i], gain_i, bias, w_qkv_all, w_o_all, ffn_gu_all, ffn_down_all,
                             j, attn_q_gain[j], attn_k_gain[j], attn_sinks[j], batch, seq)
        else:
            x2, sorted_buf = _ssm_layer(x2, mod[i], gain_i, w_zx_all, w_dt_all, w_out_all, moe_gu_all,
                                        moe_down_all, j, ssm_conv_w[j], ssm_conv_b[j], ssm_dt_bias[j],
                                        ssm_a_log[j], ssm_d[j], ssm_norm_w[j], moe_w_router[j],
                                        sorted_buf, batch, seq, moe_tm)
    return x2.reshape(batch, seq, d).astype(x.dtype)
```

```python
import functools
import math

import numpy as np
import jax
import jax.numpy as jnp
from jax import lax
from jax.experimental import pallas as pl
from jax.experimental.pallas import tpu as pltpu

F32 = jnp.float32
BF16 = jnp.bfloat16

D_MODEL = 1024
DEPTH = 4
N_Q_HEADS = 16
N_KV_HEADS = 4
HEAD_DIM = 64
Q_PER_KV = 4
ATTN_BLOCK = 128
N_BUCKETS = 32
MAX_DISTANCE = 128
D_INNER = 2048
N_SSM_HEADS = 32
N_SSM_GROUPS = 4
D_STATE = 128
CONV_WIDTH = 4
SSM_CHUNK = 128
D_BC = 2 * N_SSM_GROUPS * D_STATE
D_FF = 3584
N_EXPERTS = 8
TOP_K = 2
NORM_EPS = 1e-6

LANES = 128
SUBLANES = 8
VMEM_LIMIT = 56 * 1024 * 1024
NEG_INF = float("-inf")


def _cparams(sem):
    return pltpu.CompilerParams(dimension_semantics=sem, vmem_limit_bytes=VMEM_LIMIT)


def _sigmoid(v):
    return 1.0 / (1.0 + jnp.exp(-v))


def _rms_mod(x, gain, scale, shift):
    ms = jnp.mean(x * x, axis=-1, keepdims=True)
    y = x * lax.rsqrt(ms + NORM_EPS)
    return (y * gain) * (1.0 + scale) + shift


def _split2(v):
    hi = v.astype(BF16)
    lo = (v - hi.astype(F32)).astype(BF16)
    return hi, lo


def _split3(v):
    hi = v.astype(BF16)
    r = v - hi.astype(F32)
    mid = r.astype(BF16)
    lo = (r - mid.astype(F32)).astype(BF16)
    return hi, mid, lo


def _dot(a, b):
    return jnp.dot(a, b, preferred_element_type=F32)


def _store_tile_rows(ref, row0, val):
    n = val.shape[0]
    for c in range(val.shape[1] // LANES):
        ref[pl.ds(row0 * SUBLANES + c, n, stride=SUBLANES), :] = val[:, c * LANES:(c + 1) * LANES]


def _load_tile_rows(ref, n):
    return [ref[pl.ds(c, n, stride=SUBLANES), :] for c in range(SUBLANES)]


def _mod_kernel(c_ref, w_ref, b_ref, o_ref):
    c = c_ref[...]
    cond = c * _sigmoid(c)
    o_ref[0] = _dot(cond.astype(BF16), w_ref[0].astype(BF16)) + b_ref[0]


def _adaln_mod(c, ada_w, ada_b):
    depth, d, n = ada_w.shape
    b = c.shape[0]
    tn = 1024
    c_pad = jnp.zeros((SUBLANES, d), F32).at[:b].set(c)
    out = pl.pallas_call(
        _mod_kernel,
        out_shape=jax.ShapeDtypeStruct((depth, SUBLANES, n), F32),
        grid=(depth, n // tn),
        in_specs=[
            pl.BlockSpec((SUBLANES, d), lambda i, j: (0, 0)),
            pl.BlockSpec((1, d, tn), lambda i, j: (i, 0, j)),
            pl.BlockSpec((1, 1, tn), lambda i, j: (i, 0, j)),
        ],
        out_specs=pl.BlockSpec((1, SUBLANES, tn), lambda i, j: (i, 0, j)),
        compiler_params=_cparams(("arbitrary", "arbitrary")),
        name="adaln_mod",
    )(c_pad, ada_w, ada_b.reshape(depth, 1, n))
    return out[:, :b]


def _t5_bucket_table():
    qi = np.arange(ATTN_BLOCK)[:, None]
    kj = np.arange(2 * ATTN_BLOCK)[None, :]
    dist = qi + ATTN_BLOCK - kj
    max_exact = N_BUCKETS // 2
    d = np.maximum(dist, 0)
    df = np.maximum(d, 1).astype(np.float64)
    large = max_exact + (np.log(df / max_exact) / math.log(MAX_DISTANCE / max_exact)
                         * (N_BUCKETS - max_exact)).astype(np.int32)
    large = np.minimum(large, N_BUCKETS - 1)
    bucket = np.where(d < max_exact, d, large)
    valid = (dist >= 0) & (dist < ATTN_BLOCK)
    return np.where(valid, bucket, -1).astype(np.int32)


def _bias_kernel(rb_ref, bkt_ref, o_ref):
    j = pl.program_id(0)
    gp = j // Q_PER_KV
    qi = j % Q_PER_KV
    bkt = bkt_ref[...]
    for a in range(2):
        h = (2 * gp + a) * Q_PER_KV + qi
        acc = jnp.full(bkt.shape, NEG_INF, F32)
        for b in range(N_BUCKETS):
            acc = jnp.where(bkt == b, rb_ref[b, h], acc)
        o_ref[0, a * ATTN_BLOCK:(a + 1) * ATTN_BLOCK, :] = acc


def _bias_table(rel_bias):
    bkt = jnp.asarray(_t5_bucket_table())
    n_slots = N_Q_HEADS // 2
    return pl.pallas_call(
        _bias_kernel,
        out_shape=jax.ShapeDtypeStruct((n_slots, 2 * ATTN_BLOCK, 2 * ATTN_BLOCK), F32),
        grid=(n_slots,),
        in_specs=[
            pl.BlockSpec(memory_space=pltpu.SMEM),
            pl.BlockSpec((ATTN_BLOCK, 2 * ATTN_BLOCK), lambda j: (0, 0)),
        ],
        out_specs=pl.BlockSpec((1, 2 * ATTN_BLOCK, 2 * ATTN_BLOCK), lambda j: (j, 0, 0)),
        compiler_params=_cparams(("arbitrary",)),
        name="t5_bias_table",
    )(rel_bias.astype(F32), bkt)


def _qkv_kernel(x_ref, ms_ref, gain_ref, w_ref, qg_ref, kg_ref, q_ref, k_ref, v_ref):
    tm = x_ref.shape[0]
    ms = ms_ref[0]
    hm = _rms_mod(x_ref[...], gain_ref[...], ms[1:2], ms[0:1]).astype(BF16)
    lo = lax.broadcasted_iota(jnp.int32, (tm, LANES), 1) < HEAD_DIM

    def head_norm(slab, g):
        sq = slab * slab
        s_lo = jnp.sum(jnp.where(lo, sq, 0.0), axis=-1, keepdims=True)
        s_hi = jnp.sum(jnp.where(lo, 0.0, sq), axis=-1, keepdims=True)
        msq = jnp.where(lo, s_lo, s_hi) * (1.0 / HEAD_DIM)
        return slab * lax.rsqrt(msq + NORM_EPS) * g

    nq = N_Q_HEADS * HEAD_DIM
    nk = N_KV_HEADS * HEAD_DIM
    q_all = _dot(hm, w_ref[:, :nq])
    for j in range(nq // LANES):
        sl = slice(j * LANES, (j + 1) * LANES)
        q_ref[:, sl] = (head_norm(q_all[:, sl], qg_ref[...]) * (HEAD_DIM ** -0.5)).astype(BF16)
    k_all = _dot(hm, w_ref[:, nq:nq + nk])
    for j in range(nk // LANES):
        sl = slice(j * LANES, (j + 1) * LANES)
        k_ref[:, sl] = head_norm(k_all[:, sl], kg_ref[...]).astype(BF16)
    v_ref[...] = _dot(hm, w_ref[:, nq + nk:]).astype(BF16)


def _qkv_proj(x2, ms, gain, w_qkv_all, li, qg2, kg2, seq, tm=512):
    t, d = x2.shape
    nq = N_Q_HEADS * HEAD_DIM
    nk = N_KV_HEADS * HEAD_DIM
    tpb = seq // tm
    return pl.pallas_call(
        _qkv_kernel,
        out_shape=(jax.ShapeDtypeStruct((t, nq), BF16),
                   jax.ShapeDtypeStruct((t, nk), BF16),
                   jax.ShapeDtypeStruct((t, nk), BF16)),
        grid=(t // tm,),
        in_specs=[
            pl.BlockSpec((tm, d), lambda i: (i, 0)),
            pl.BlockSpec((1, 2, d), lambda i: (i // tpb, 0, 0)),
            pl.BlockSpec((1, d), lambda i: (0, 0)),
            pl.BlockSpec((None, d, nq + 2 * nk), lambda i: (li, 0, 0)),
            pl.BlockSpec((1, LANES), lambda i: (0, 0)),
            pl.BlockSpec((1, LANES), lambda i: (0, 0)),
        ],
        out_specs=(pl.BlockSpec((tm, nq), lambda i: (i, 0)),
                   pl.BlockSpec((tm, nk), lambda i: (i, 0)),
                   pl.BlockSpec((tm, nk), lambda i: (i, 0))),
        compiler_params=_cparams(("arbitrary",)),
        name="attn_qkv",
    )(x2, ms, gain, w_qkv_all, qg2, kg2)


def _attn_kernel(sink_ref, q_ref, kc_ref, kp_ref, vc_ref, vp_ref, bias_ref, o_ref):
    blk = ATTN_BLOCK
    first = pl.program_id(1) == 0
    lane = lax.broadcasted_iota(jnp.int32, (blk, LANES), 1)
    lo = lane < HEAD_DIM
    mask_lo = jnp.where(lo, 1.0, 0.0).astype(BF16)
    mask_hi = jnp.where(lo, 0.0, 1.0).astype(BF16)
    row = lax.broadcasted_iota(jnp.int32, (2 * blk, 1), 0)
    nt = (((1,), (1,)), ((), ()))
    for gp in range(N_KV_HEADS // 2):
        ksl = slice(gp * LANES, (gp + 1) * LANES)
        kc = kc_ref[:, ksl]
        kp = kp_ref[:, ksl]
        vc = vc_ref[:, ksl]
        vp = vp_ref[:, ksl]
        for qi in range(Q_PER_KV):
            j = gp * Q_PER_KV + qi
            sl = slice(j * LANES, (j + 1) * LANES)
            qs = q_ref[:, sl]
            q2 = jnp.concatenate([qs * mask_lo, qs * mask_hi], axis=0)
            s_p = lax.dot_general(q2, kp, nt, preferred_element_type=F32)
            s_c = lax.dot_general(q2, kc, nt, preferred_element_type=F32)
            bias = bias_ref[j]
            s_p = jnp.where(first, NEG_INF, s_p + bias[:, :blk])
            s_c = s_c + bias[:, blk:]
            h_a = (2 * gp) * Q_PER_KV + qi
            h_b = (2 * gp + 1) * Q_PER_KV + qi
            sink = jnp.where(row < blk, sink_ref[h_a], sink_ref[h_b])
            m = jnp.maximum(jnp.max(jnp.maximum(s_p, s_c), axis=-1, keepdims=True), sink)
            p_p = jnp.exp(s_p - m)
            p_c = jnp.exp(s_c - m)
            den = jnp.sum(p_p + p_c, axis=-1, keepdims=True) + jnp.exp(sink - m)
            pv = _dot(p_p.astype(BF16), vp) + _dot(p_c.astype(BF16), vc)
            pv = pv / den
            o_ref[:, sl] = jnp.where(lo, pv[:blk], pv[blk:]).astype(BF16)


def _attention(q, k, v, bias, sinks, batch, seq):
    t, nq = q.shape
    nk = k.shape[1]
    blk = ATTN_BLOCK
    nb = seq // blk
    cur = lambda b, n, *_: (b * nb + n, 0)
    prev = lambda b, n, *_: (b * nb + jnp.maximum(n - 1, 0), 0)
    return pl.pallas_call(
        _attn_kernel,
        out_shape=jax.ShapeDtypeStruct((t, nq), BF16),
        grid=(batch, nb),
        in_specs=[
            pl.BlockSpec(memory_space=pltpu.SMEM),
            pl.BlockSpec((blk, nq), cur),
            pl.BlockSpec((blk, nk), cur),
            pl.BlockSpec((blk, nk), prev),
            pl.BlockSpec((blk, nk), cur),
            pl.BlockSpec((blk, nk), prev),
            pl.BlockSpec(bias.shape, lambda b, n: (0, 0, 0)),
        ],
        out_specs=pl.BlockSpec((blk, nq), cur),
        compiler_params=_cparams(("arbitrary", "arbitrary")),
        name="swa_attention",
    )(sinks.astype(F32), q, k, k, v, v, bias)


def _swiglu_step(hf, wg_ref, wu_ref, wd_ref, acc_s):
    g = _dot(hf, wg_ref[...])
    u = _dot(hf, wu_ref[...])
    a = (g * _sigmoid(g) * u).astype(BF16)
    acc_s[...] += _dot(a, wd_ref[...])


def _ffn_dense_kernel(x_ref, o_ref, wo_ref, mv_ref, gain_ref, wg_ref, wu_ref, wd_ref,
                      out_ref, hf_s, acc_s):
    j = pl.program_id(1)

    @pl.when(j == 0)
    def _():
        mv = mv_ref[0]
        x1 = x_ref[...] + mv[0:1] * _dot(o_ref[...], wo_ref[...])
        out_ref[...] = x1
        hf_s[...] = _rms_mod(x1, gain_ref[...], mv[2:3], mv[1:2]).astype(BF16)
        acc_s[...] = jnp.zeros_like(acc_s)

    _swiglu_step(hf_s[...], wg_ref, wu_ref, wd_ref, acc_s)

    @pl.when(j == pl.num_programs(1) - 1)
    def _():
        out_ref[...] = out_ref[...] + mv_ref[0][3:4] * acc_s[...]


def _ffn_dense(x2, o, wo_all, mv, gain, w_gu_all, w_down_all, li, seq, tm=1024, tf=512):
    t, d = x2.shape
    f = w_down_all.shape[1]
    nj = f // tf
    tpb = seq // tm
    return pl.pallas_call(
        _ffn_dense_kernel,
        out_shape=jax.ShapeDtypeStruct((t, d), F32),
        grid=(t // tm, nj),
        in_specs=[
            pl.BlockSpec((tm, d), lambda i, j: (i, 0)),
            pl.BlockSpec((tm, o.shape[1]), lambda i, j: (i, 0)),
            pl.BlockSpec((None,) + wo_all.shape[1:], lambda i, j: (li, 0, 0)),
            pl.BlockSpec((1, 4, d), lambda i, j: (i // tpb, 0, 0)),
            pl.BlockSpec((1, d), lambda i, j: (0, 0)),
            pl.BlockSpec((None, d, tf), lambda i, j: (li, 0, j)),
            pl.BlockSpec((None, d, tf), lambda i, j: (li, 0, nj + j)),
            pl.BlockSpec((None, tf, d), lambda i, j: (li, j, 0)),
        ],
        out_specs=pl.BlockSpec((tm, d), lambda i, j: (i, 0)),
        scratch_shapes=[pltpu.VMEM((tm, d), BF16), pltpu.VMEM((tm, d), F32)],
        compiler_params=_cparams(("arbitrary", "arbitrary")),
        name="wo_ffn_dense",
    )(x2, o, wo_all, mv, gain, w_gu_all, w_gu_all, w_down_all)


def _ffn_moe_kernel(te_ref, xs_ref, wg_ref, wu_ref, wd_ref, out_ref, hf_s, acc_s):
    i = pl.program_id(0)
    j = pl.program_id(1)
    used = i < te_ref[pl.num_programs(0)]

    tm = hf_s.shape[0]

    @pl.when(used & (j == 0))
    def _():
        for c, part in enumerate(_load_tile_rows(xs_ref, tm)):
            hf_s[:, c * LANES:(c + 1) * LANES] = part.astype(BF16)
        acc_s[...] = jnp.zeros_like(acc_s)

    @pl.when(used)
    def _():
        _swiglu_step(hf_s[...], wg_ref, wu_ref, wd_ref, acc_s)

    @pl.when(j == pl.num_programs(1) - 1)
    def _():
        @pl.when(used)
        def _():
            _store_tile_rows(out_ref, 0, acc_s[...])

        @pl.when(jnp.logical_not(used))
        def _():
            out_ref[...] = jnp.zeros_like(out_ref)


def _ffn_moe(xs, tile_expert, w_gu_all, w_down_all, li, tm, tf=896):
    r = xs.shape[0] // SUBLANES
    d = w_gu_all.shape[2]
    f = w_down_all.shape[2]
    nj = f // tf
    nt = r // tm

    def jj(i, j, te):
        return jnp.where(i < te[nt], j, 0)

    return pl.pallas_call(
        _ffn_moe_kernel,
        out_shape=jax.ShapeDtypeStruct((r * SUBLANES, LANES), F32),
        grid_spec=pltpu.PrefetchScalarGridSpec(
            num_scalar_prefetch=1,
            grid=(nt, nj),
            in_specs=[
                pl.BlockSpec((tm * SUBLANES, LANES), lambda i, j, te: (i, 0)),
                pl.BlockSpec((None, None, d, tf), lambda i, j, te: (li, te[i], 0, jj(i, j, te))),
                pl.BlockSpec((None, None, d, tf), lambda i, j, te: (li, te[i], 0, nj + jj(i, j, te))),
                pl.BlockSpec((None, None, tf, d), lambda i, j, te: (li, te[i], jj(i, j, te), 0)),
            ],
            out_specs=pl.BlockSpec((tm * SUBLANES, LANES), lambda i, j, te: (i, 0)),
            scratch_shapes=[pltpu.VMEM((tm, d), BF16), pltpu.VMEM((tm, d), F32)],
        ),
        compiler_params=_cparams(("arbitrary", "arbitrary")),
        name="moe_expert_ffn",
    )(tile_expert, xs, w_gu_all, w_gu_all, w_down_all)


def _inproj_kernel(x_ref, ms_ref, gain_ref, wz_ref, wx_ref, wdt_ref, cw_ref, cb_ref,
                   z_ref, xbc_ref, dt_ref, hm_s, carry_s, *, tiles_per_seq):
    i = pl.program_id(0)
    j = pl.program_id(1)
    tm = z_ref.shape[0]

    @pl.when((i == 0) & (j == 0))
    def _():
        carry_s[...] = jnp.zeros_like(carry_s)

    @pl.when(j == 0)
    def _():
        ms = ms_ref[0]
        hm = _rms_mod(x_ref[...], gain_ref[...], ms[1:2], ms[0:1]).astype(BF16)
        hm_s[...] = hm
        dt_ref[...] = _dot(hm, wdt_ref[...])

    hm = hm_s[...]
    strip = 2 * LANES
    rowi = lax.broadcasted_iota(jnp.int32, (SUBLANES, strip), 0)
    first = i % tiles_per_seq == 0
    for c0 in range(0, xbc_ref.shape[1], strip):
        cs = slice(c0, c0 + strip)
        res = _dot(hm, wx_ref[:, cs])
        history = jnp.where(first, 0.0, carry_s[j, :, cs])
        acc = res * cw_ref[CONV_WIDTH - 1:CONV_WIDTH, cs] + cb_ref[:, cs]
        for s in range(1, CONV_WIDTH):
            down = pltpu.roll(res, s, 0)
            head = jnp.where(rowi < s, pltpu.roll(history, s, 0), down[:SUBLANES])
            shifted = jnp.concatenate([head, down[SUBLANES:]], axis=0)
            acc = acc + shifted * cw_ref[CONV_WIDTH - 1 - s:CONV_WIDTH - s, cs]
        carry_s[j, :, cs] = res[tm - SUBLANES:]
        xbc_ref[:, cs] = (acc * _sigmoid(acc)).astype(BF16)
    for c0 in range(0, z_ref.shape[1], strip):
        cs = slice(c0, c0 + strip)
        z_ref[:, cs] = _dot(hm, wz_ref[:, cs]).astype(BF16)


def _ssm_in_proj(x2, ms, gain, w_z_all, w_xbc_all, w_dt_all, li, conv_w, conv_b, seq, tm=1024, nj=2):
    t, d = x2.shape
    nz = w_z_all.shape[2] // nj
    nx = w_xbc_all.shape[2] // nj
    tpb = seq // tm
    return pl.pallas_call(
        functools.partial(_inproj_kernel, tiles_per_seq=tpb),
        out_shape=(jax.ShapeDtypeStruct((t, nz * nj), BF16), jax.ShapeDtypeStruct((t, nx * nj), BF16),
                   jax.ShapeDtypeStruct((t, LANES), F32)),
        grid=(t // tm, nj),
        in_specs=[
            pl.BlockSpec((tm, d), lambda i, j: (i, 0)),
            pl.BlockSpec((1, 2, d), lambda i, j: (i // tpb, 0, 0)),
            pl.BlockSpec((1, d), lambda i, j: (0, 0)),
            pl.BlockSpec((None, d, nz), lambda i, j: (li, 0, j)),
            pl.BlockSpec((None, d, nx), lambda i, j: (li, 0, j)),
            pl.BlockSpec((None, d, LANES), lambda i, j: (li, 0, 0)),
            pl.BlockSpec((CONV_WIDTH, nx), lambda i, j: (0, j)),
            pl.BlockSpec((1, nx), lambda i, j: (0, j)),
        ],
        out_specs=(pl.BlockSpec((tm, nz), lambda i, j: (i, j)),
                   pl.BlockSpec((tm, nx), lambda i, j: (i, j)),
                   pl.BlockSpec((tm, LANES), lambda i, j: (i, 0))),
        scratch_shapes=[pltpu.VMEM((tm, d), BF16), pltpu.VMEM((nj, SUBLANES, nx), F32)],
        compiler_params=_cparams(("arbitrary", "arbitrary")),
        name="ssm_in_proj",
    )(x2, ms, gain, w_z_all, w_xbc_all, w_dt_all, conv_w.astype(F32), conv_b.astype(F32)[None])


def _ssd_kernel(z_ref, xs_ref, bc_ref, dt_ref, dtb_ref, alog_ref, dskip_ref, nw_ref, r64_ref,
                y_ref, state_s, ybuf_s):
    L = SSM_CHUNK
    gw = D_INNER // N_SSM_GROUPS
    hpg = N_SSM_HEADS // N_SSM_GROUPS

    @pl.when(pl.program_id(1) == 0)
    def _():
        state_s[...] = jnp.zeros_like(state_s)

    dtv = jax.nn.softplus(dt_ref[...] + dtb_ref[...])
    a = -jnp.exp(alog_ref[...])
    dta = dtv * a
    ri = lax.broadcasted_iota(jnp.int32, (L, L), 0)
    ci = lax.broadcasted_iota(jnp.int32, (L, L), 1)
    causal = ri >= ci
    tri = jnp.where(causal, 1.0, 0.0).astype(BF16)
    p0, p1, p2 = _split3(dta)
    acs = _dot(tri, p0) + _dot(tri, p1) + _dot(tri, p2)
    acs_t = acs.T
    eacs = jnp.exp(acs)
    dte = jnp.exp(acs[L - 1:L, :] - acs)
    stacked = jnp.concatenate([dtv, eacs, dte], axis=0)
    s_hi, s_lo = _split2(stacked)
    expd = _dot(s_hi, r64_ref[...]) + _dot(s_lo, r64_ref[...])

    lo = lax.broadcasted_iota(jnp.int32, (L, LANES), 1) < (LANES // 2)
    for g in range(N_SSM_GROUPS):
        gs = slice(g * gw, (g + 1) * gw)
        b_g = bc_ref[:, g * D_STATE:(g + 1) * D_STATE]
        c_b = bc_ref[:, (N_SSM_GROUPS + g) * D_STATE:(N_SSM_GROUPS + g + 1) * D_STATE]
        b_t = b_g.astype(F32).T.astype(BF16)
        cb = _dot(c_b, b_t)
        xs_g = xs_ref[:, gs].astype(F32)
        xd = xs_g * expd[0:L, gs]
        xd_b = xd.astype(BF16)
        st_prev = state_s[g]
        y_off = _dot(c_b, st_prev.astype(BF16)) * expd[L:2 * L, gs]
        xdw = (xd * expd[2 * L:3 * L, gs]).astype(BF16)
        state_s[g] = expd[2 * L - 1:2 * L, gs] * st_prev + _dot(b_t, xdw)
        for jp in range(hpg // 2):
            ps = slice(jp * LANES, (jp + 1) * LANES)
            slab = xd_b[:, ps]
            halves = []
            for half in range(2):
                hh = g * hpg + 2 * jp + half
                diff = acs[:, hh:hh + 1] - acs_t[hh:hh + 1, :]
                dec = jnp.exp(jnp.where(causal, diff, NEG_INF))
                halves.append(_dot((cb * dec).astype(BF16), slab))
            y = jnp.where(lo, halves[0], halves[1]) + y_off[:, ps]
            cols = slice(g * gw + jp * LANES, g * gw + (jp + 1) * LANES)
            ybuf_s[:, cols] = y + dskip_ref[:, cols] * xs_g[:, ps]

    z = z_ref[...].astype(F32)
    yg = ybuf_s[...] * (z * _sigmoid(z))
    ms = jnp.mean(yg * yg, axis=-1, keepdims=True)
    y_ref[...] = (yg * lax.rsqrt(ms + NORM_EPS) * nw_ref[...]).astype(BF16)


def _ssd(z, xbc, dt, dt_bias, a_log, d_skip, norm_w, batch, seq):
    t = z.shape[0]
    L = SSM_CHUNK
    nc = seq // L
    di, dbc = D_INNER, D_BC
    pad = LANES - N_SSM_HEADS
    dtb = jnp.pad(dt_bias.astype(F32), (0, pad))[None]
    alog = jnp.pad(a_log.astype(F32), (0, pad))[None]
    dskip = jnp.repeat(d_skip.astype(F32), di // N_SSM_HEADS)[None]
    nw = norm_w.astype(F32)[None]
    r64 = jnp.asarray((np.arange(LANES)[:, None] == (np.arange(di)[None, :] // (di // N_SSM_HEADS)))
                      .astype(np.float32)).astype(BF16)
    row = lambda b, c: (b * nc + c, 0)
    const = lambda b, c: (0, 0)
    return pl.pallas_call(
        _ssd_kernel,
        out_shape=jax.ShapeDtypeStruct((t, di), BF16),
        grid=(batch, nc),
        in_specs=[
            pl.BlockSpec((L, di), row),
            pl.BlockSpec((L, di), row),
            pl.BlockSpec((L, dbc), lambda b, c: (b * nc + c, di // dbc)),
            pl.BlockSpec((L, LANES), row),
            pl.BlockSpec((1, LANES), const),
            pl.BlockSpec((1, LANES), const),
            pl.BlockSpec((1, di), const),
            pl.BlockSpec((1, di), const),
            pl.BlockSpec((LANES, di), const),
        ],
        out_specs=pl.BlockSpec((L, di), row),
        scratch_shapes=[
            pltpu.VMEM((N_SSM_GROUPS, D_STATE, di // N_SSM_GROUPS), F32),
            pltpu.VMEM((L, di), F32),
        ],
        compiler_params=_cparams(("arbitrary", "arbitrary")),
        name="ssd_scan",
    )(z, xbc, xbc, dt, dtb, alog, dskip, nw, r64)


ROUTER_SUB = 512


def _router_kernel(x_ref, y_ref, wout_ref, mv_ref, gain_ref, wr_ref,
                   x1_ref, hf_ref, rinfo_ref, cnt_ref, carry_s):
    tm = x_ref.shape[0]
    sub = ROUTER_SUB

    @pl.when(pl.program_id(0) == 0)
    def _():
        carry_s[...] = jnp.zeros_like(carry_s)

    mv = mv_ref[0]
    lane = lax.broadcasted_iota(jnp.int32, (sub, LANES), 1)
    lanef = lane.astype(F32)
    ri = lax.broadcasted_iota(jnp.int32, (sub, sub), 0)
    ci = lax.broadcasted_iota(jnp.int32, (sub, sub), 1)
    before = jnp.where(ri > ci, 1.0, 0.0).astype(BF16)
    carry = carry_s[...]
    for h in range(tm // sub):
        rs = slice(h * sub, (h + 1) * sub)
        x1 = x_ref[rs, :] + mv[0:1] * _dot(y_ref[rs, :], wout_ref[...])
        x1_ref[rs, :] = x1
        hf = _rms_mod(x1, gain_ref[...], mv[2:3], mv[1:2])
        _store_tile_rows(hf_ref, h * sub, hf)
        h_hi, h_lo = _split2(hf)
        hw = _dot(h_hi, wr_ref[...])
        logits = hw[:, :LANES] + hw[:, LANES:] + _dot(h_lo, wr_ref[:, :LANES])
        lg = jnp.where(lane < N_EXPERTS, logits, NEG_INF)
        v0 = jnp.max(lg, axis=-1, keepdims=True)
        i0 = jnp.min(jnp.where(lg == v0, lanef, float(LANES)), axis=-1, keepdims=True)
        lg1 = jnp.where(lanef == i0, NEG_INF, lg)
        v1 = jnp.max(lg1, axis=-1, keepdims=True)
        i1 = jnp.min(jnp.where(lg1 == v1, lanef, float(LANES)), axis=-1, keepdims=True)
        e1 = jnp.exp(v1 - v0)
        w0 = 1.0 / (1.0 + e1)
        w1 = e1 / (1.0 + e1)
        oh0 = jnp.where(lanef == i0, 1.0, 0.0)
        oh1 = jnp.where(lanef == i1, 1.0, 0.0)
        pre = _dot(before, jnp.concatenate([oh0, oh1], axis=1).astype(BF16))
        cnt0 = jnp.sum(oh0, axis=0, keepdims=True)
        cnt1 = jnp.sum(oh1, axis=0, keepdims=True)
        rank0 = jnp.sum(oh0 * (pre[:, :LANES] + carry), axis=-1, keepdims=True)
        rank1 = jnp.sum(oh1 * (pre[:, LANES:] + carry + cnt0), axis=-1, keepdims=True)
        carry = carry + cnt0 + cnt1
        rinfo_ref[rs, :] = jnp.where(lane == 0, i0, jnp.where(lane == 1, i1, jnp.where(
            lane == 2, rank0, jnp.where(lane == 3, rank1, jnp.where(
                lane == 4, w0, jnp.where(lane == 5, w1, 0.0))))))
    carry_s[...] = carry
    cnt_ref[...] = carry


def _router(x2, y, wout_all, li, mv, gain, wr, seq, tm=1024):
    t, d = x2.shape
    tpb = seq // tm
    return pl.pallas_call(
        _router_kernel,
        out_shape=(jax.ShapeDtypeStruct((t, d), F32),
                   jax.ShapeDtypeStruct((t * SUBLANES, LANES), F32),
                   jax.ShapeDtypeStruct((t, LANES), F32), jax.ShapeDtypeStruct((1, LANES), F32)),
        grid=(t // tm,),
        in_specs=[
            pl.BlockSpec((tm, d), lambda i: (i, 0)),
            pl.BlockSpec((tm, y.shape[1]), lambda i: (i, 0)),
            pl.BlockSpec((None,) + wout_all.shape[1:], lambda i: (li, 0, 0)),
            pl.BlockSpec((1, 3, d), lambda i: (i // tpb, 0, 0)),
            pl.BlockSpec((1, d), lambda i: (0, 0)),
            pl.BlockSpec((d, 2 * LANES), lambda i: (0, 0)),
        ],
        out_specs=(pl.BlockSpec((tm, d), lambda i: (i, 0)),
                   pl.BlockSpec((tm * SUBLANES, LANES), lambda i: (i, 0)),
                   pl.BlockSpec((tm, LANES), lambda i: (i, 0)),
                   pl.BlockSpec((1, LANES), lambda i: (0, 0))),
        scratch_shapes=[pltpu.VMEM((1, LANES), F32)],
        compiler_params=_cparams(("arbitrary",)),
        name="ssm_out_router",
    )(x2, y, wout_all, mv, gain, wr)


DMA_UNROLL = 8


def _row_copy(src, src_row, dst, dst_row, sem):
    return pltpu.make_async_copy(
        src.at[pl.ds(pl.multiple_of(src_row * SUBLANES, SUBLANES), SUBLANES)],
        dst.at[pl.ds(pl.multiple_of(dst_row * SUBLANES, SUBLANES), SUBLANES)], sem)


def _dispatch_kernel(pos_ref, hf_ref, sorted_in_ref, sorted_ref, sem):
    del sorted_in_ref
    tm = hf_ref.shape[0] // SUBLANES
    base = pl.program_id(0) * (TOP_K * tm)

    def issue(blk, carry):
        for u in range(DMA_UNROLL):
            t = blk * DMA_UNROLL + u
            for k in range(TOP_K):
                _row_copy(hf_ref, t, sorted_ref, pos_ref[base + TOP_K * t + k], sem).start(priority=k)
        return carry

    lax.fori_loop(0, tm // DMA_UNROLL, issue, 0)
    for k in range(TOP_K):
        pltpu.make_async_copy(hf_ref, sorted_ref.at[pl.ds(0, tm * SUBLANES)], sem).wait()


def _dispatch(pos_flat, hf, sorted_init, tm=512):
    t = hf.shape[0] // SUBLANES
    return pl.pallas_call(
        _dispatch_kernel,
        out_shape=jax.ShapeDtypeStruct(sorted_init.shape, hf.dtype),
        grid_spec=pltpu.PrefetchScalarGridSpec(
            num_scalar_prefetch=1,
            grid=(t // tm,),
            in_specs=[pl.BlockSpec((tm * SUBLANES, LANES), lambda i, pos: (i, 0)),
                      pl.BlockSpec(memory_space=pl.ANY)],
            out_specs=pl.BlockSpec(memory_space=pl.ANY),
            scratch_shapes=[pltpu.SemaphoreType.DMA],
        ),
        input_output_aliases={2: 0},
        compiler_params=_cparams(("arbitrary",)),
        name="moe_dispatch",
    )(pos_flat, hf, sorted_init)


def _combine_kernel(pos_ref, x1_ref, rinfo_ref, g2_ref, ys_ref, out_ref, buf, sem):
    tm, d = x1_ref.shape
    base = pl.program_id(0) * (TOP_K * tm)

    def issue(blk, carry):
        for u in range(DMA_UNROLL):
            t = blk * DMA_UNROLL + u
            for k in range(TOP_K):
                _row_copy(ys_ref, pos_ref[base + TOP_K * t + k], buf.at[k], t, sem).start(priority=k)
        return carry

    lax.fori_loop(0, tm // DMA_UNROLL, issue, 0)
    for k in range(TOP_K):
        pltpu.make_async_copy(ys_ref.at[pl.ds(0, tm * SUBLANES)], buf.at[k], sem).wait()
    rinfo = rinfo_ref[...]
    w0 = rinfo[:, 4:5]
    w1 = rinfo[:, 5:6]
    g2 = g2_ref[0]
    for c in range(d // LANES):
        cs = slice(c * LANES, (c + 1) * LANES)
        ffn = (w0 * buf[0, pl.ds(c, tm, stride=SUBLANES), :]
               + w1 * buf[1, pl.ds(c, tm, stride=SUBLANES), :])
        out_ref[:, cs] = x1_ref[:, cs] + g2[:, cs] * ffn


def _combine(pos_flat, x1, rinfo, g2, ys, seq, tm=512):
    t, d = x1.shape
    tpb = seq // tm
    return pl.pallas_call(
        _combine_kernel,
        out_shape=jax.ShapeDtypeStruct((t, d), F32),
        grid_spec=pltpu.PrefetchScalarGridSpec(
            num_scalar_prefetch=1,
            grid=(t // tm,),
            in_specs=[pl.BlockSpec((tm, d), lambda i, pos: (i, 0)),
                      pl.BlockSpec((tm, LANES), lambda i, pos: (i, 0)),
                      pl.BlockSpec((1, 1, d), lambda i, pos: (i // tpb, 0, 0)),
                      pl.BlockSpec(memory_space=pl.ANY)],
            out_specs=pl.BlockSpec((tm, d), lambda i, pos: (i, 0)),
            scratch_shapes=[pltpu.VMEM((TOP_K, tm * SUBLANES, LANES), F32), pltpu.SemaphoreType.DMA],
        ),
        compiler_params=_cparams(("arbitrary",)),
        name="moe_combine",
    )(pos_flat, x1, rinfo, g2, ys)


def _head_slab_columns():
    perm = np.zeros((N_Q_HEADS,), np.int32)
    for gp in range(N_KV_HEADS // 2):
        for qi in range(Q_PER_KV):
            for a in range(2):
                perm[(gp * Q_PER_KV + qi) * 2 + a] = (2 * gp + a) * Q_PER_KV + qi
    return (perm[:, None] * HEAD_DIM + np.arange(HEAD_DIM)[None, :]).reshape(-1)


def _attn_layer(x2, mod_i, gain_i, bias, w_qkv_all, w_o_all, w_gu_all, w_down_all, li,
                q_gain, k_gain, sinks, batch, seq):
    d = D_MODEL
    sh1, sc1, g1, sh2, sc2, g2 = [mod_i[:, k * d:(k + 1) * d] for k in range(6)]
    qg2 = jnp.tile(q_gain.astype(F32), 2)[None]
    kg2 = jnp.tile(k_gain.astype(F32), 2)[None]
    q, k, v = _qkv_proj(x2, jnp.stack([sh1, sc1], axis=1), gain_i[0:1], w_qkv_all, li, qg2, kg2, seq)
    o = _attention(q, k, v, bias, sinks, batch, seq)
    mv = jnp.stack([g1, sh2, sc2, g2], axis=1)
    return _ffn_dense(x2, o, w_o_all, mv, gain_i[1:2], w_gu_all, w_down_all, li, seq)


def _moe_slots(rinfo, cnt, n_tokens, moe_tm):
    counts = cnt[0, :N_EXPERTS].astype(jnp.int32)
    padded = ((counts + moe_tm - 1) // moe_tm) * moe_tm
    ends = jnp.cumsum(padded)
    base = ends - padded
    eidx = rinfo[:, 0:TOP_K].astype(jnp.int32)
    rank = rinfo[:, 2:2 + TOP_K].astype(jnp.int32)
    pos = jnp.sum(jnp.where(eidx[..., None] == jnp.arange(N_EXPERTS), base, 0), axis=-1) + rank
    n_tiles = (TOP_K * n_tokens) // moe_tm + N_EXPERTS
    tile_start = jnp.arange(n_tiles, dtype=jnp.int32) * moe_tm
    tile_expert = jnp.minimum(jnp.sum(tile_start[:, None] >= ends[None, :], axis=-1),
                              N_EXPERTS - 1).astype(jnp.int32)
    n_used = ends[-1] // moe_tm
    tile_expert = jnp.where(tile_start < ends[-1], tile_expert, tile_expert[n_used - 1])
    return pos.reshape(-1), jnp.concatenate([tile_expert, n_used[None].astype(jnp.int32)])


def _ssm_layer(x2, mod_i, gain_i, w_z_all, w_xbc_all, w_dt_all, w_out_all, w_gu_all, w_down_all, li,
               conv_w, conv_b, dt_bias, a_log, d_skip, norm_w, w_router, sorted_init,
               batch, seq, moe_tm):
    d = D_MODEL
    sh1, sc1, g1, sh2, sc2, g2 = [mod_i[:, k * d:(k + 1) * d] for k in range(6)]
    z, xbc, dt = _ssm_in_proj(x2, jnp.stack([sh1, sc1], axis=1), gain_i[0:1], w_z_all, w_xbc_all,
                              w_dt_all, li, conv_w, conv_b, seq)
    y = _ssd(z, xbc, dt, dt_bias, a_log, d_skip, norm_w, batch, seq)

    wr = jnp.pad(w_router.astype(F32), ((0, 0), (0, LANES - N_EXPERTS)))
    wr_hi = wr.astype(BF16)
    wr_lo = (wr - wr_hi.astype(F32)).astype(BF16)
    x1, hf, rinfo, cnt = _router(x2, y, w_out_all, li, jnp.stack([g1, sh2, sc2], axis=1),
                                 gain_i[1:2], jnp.concatenate([wr_hi, wr_lo], axis=1), seq)
    pos_flat, tile_expert = _moe_slots(rinfo, cnt, x2.shape[0], moe_tm)
    xs = _dispatch(pos_flat, hf, sorted_init)
    ys = _ffn_moe(xs, tile_expert, w_gu_all, w_down_all, li, moe_tm)
    return _combine(pos_flat, x1, rinfo, g2[:, None, :], ys, seq), xs


def kernel(x, c, ada_w, ada_b, norm_gain, rel_bias, attn_w_qkv, attn_q_gain, attn_k_gain, attn_sinks, attn_w_o, ssm_w_in, ssm_conv_w, ssm_conv_b, ssm_dt_bias, ssm_a_log, ssm_d, ssm_norm_w, ssm_w_out, ffn_w_gu, ffn_w_down, moe_w_router, moe_w_gu, moe_w_down):
    batch, seq, d = x.shape
    moe_tm = 1024
    mod = _adaln_mod(c.astype(F32), ada_w.astype(F32), ada_b.astype(F32))
    bias = _bias_table(rel_bias)

    col = _head_slab_columns()
    nq = N_Q_HEADS * HEAD_DIM
    n_zx = 2 * D_INNER + D_BC
    w_qkv_all = jnp.concatenate([attn_w_qkv[:, :, :nq][:, :, col], attn_w_qkv[:, :, nq:]],
                                axis=2).astype(BF16)
    w_o_all = attn_w_o[:, col, :].astype(BF16)
    ffn_gu_all = ffn_w_gu.astype(BF16)
    ffn_down_all = ffn_w_down.astype(BF16)
    w_z_all = ssm_w_in[:, :, :D_INNER].astype(BF16)
    w_xbc_all = ssm_w_in[:, :, D_INNER:n_zx].astype(BF16)
    w_dt_all = jnp.pad(ssm_w_in[:, :, n_zx:], ((0, 0), (0, 0), (0, LANES - N_SSM_HEADS))).astype(BF16)
    w_out_all = ssm_w_out.astype(BF16)
    moe_gu_all = moe_w_gu.astype(BF16)
    moe_down_all = moe_w_down.astype(BF16)

    n_sorted = TOP_K * batch * seq + N_EXPERTS * moe_tm
    sorted_buf = jnp.zeros((n_sorted * SUBLANES, LANES), F32)

    x2 = x.reshape(batch * seq, d).astype(F32)
    for i in range(DEPTH):
        j = i // 2
        gain_i = norm_gain[i].astype(F32)
        if i % 2 == 0:
            x2 = _attn_layer(x2, mod[i], gain_i, bias, w_qkv_all, w_o_all, ffn_gu_all, ffn_down_all,
                             j, attn_q_gain[j], attn_k_gain[j], attn_sinks[j], batch, seq)
        else:
            x2, sorted_buf = _ssm_layer(x2, mod[i], gain_i, w_z_all, w_xbc_all, w_dt_all, w_out_all, moe_gu_all,
                                        moe_down_all, j, ssm_conv_w[j], ssm_conv_b[j], ssm_dt_bias[j],
                                        ssm_a_log[j], ssm_d[j], ssm_norm_w[j], moe_w_router[j],
                                        sorted_buf, batch, seq, moe_tm)
    return x2.reshape(batch, seq, d).astype(x.dtype)
```

```python
import functools
import math

import numpy as np
import jax
import jax.numpy as jnp
from jax import lax
from jax.experimental import pallas as pl
from jax.experimental.pallas import tpu as pltpu

F32 = jnp.float32
BF16 = jnp.bfloat16

D_MODEL = 1024
DEPTH = 4
N_Q_HEADS = 16
N_KV_HEADS = 4
HEAD_DIM = 64
Q_PER_KV = 4
ATTN_BLOCK = 128
N_BUCKETS = 32
MAX_DISTANCE = 128
D_INNER = 2048
N_SSM_HEADS = 32
N_SSM_GROUPS = 4
D_STATE = 128
CONV_WIDTH = 4
SSM_CHUNK = 128
D_BC = 2 * N_SSM_GROUPS * D_STATE
D_FF = 3584
N_EXPERTS = 8
TOP_K = 2
NORM_EPS = 1e-6

LANES = 128
SUBLANES = 8
VMEM_LIMIT = 56 * 1024 * 1024
NEG_INF = float("-inf")


def _cparams(sem):
    return pltpu.CompilerParams(dimension_semantics=sem, vmem_limit_bytes=VMEM_LIMIT)


def _sigmoid(v):
    return 1.0 / (1.0 + jnp.exp(-v))


def _rms_mod(x, gain, scale, shift):
    ms = jnp.mean(x * x, axis=-1, keepdims=True)
    y = x * lax.rsqrt(ms + NORM_EPS)
    return (y * gain) * (1.0 + scale) + shift


def _split2(v):
    hi = v.astype(BF16)
    lo = (v - hi.astype(F32)).astype(BF16)
    return hi, lo


def _split3(v):
    hi = v.astype(BF16)
    r = v - hi.astype(F32)
    mid = r.astype(BF16)
    lo = (r - mid.astype(F32)).astype(BF16)
    return hi, mid, lo


def _dot(a, b):
    return jnp.dot(a, b, preferred_element_type=F32)


def _store_tile_rows(ref, row0, val):
    n = val.shape[0]
    for c in range(val.shape[1] // LANES):
        ref[pl.ds(row0 * SUBLANES + c, n, stride=SUBLANES), :] = val[:, c * LANES:(c + 1) * LANES]


def _load_tile_rows(ref, n):
    return [ref[pl.ds(c, n, stride=SUBLANES), :] for c in range(SUBLANES)]


def _mod_kernel(c_ref, w_ref, b_ref, o_ref):
    c = c_ref[...]
    cond = c * _sigmoid(c)
    o_ref[0] = _dot(cond.astype(BF16), w_ref[0].astype(BF16)) + b_ref[0]


def _adaln_mod(c, ada_w, ada_b):
    depth, d, n = ada_w.shape
    b = c.shape[0]
    tn = 1024
    c_pad = jnp.zeros((SUBLANES, d), F32).at[:b].set(c)
    out = pl.pallas_call(
        _mod_kernel,
        out_shape=jax.ShapeDtypeStruct((depth, SUBLANES, n), F32),
        grid=(depth, n // tn),
        in_specs=[
            pl.BlockSpec((SUBLANES, d), lambda i, j: (0, 0)),
            pl.BlockSpec((1, d, tn), lambda i, j: (i, 0, j)),
            pl.BlockSpec((1, 1, tn), lambda i, j: (i, 0, j)),
        ],
        out_specs=pl.BlockSpec((1, SUBLANES, tn), lambda i, j: (i, 0, j)),
        compiler_params=_cparams(("arbitrary", "arbitrary")),
        name="adaln_mod",
    )(c_pad, ada_w, ada_b.reshape(depth, 1, n))
    return out[:, :b]


def _t5_bucket_table():
    qi = np.arange(ATTN_BLOCK)[:, None]
    kj = np.arange(2 * ATTN_BLOCK)[None, :]
    dist = qi + ATTN_BLOCK - kj
    max_exact = N_BUCKETS // 2
    d = np.maximum(dist, 0)
    df = np.maximum(d, 1).astype(np.float64)
    large = max_exact + (np.log(df / max_exact) / math.log(MAX_DISTANCE / max_exact)
                         * (N_BUCKETS - max_exact)).astype(np.int32)
    large = np.minimum(large, N_BUCKETS - 1)
    bucket = np.where(d < max_exact, d, large)
    valid = (dist >= 0) & (dist < ATTN_BLOCK)
    return np.where(valid, bucket, -1).astype(np.int32)


def _bias_kernel(rb_ref, bkt_ref, o_ref):
    j = pl.program_id(0)
    gp = j // Q_PER_KV
    qi = j % Q_PER_KV
    bkt = bkt_ref[...]
    for a in range(2):
        h = (2 * gp + a) * Q_PER_KV + qi
        acc = jnp.full(bkt.shape, NEG_INF, F32)
        for b in range(N_BUCKETS):
            acc = jnp.where(bkt == b, rb_ref[b, h], acc)
        o_ref[0, a * ATTN_BLOCK:(a + 1) * ATTN_BLOCK, :] = acc


def _bias_table(rel_bias):
    bkt = jnp.asarray(_t5_bucket_table())
    n_slots = N_Q_HEADS // 2
    return pl.pallas_call(
        _bias_kernel,
        out_shape=jax.ShapeDtypeStruct((n_slots, 2 * ATTN_BLOCK, 2 * ATTN_BLOCK), F32),
        grid=(n_slots,),
        in_specs=[
            pl.BlockSpec(memory_space=pltpu.SMEM),
            pl.BlockSpec((ATTN_BLOCK, 2 * ATTN_BLOCK), lambda j: (0, 0)),
        ],
        out_specs=pl.BlockSpec((1, 2 * ATTN_BLOCK, 2 * ATTN_BLOCK), lambda j: (j, 0, 0)),
        compiler_params=_cparams(("arbitrary",)),
        name="t5_bias_table",
    )(rel_bias.astype(F32), bkt)


def _qkv_kernel(x_ref, ms_ref, gain_ref, w_ref, qg_ref, kg_ref, q_ref, k_ref, v_ref):
    tm = x_ref.shape[0]
    ms = ms_ref[0]
    hm = _rms_mod(x_ref[...], gain_ref[...], ms[1:2], ms[0:1]).astype(BF16)
    lo = lax.broadcasted_iota(jnp.int32, (tm, LANES), 1) < HEAD_DIM

    def head_norm(slab, g):
        sq = slab * slab
        s_lo = jnp.sum(jnp.where(lo, sq, 0.0), axis=-1, keepdims=True)
        s_hi = jnp.sum(jnp.where(lo, 0.0, sq), axis=-1, keepdims=True)
        msq = jnp.where(lo, s_lo, s_hi) * (1.0 / HEAD_DIM)
        return slab * lax.rsqrt(msq + NORM_EPS) * g

    nq = N_Q_HEADS * HEAD_DIM
    nk = N_KV_HEADS * HEAD_DIM
    q_all = _dot(hm, w_ref[:, :nq])
    for j in range(nq // LANES):
        sl = slice(j * LANES, (j + 1) * LANES)
        q_ref[:, sl] = (head_norm(q_all[:, sl], qg_ref[...]) * (HEAD_DIM ** -0.5)).astype(BF16)
    k_all = _dot(hm, w_ref[:, nq:nq + nk])
    for j in range(nk // LANES):
        sl = slice(j * LANES, (j + 1) * LANES)
        k_ref[:, sl] = head_norm(k_all[:, sl], kg_ref[...]).astype(BF16)
    v_ref[...] = _dot(hm, w_ref[:, nq + nk:]).astype(BF16)


def _qkv_proj(x2, ms, gain, w_qkv_all, li, qg2, kg2, seq, tm=512):
    t, d = x2.shape
    nq = N_Q_HEADS * HEAD_DIM
    nk = N_KV_HEADS * HEAD_DIM
    tpb = seq // tm
    return pl.pallas_call(
        _qkv_kernel,
        out_shape=(jax.ShapeDtypeStruct((t, nq), BF16),
                   jax.ShapeDtypeStruct((t, nk), BF16),
                   jax.ShapeDtypeStruct((t, nk), BF16)),
        grid=(t // tm,),
        in_specs=[
            pl.BlockSpec((tm, d), lambda i: (i, 0)),
            pl.BlockSpec((1, 2, d), lambda i: (i // tpb, 0, 0)),
            pl.BlockSpec((1, d), lambda i: (0, 0)),
            pl.BlockSpec((None, d, nq + 2 * nk), lambda i: (li, 0, 0)),
            pl.BlockSpec((1, LANES), lambda i: (0, 0)),
            pl.BlockSpec((1, LANES), lambda i: (0, 0)),
        ],
        out_specs=(pl.BlockSpec((tm, nq), lambda i: (i, 0)),
                   pl.BlockSpec((tm, nk), lambda i: (i, 0)),
                   pl.BlockSpec((tm, nk), lambda i: (i, 0))),
        compiler_params=_cparams(("arbitrary",)),
        name="attn_qkv",
    )(x2, ms, gain, w_qkv_all, qg2, kg2)


def _attn_kernel(sink_ref, q_ref, kc_ref, kp_ref, vc_ref, vp_ref, bias_ref, o_ref):
    blk = ATTN_BLOCK
    first = pl.program_id(1) == 0
    lane = lax.broadcasted_iota(jnp.int32, (blk, LANES), 1)
    lo = lane < HEAD_DIM
    mask_lo = jnp.where(lo, 1.0, 0.0).astype(BF16)
    mask_hi = jnp.where(lo, 0.0, 1.0).astype(BF16)
    row = lax.broadcasted_iota(jnp.int32, (2 * blk, 1), 0)
    nt = (((1,), (1,)), ((), ()))
    for gp in range(N_KV_HEADS // 2):
        ksl = slice(gp * LANES, (gp + 1) * LANES)
        kc = kc_ref[:, ksl]
        kp = kp_ref[:, ksl]
        vc = vc_ref[:, ksl]
        vp = vp_ref[:, ksl]
        for qi in range(Q_PER_KV):
            j = gp * Q_PER_KV + qi
            sl = slice(j * LANES, (j + 1) * LANES)
            qs = q_ref[:, sl]
            q2 = jnp.concatenate([qs * mask_lo, qs * mask_hi], axis=0)
            s_p = lax.dot_general(q2, kp, nt, preferred_element_type=F32)
            s_c = lax.dot_general(q2, kc, nt, preferred_element_type=F32)
            bias = bias_ref[j]
            s_p = jnp.where(first, NEG_INF, s_p + bias[:, :blk])
            s_c = s_c + bias[:, blk:]
            h_a = (2 * gp) * Q_PER_KV + qi
            h_b = (2 * gp + 1) * Q_PER_KV + qi
            sink = jnp.where(row < blk, sink_ref[h_a], sink_ref[h_b])
            m = jnp.maximum(jnp.max(jnp.maximum(s_p, s_c), axis=-1, keepdims=True), sink)
            p_p = jnp.exp(s_p - m)
            p_c = jnp.exp(s_c - m)
            den = jnp.sum(p_p + p_c, axis=-1, keepdims=True) + jnp.exp(sink - m)
            pv = _dot(p_p.astype(BF16), vp) + _dot(p_c.astype(BF16), vc)
            pv = pv / den
            o_ref[:, sl] = jnp.where(lo, pv[:blk], pv[blk:]).astype(BF16)


def _attention(q, k, v, bias, sinks, batch, seq):
    t, nq = q.shape
    nk = k.shape[1]
    blk = ATTN_BLOCK
    nb = seq // blk
    cur = lambda b, n, *_: (b * nb + n, 0)
    prev = lambda b, n, *_: (b * nb + jnp.maximum(n - 1, 0), 0)
    return pl.pallas_call(
        _attn_kernel,
        out_shape=jax.ShapeDtypeStruct((t, nq), BF16),
        grid=(batch, nb),
        in_specs=[
            pl.BlockSpec(memory_space=pltpu.SMEM),
            pl.BlockSpec((blk, nq), cur),
            pl.BlockSpec((blk, nk), cur),
            pl.BlockSpec((blk, nk), prev),
            pl.BlockSpec((blk, nk), cur),
            pl.BlockSpec((blk, nk), prev),
            pl.BlockSpec(bias.shape, lambda b, n: (0, 0, 0)),
        ],
        out_specs=pl.BlockSpec((blk, nq), cur),
        compiler_params=_cparams(("arbitrary", "arbitrary")),
        name="swa_attention",
    )(sinks.astype(F32), q, k, k, v, v, bias)


def _swiglu_step(hf, wg_ref, wu_ref, wd_ref, acc_s):
    g = _dot(hf, wg_ref[...])
    u = _dot(hf, wu_ref[...])
    a = (g * _sigmoid(g) * u).astype(BF16)
    acc_s[...] += _dot(a, wd_ref[...])


def _ffn_dense_kernel(x_ref, o_ref, wo_ref, mv_ref, gain_ref, wg_ref, wu_ref, wd_ref,
                      out_ref, hf_s, acc_s):
    j = pl.program_id(1)

    @pl.when(j == 0)
    def _():
        mv = mv_ref[0]
        x1 = x_ref[...] + mv[0:1] * _dot(o_ref[...], wo_ref[...])
        out_ref[...] = x1
        hf_s[...] = _rms_mod(x1, gain_ref[...], mv[2:3], mv[1:2]).astype(BF16)
        acc_s[...] = jnp.zeros_like(acc_s)

    _swiglu_step(hf_s[...], wg_ref, wu_ref, wd_ref, acc_s)

    @pl.when(j == pl.num_programs(1) - 1)
    def _():
        out_ref[...] = out_ref[...] + mv_ref[0][3:4] * acc_s[...]


def _ffn_dense(x2, o, wo_all, mv, gain, w_gu_all, w_down_all, li, seq, tm=1024, tf=512):
    t, d = x2.shape
    f = w_down_all.shape[1]
    nj = f // tf
    tpb = seq // tm
    return pl.pallas_call(
        _ffn_dense_kernel,
        out_shape=jax.ShapeDtypeStruct((t, d), F32),
        grid=(t // tm, nj),
        in_specs=[
            pl.BlockSpec((tm, d), lambda i, j: (i, 0)),
            pl.BlockSpec((tm, o.shape[1]), lambda i, j: (i, 0)),
            pl.BlockSpec((None,) + wo_all.shape[1:], lambda i, j: (li, 0, 0)),
            pl.BlockSpec((1, 4, d), lambda i, j: (i // tpb, 0, 0)),
            pl.BlockSpec((1, d), lambda i, j: (0, 0)),
            pl.BlockSpec((None, d, tf), lambda i, j: (li, 0, j)),
            pl.BlockSpec((None, d, tf), lambda i, j: (li, 0, nj + j)),
            pl.BlockSpec((None, tf, d), lambda i, j: (li, j, 0)),
        ],
        out_specs=pl.BlockSpec((tm, d), lambda i, j: (i, 0)),
        scratch_shapes=[pltpu.VMEM((tm, d), BF16), pltpu.VMEM((tm, d), F32)],
        compiler_params=_cparams(("arbitrary", "arbitrary")),
        name="wo_ffn_dense",
    )(x2, o, wo_all, mv, gain, w_gu_all, w_gu_all, w_down_all)


def _ffn_moe_kernel(te_ref, xs_ref, wg_ref, wu_ref, wd_ref, out_ref, hf_s, acc_s):
    i = pl.program_id(0)
    j = pl.program_id(1)
    used = i < te_ref[pl.num_programs(0)]

    tm = hf_s.shape[0]

    @pl.when(used & (j == 0))
    def _():
        for c, part in enumerate(_load_tile_rows(xs_ref, tm)):
            hf_s[:, c * LANES:(c + 1) * LANES] = part.astype(BF16)
        acc_s[...] = jnp.zeros_like(acc_s)

    @pl.when(used)
    def _():
        _swiglu_step(hf_s[...], wg_ref, wu_ref, wd_ref, acc_s)

    @pl.when(j == pl.num_programs(1) - 1)
    def _():
        @pl.when(used)
        def _():
            _store_tile_rows(out_ref, 0, acc_s[...])

        @pl.when(jnp.logical_not(used))
        def _():
            out_ref[...] = jnp.zeros_like(out_ref)


def _ffn_moe(xs, tile_expert, w_gu_all, w_down_all, li, tm, tf=896):
    r = xs.shape[0] // SUBLANES
    d = w_gu_all.shape[2]
    f = w_down_all.shape[2]
    nj = f // tf
    nt = r // tm

    def jj(i, j, te):
        return jnp.where(i < te[nt], j, 0)

    return pl.pallas_call(
        _ffn_moe_kernel,
        out_shape=jax.ShapeDtypeStruct((r * SUBLANES, LANES), F32),
        grid_spec=pltpu.PrefetchScalarGridSpec(
            num_scalar_prefetch=1,
            grid=(nt, nj),
            in_specs=[
                pl.BlockSpec((tm * SUBLANES, LANES), lambda i, j, te: (i, 0)),
                pl.BlockSpec((None, None, d, tf), lambda i, j, te: (li, te[i], 0, jj(i, j, te))),
                pl.BlockSpec((None, None, d, tf), lambda i, j, te: (li, te[i], 0, nj + jj(i, j, te))),
                pl.BlockSpec((None, None, tf, d), lambda i, j, te: (li, te[i], jj(i, j, te), 0)),
            ],
            out_specs=pl.BlockSpec((tm * SUBLANES, LANES), lambda i, j, te: (i, 0)),
            scratch_shapes=[pltpu.VMEM((tm, d), BF16), pltpu.VMEM((tm, d), F32)],
        ),
        compiler_params=_cparams(("arbitrary", "arbitrary")),
        name="moe_expert_ffn",
    )(tile_expert, xs, w_gu_all, w_gu_all, w_down_all)


def _inproj_kernel(x_ref, ms_ref, gain_ref, wz_ref, wx_ref, wdt_ref, cw_ref, cb_ref,
                   z_ref, xbc_ref, dt_ref, hm_s, carry_s, *, tiles_per_seq):
    i = pl.program_id(0)
    j = pl.program_id(1)
    tm = z_ref.shape[0]

    @pl.when((i == 0) & (j == 0))
    def _():
        carry_s[...] = jnp.zeros_like(carry_s)

    @pl.when(j == 0)
    def _():
        ms = ms_ref[0]
        hm = _rms_mod(x_ref[...], gain_ref[...], ms[1:2], ms[0:1]).astype(BF16)
        hm_s[...] = hm
        dt_ref[...] = _dot(hm, wdt_ref[...])

    hm = hm_s[...]
    strip = 2 * LANES
    rowi = lax.broadcasted_iota(jnp.int32, (SUBLANES, strip), 0)
    first = i % tiles_per_seq == 0
    n_z_strips = z_ref.shape[1] // strip
    for k, c0 in enumerate(range(0, xbc_ref.shape[1], strip)):
        cs = slice(c0, c0 + strip)
        res = _dot(hm, wx_ref[:, cs])
        if k < n_z_strips:
            zs = slice(k * strip, (k + 1) * strip)
            z_ref[:, zs] = _dot(hm, wz_ref[:, zs]).astype(BF16)
        history = jnp.where(first, 0.0, carry_s[j, :, cs])
        acc = res * cw_ref[CONV_WIDTH - 1:CONV_WIDTH, cs] + cb_ref[:, cs]
        for s in range(1, CONV_WIDTH):
            down = pltpu.roll(res, s, 0)
            head = jnp.where(rowi < s, pltpu.roll(history, s, 0), down[:SUBLANES])
            shifted = jnp.concatenate([head, down[SUBLANES:]], axis=0)
            acc = acc + shifted * cw_ref[CONV_WIDTH - 1 - s:CONV_WIDTH - s, cs]
        carry_s[j, :, cs] = res[tm - SUBLANES:]
        xbc_ref[:, cs] = (acc * _sigmoid(acc)).astype(BF16)


def _ssm_in_proj(x2, ms, gain, w_z_all, w_xbc_all, w_dt_all, li, conv_w, conv_b, seq, tm=1024, nj=2):
    t, d = x2.shape
    nz = w_z_all.shape[2] // nj
    nx = w_xbc_all.shape[2] // nj
    tpb = seq // tm
    return pl.pallas_call(
        functools.partial(_inproj_kernel, tiles_per_seq=tpb),
        out_shape=(jax.ShapeDtypeStruct((t, nz * nj), BF16), jax.ShapeDtypeStruct((t, nx * nj), BF16),
                   jax.ShapeDtypeStruct((t, LANES), F32)),
        grid=(t // tm, nj),
        in_specs=[
            pl.BlockSpec((tm, d), lambda i, j: (i, 0)),
            pl.BlockSpec((1, 2, d), lambda i, j: (i // tpb, 0, 0)),
            pl.BlockSpec((1, d), lambda i, j: (0, 0)),
            pl.BlockSpec((None, d, nz), lambda i, j: (li, 0, j)),
            pl.BlockSpec((None, d, nx), lambda i, j: (li, 0, j)),
            pl.BlockSpec((None, d, LANES), lambda i, j: (li, 0, 0)),
            pl.BlockSpec((CONV_WIDTH, nx), lambda i, j: (0, j)),
            pl.BlockSpec((1, nx), lambda i, j: (0, j)),
        ],
        out_specs=(pl.BlockSpec((tm, nz), lambda i, j: (i, j)),
                   pl.BlockSpec((tm, nx), lambda i, j: (i, j)),
                   pl.BlockSpec((tm, LANES), lambda i, j: (i, 0))),
        scratch_shapes=[pltpu.VMEM((tm, d), BF16), pltpu.VMEM((nj, SUBLANES, nx), F32)],
        compiler_params=_cparams(("arbitrary", "arbitrary")),
        name="ssm_in_proj",
    )(x2, ms, gain, w_z_all, w_xbc_all, w_dt_all, conv_w.astype(F32), conv_b.astype(F32)[None])


def _ssd_kernel(z_ref, xs_ref, bc_ref, dt_ref, dtb_ref, alog_ref, dskip_ref, nw_ref, r64_ref,
                y_ref, state_s, ybuf_s):
    L = SSM_CHUNK
    gw = D_INNER // N_SSM_GROUPS
    hpg = N_SSM_HEADS // N_SSM_GROUPS

    @pl.when(pl.program_id(1) == 0)
    def _():
        state_s[...] = jnp.zeros_like(state_s)

    dtv = jax.nn.softplus(dt_ref[...] + dtb_ref[...])
    a = -jnp.exp(alog_ref[...])
    dta = dtv * a
    ri = lax.broadcasted_iota(jnp.int32, (L, L), 0)
    ci = lax.broadcasted_iota(jnp.int32, (L, L), 1)
    causal = ri >= ci
    tri = jnp.where(causal, 1.0, 0.0).astype(BF16)
    p0, p1, p2 = _split3(dta)
    acs = _dot(tri, p0) + _dot(tri, p1) + _dot(tri, p2)
    acs_t = acs.T
    eacs = jnp.exp(acs)
    dte = jnp.exp(acs[L - 1:L, :] - acs)
    stacked = jnp.concatenate([dtv, eacs, dte], axis=0)
    s_hi, s_lo = _split2(stacked)
    expd = _dot(s_hi, r64_ref[...]) + _dot(s_lo, r64_ref[...])

    lo = lax.broadcasted_iota(jnp.int32, (L, LANES), 1) < (LANES // 2)
    for g in range(N_SSM_GROUPS):
        gs = slice(g * gw, (g + 1) * gw)
        b_g = bc_ref[:, g * D_STATE:(g + 1) * D_STATE]
        c_b = bc_ref[:, (N_SSM_GROUPS + g) * D_STATE:(N_SSM_GROUPS + g + 1) * D_STATE]
        b_t = b_g.astype(F32).T.astype(BF16)
        cb = _dot(c_b, b_t)
        xs_g = xs_ref[:, gs].astype(F32)
        xd = xs_g * expd[0:L, gs]
        xd_b = xd.astype(BF16)
        st_prev = state_s[g]
        y_off = _dot(c_b, st_prev.astype(BF16)) * expd[L:2 * L, gs]
        xdw = (xd * expd[2 * L:3 * L, gs]).astype(BF16)
        state_s[g] = expd[2 * L - 1:2 * L, gs] * st_prev + _dot(b_t, xdw)
        for jp in range(hpg // 2):
            ps = slice(jp * LANES, (jp + 1) * LANES)
            slab = xd_b[:, ps]
            halves = []
            for half in range(2):
                hh = g * hpg + 2 * jp + half
                diff = acs[:, hh:hh + 1] - acs_t[hh:hh + 1, :]
                dec = jnp.exp(jnp.where(causal, diff, NEG_INF))
                halves.append(_dot((cb * dec).astype(BF16), slab))
            y = jnp.where(lo, halves[0], halves[1]) + y_off[:, ps]
            cols = slice(g * gw + jp * LANES, g * gw + (jp + 1) * LANES)
            ybuf_s[:, cols] = y + dskip_ref[:, cols] * xs_g[:, ps]

    z = z_ref[...].astype(F32)
    yg = ybuf_s[...] * (z * _sigmoid(z))
    ms = jnp.mean(yg * yg, axis=-1, keepdims=True)
    y_ref[...] = (yg * lax.rsqrt(ms + NORM_EPS) * nw_ref[...]).astype(BF16)


def _ssd(z, xbc, dt, dt_bias, a_log, d_skip, norm_w, batch, seq):
    t = z.shape[0]
    L = SSM_CHUNK
    nc = seq // L
    di, dbc = D_INNER, D_BC
    pad = LANES - N_SSM_HEADS
    dtb = jnp.pad(dt_bias.astype(F32), (0, pad))[None]
    alog = jnp.pad(a_log.astype(F32), (0, pad))[None]
    dskip = jnp.repeat(d_skip.astype(F32), di // N_SSM_HEADS)[None]
    nw = norm_w.astype(F32)[None]
    r64 = jnp.asarray((np.arange(LANES)[:, None] == (np.arange(di)[None, :] // (di // N_SSM_HEADS)))
                      .astype(np.float32)).astype(BF16)
    row = lambda b, c: (b * nc + c, 0)
    const = lambda b, c: (0, 0)
    return pl.pallas_call(
        _ssd_kernel,
        out_shape=jax.ShapeDtypeStruct((t, di), BF16),
        grid=(batch, nc),
        in_specs=[
            pl.BlockSpec((L, di), row),
            pl.BlockSpec((L, di), row),
            pl.BlockSpec((L, dbc), lambda b, c: (b * nc + c, di // dbc)),
            pl.BlockSpec((L, LANES), row),
            pl.BlockSpec((1, LANES), const),
            pl.BlockSpec((1, LANES), const),
            pl.BlockSpec((1, di), const),
            pl.BlockSpec((1, di), const),
            pl.BlockSpec((LANES, di), const),
        ],
        out_specs=pl.BlockSpec((L, di), row),
        scratch_shapes=[
            pltpu.VMEM((N_SSM_GROUPS, D_STATE, di // N_SSM_GROUPS), F32),
            pltpu.VMEM((L, di), F32),
        ],
        compiler_params=_cparams(("arbitrary", "arbitrary")),
        name="ssd_scan",
    )(z, xbc, xbc, dt, dtb, alog, dskip, nw, r64)


ROUTER_SUB = 512


def _router_kernel(x_ref, y_ref, wout_ref, mv_ref, gain_ref, wr_ref,
                   x1_ref, hf_ref, rinfo_ref, cnt_ref, carry_s):
    tm = x_ref.shape[0]
    sub = ROUTER_SUB

    @pl.when(pl.program_id(0) == 0)
    def _():
        carry_s[...] = jnp.zeros_like(carry_s)

    mv = mv_ref[0]
    lane = lax.broadcasted_iota(jnp.int32, (sub, LANES), 1)
    lanef = lane.astype(F32)
    ri = lax.broadcasted_iota(jnp.int32, (sub, sub), 0)
    ci = lax.broadcasted_iota(jnp.int32, (sub, sub), 1)
    before = jnp.where(ri > ci, 1.0, 0.0).astype(BF16)
    carry = carry_s[...]
    for h in range(tm // sub):
        rs = slice(h * sub, (h + 1) * sub)
        x1 = x_ref[rs, :] + mv[0:1] * _dot(y_ref[rs, :], wout_ref[...])
        x1_ref[rs, :] = x1
        hf = _rms_mod(x1, gain_ref[...], mv[2:3], mv[1:2])
        _store_tile_rows(hf_ref, h * sub, hf)
        h_hi, h_lo = _split2(hf)
        hw = _dot(h_hi, wr_ref[...])
        logits = hw[:, :LANES] + hw[:, LANES:] + _dot(h_lo, wr_ref[:, :LANES])
        lg = jnp.where(lane < N_EXPERTS, logits, NEG_INF)
        v0 = jnp.max(lg, axis=-1, keepdims=True)
        i0 = jnp.min(jnp.where(lg == v0, lanef, float(LANES)), axis=-1, keepdims=True)
        lg1 = jnp.where(lanef == i0, NEG_INF, lg)
        v1 = jnp.max(lg1, axis=-1, keepdims=True)
        i1 = jnp.min(jnp.where(lg1 == v1, lanef, float(LANES)), axis=-1, keepdims=True)
        e1 = jnp.exp(v1 - v0)
        w0 = 1.0 / (1.0 + e1)
        w1 = e1 / (1.0 + e1)
        oh0 = jnp.where(lanef == i0, 1.0, 0.0)
        oh1 = jnp.where(lanef == i1, 1.0, 0.0)
        pre = _dot(before, jnp.concatenate([oh0, oh1], axis=1).astype(BF16))
        cnt0 = jnp.sum(oh0, axis=0, keepdims=True)
        cnt1 = jnp.sum(oh1, axis=0, keepdims=True)
        rank0 = jnp.sum(oh0 * (pre[:, :LANES] + carry), axis=-1, keepdims=True)
        rank1 = jnp.sum(oh1 * (pre[:, LANES:] + carry + cnt0), axis=-1, keepdims=True)
        carry = carry + cnt0 + cnt1
        rinfo_ref[rs, :] = jnp.where(lane == 0, i0, jnp.where(lane == 1, i1, jnp.where(
            lane == 2, rank0, jnp.where(lane == 3, rank1, jnp.where(
                lane == 4, w0, jnp.where(lane == 5, w1, 0.0))))))
    carry_s[...] = carry
    cnt_ref[...] = carry


def _router(x2, y, wout_all, li, mv, gain, wr, seq, tm=1024):
    t, d = x2.shape
    tpb = seq // tm
    return pl.pallas_call(
        _router_kernel,
        out_shape=(jax.ShapeDtypeStruct((t, d), F32),
                   jax.ShapeDtypeStruct((t * SUBLANES, LANES), F32),
                   jax.ShapeDtypeStruct((t, LANES), F32), jax.ShapeDtypeStruct((1, LANES), F32)),
        grid=(t // tm,),
        in_specs=[
            pl.BlockSpec((tm, d), lambda i: (i, 0)),
            pl.BlockSpec((tm, y.shape[1]), lambda i: (i, 0)),
            pl.BlockSpec((None,) + wout_all.shape[1:], lambda i: (li, 0, 0)),
            pl.BlockSpec((1, 3, d), lambda i: (i // tpb, 0, 0)),
            pl.BlockSpec((1, d), lambda i: (0, 0)),
            pl.BlockSpec((d, 2 * LANES), lambda i: (0, 0)),
        ],
        out_specs=(pl.BlockSpec((tm, d), lambda i: (i, 0)),
                   pl.BlockSpec((tm * SUBLANES, LANES), lambda i: (i, 0)),
                   pl.BlockSpec((tm, LANES), lambda i: (i, 0)),
                   pl.BlockSpec((1, LANES), lambda i: (0, 0))),
        scratch_shapes=[pltpu.VMEM((1, LANES), F32)],
        compiler_params=_cparams(("arbitrary",)),
        name="ssm_out_router",
    )(x2, y, wout_all, mv, gain, wr)


DMA_UNROLL = 8


def _row_copy(src, src_row, dst, dst_row, sem):
    return pltpu.make_async_copy(
        src.at[pl.ds(pl.multiple_of(src_row * SUBLANES, SUBLANES), SUBLANES)],
        dst.at[pl.ds(pl.multiple_of(dst_row * SUBLANES, SUBLANES), SUBLANES)], sem)


def _dispatch_kernel(pos_ref, hf_ref, sorted_in_ref, sorted_ref, sem):
    del sorted_in_ref
    tm = hf_ref.shape[0] // SUBLANES
    base = pl.program_id(0) * (TOP_K * tm)

    def issue(blk, carry):
        for u in range(DMA_UNROLL):
            t = blk * DMA_UNROLL + u
            for k in range(TOP_K):
                _row_copy(hf_ref, t, sorted_ref, pos_ref[base + TOP_K * t + k], sem).start(priority=k)
        return carry

    lax.fori_loop(0, tm // DMA_UNROLL, issue, 0)
    for k in range(TOP_K):
        pltpu.make_async_copy(hf_ref, sorted_ref.at[pl.ds(0, tm * SUBLANES)], sem).wait()


def _dispatch(pos_flat, hf, sorted_init, tm=1024):
    t = hf.shape[0] // SUBLANES
    return pl.pallas_call(
        _dispatch_kernel,
        out_shape=jax.ShapeDtypeStruct(sorted_init.shape, hf.dtype),
        grid_spec=pltpu.PrefetchScalarGridSpec(
            num_scalar_prefetch=1,
            grid=(t // tm,),
            in_specs=[pl.BlockSpec((tm * SUBLANES, LANES), lambda i, pos: (i, 0)),
                      pl.BlockSpec(memory_space=pl.ANY)],
            out_specs=pl.BlockSpec(memory_space=pl.ANY),
            scratch_shapes=[pltpu.SemaphoreType.DMA],
        ),
        input_output_aliases={2: 0},
        compiler_params=_cparams(("arbitrary",)),
        name="moe_dispatch",
    )(pos_flat, hf, sorted_init)


def _combine_kernel(pos_ref, x1_ref, rinfo_ref, g2_ref, ys_ref, out_ref, buf, sem):
    tm, d = x1_ref.shape
    i = pl.program_id(0)
    slot = i % 2

    def issue_tile(tile, dst_slot):
        base = tile * (TOP_K * tm)

        def issue(blk, carry):
            for u in range(DMA_UNROLL):
                t = blk * DMA_UNROLL + u
                for k in range(TOP_K):
                    _row_copy(ys_ref, pos_ref[base + TOP_K * t + k], buf.at[dst_slot, k], t,
                              sem.at[dst_slot]).start(priority=k)
            return carry

        lax.fori_loop(0, tm // DMA_UNROLL, issue, 0)

    @pl.when(i == 0)
    def _():
        issue_tile(0, 0)

    @pl.when(i + 1 < pl.num_programs(0))
    def _():
        issue_tile(i + 1, 1 - slot)

    for k in range(TOP_K):
        pltpu.make_async_copy(ys_ref.at[pl.ds(0, tm * SUBLANES)], buf.at[slot, k], sem.at[slot]).wait()
    rinfo = rinfo_ref[...]
    w0 = rinfo[:, 4:5]
    w1 = rinfo[:, 5:6]
    g2 = g2_ref[0]
    for c in range(d // LANES):
        cs = slice(c * LANES, (c + 1) * LANES)
        ffn = (w0 * buf[slot, 0, pl.ds(c, tm, stride=SUBLANES), :]
               + w1 * buf[slot, 1, pl.ds(c, tm, stride=SUBLANES), :])
        out_ref[:, cs] = x1_ref[:, cs] + g2[:, cs] * ffn


def _combine(pos_flat, x1, rinfo, g2, ys, seq, tm=512):
    t, d = x1.shape
    tpb = seq // tm
    return pl.pallas_call(
        _combine_kernel,
        out_shape=jax.ShapeDtypeStruct((t, d), F32),
        grid_spec=pltpu.PrefetchScalarGridSpec(
            num_scalar_prefetch=1,
            grid=(t // tm,),
            in_specs=[pl.BlockSpec((tm, d), lambda i, pos: (i, 0)),
                      pl.BlockSpec((tm, LANES), lambda i, pos: (i, 0)),
                      pl.BlockSpec((1, 1, d), lambda i, pos: (i // tpb, 0, 0)),
                      pl.BlockSpec(memory_space=pl.ANY)],
            out_specs=pl.BlockSpec((tm, d), lambda i, pos: (i, 0)),
            scratch_shapes=[pltpu.VMEM((2, TOP_K, tm * SUBLANES, LANES), F32),
                            pltpu.SemaphoreType.DMA((2,))],
        ),
        compiler_params=_cparams(("arbitrary",)),
        name="moe_combine",
    )(pos_flat, x1, rinfo, g2, ys)


def _attn_weights(attn_w_qkv, attn_w_o):
    n_att, d, _ = attn_w_qkv.shape
    nq = N_Q_HEADS * HEAD_DIM
    head_grid = (N_KV_HEADS // 2, 2, Q_PER_KV, HEAD_DIM)
    w_q = attn_w_qkv[:, :, :nq].reshape((n_att, d) + head_grid)
    w_q = jnp.swapaxes(w_q, 3, 4).reshape(n_att, d, nq)
    w_qkv_all = jnp.concatenate([w_q, attn_w_qkv[:, :, nq:]], axis=2).astype(BF16)
    w_o = attn_w_o.reshape((n_att,) + head_grid + (attn_w_o.shape[2],))
    w_o_all = jnp.swapaxes(w_o, 2, 3).reshape(n_att, nq, attn_w_o.shape[2]).astype(BF16)
    return w_qkv_all, w_o_all


def _attn_layer(x2, mod_i, gain_i, bias, w_qkv_all, w_o_all, w_gu_all, w_down_all, li,
                q_gain, k_gain, sinks, batch, seq):
    d = D_MODEL
    sh1, sc1, g1, sh2, sc2, g2 = [mod_i[:, k * d:(k + 1) * d] for k in range(6)]
    qg2 = jnp.tile(q_gain.astype(F32), 2)[None]
    kg2 = jnp.tile(k_gain.astype(F32), 2)[None]
    q, k, v = _qkv_proj(x2, jnp.stack([sh1, sc1], axis=1), gain_i[0:1], w_qkv_all, li, qg2, kg2, seq)
    o = _attention(q, k, v, bias, sinks, batch, seq)
    mv = jnp.stack([g1, sh2, sc2, g2], axis=1)
    return _ffn_dense(x2, o, w_o_all, mv, gain_i[1:2], w_gu_all, w_down_all, li, seq)


def _moe_slots(rinfo, cnt, n_tokens, moe_tm):
    counts = cnt[0, :N_EXPERTS].astype(jnp.int32)
    padded = ((counts + moe_tm - 1) // moe_tm) * moe_tm
    ends = jnp.cumsum(padded)
    base = ends - padded
    eidx = rinfo[:, 0:TOP_K].astype(jnp.int32)
    rank = rinfo[:, 2:2 + TOP_K].astype(jnp.int32)
    pos = jnp.sum(jnp.where(eidx[..., None] == jnp.arange(N_EXPERTS), base, 0), axis=-1) + rank
    n_tiles = (TOP_K * n_tokens) // moe_tm + N_EXPERTS
    tile_start = jnp.arange(n_tiles, dtype=jnp.int32) * moe_tm
    tile_expert = jnp.minimum(jnp.sum(tile_start[:, None] >= ends[None, :], axis=-1),
                              N_EXPERTS - 1).astype(jnp.int32)
    n_used = ends[-1] // moe_tm
    tile_expert = jnp.where(tile_start < ends[-1], tile_expert, tile_expert[n_used - 1])
    return pos.reshape(-1), jnp.concatenate([tile_expert, n_used[None].astype(jnp.int32)])


def _ssm_layer(x2, mod_i, gain_i, w_z_all, w_xbc_all, w_dt_all, w_out_all, w_gu_all, w_down_all, li,
               conv_w, conv_b, dt_bias, a_log, d_skip, norm_w, w_router, sorted_init,
               batch, seq, moe_tm):
    d = D_MODEL
    sh1, sc1, g1, sh2, sc2, g2 = [mod_i[:, k * d:(k + 1) * d] for k in range(6)]
    z, xbc, dt = _ssm_in_proj(x2, jnp.stack([sh1, sc1], axis=1), gain_i[0:1], w_z_all, w_xbc_all,
                              w_dt_all, li, conv_w, conv_b, seq)
    y = _ssd(z, xbc, dt, dt_bias, a_log, d_skip, norm_w, batch, seq)

    wr = jnp.pad(w_router.astype(F32), ((0, 0), (0, LANES - N_EXPERTS)))
    wr_hi = wr.astype(BF16)
    wr_lo = (wr - wr_hi.astype(F32)).astype(BF16)
    x1, hf, rinfo, cnt = _router(x2, y, w_out_all, li, jnp.stack([g1, sh2, sc2], axis=1),
                                 gain_i[1:2], jnp.concatenate([wr_hi, wr_lo], axis=1), seq)
    pos_flat, tile_expert = _moe_slots(rinfo, cnt, x2.shape[0], moe_tm)
    xs = _dispatch(pos_flat, hf, sorted_init)
    ys = _ffn_moe(xs, tile_expert, w_gu_all, w_down_all, li, moe_tm)
    return _combine(pos_flat, x1, rinfo, g2[:, None, :], ys, seq), xs


def kernel(x, c, ada_w, ada_b, norm_gain, rel_bias, attn_w_qkv, attn_q_gain, attn_k_gain, attn_sinks, attn_w_o, ssm_w_in, ssm_conv_w, ssm_conv_b, ssm_dt_bias, ssm_a_log, ssm_d, ssm_norm_w, ssm_w_out, ffn_w_gu, ffn_w_down, moe_w_router, moe_w_gu, moe_w_down):
    batch, seq, d = x.shape
    moe_tm = 1024
    mod = _adaln_mod(c.astype(F32), ada_w.astype(F32), ada_b.astype(F32))
    bias = _bias_table(rel_bias)

    n_zx = 2 * D_INNER + D_BC
    w_qkv_all, w_o_all = _attn_weights(attn_w_qkv, attn_w_o)
    ffn_gu_all = ffn_w_gu.astype(BF16)
    ffn_down_all = ffn_w_down.astype(BF16)
    w_z_all = ssm_w_in[:, :, :D_INNER].astype(BF16)
    w_xbc_all = ssm_w_in[:, :, D_INNER:n_zx].astype(BF16)
    w_dt_all = jnp.pad(ssm_w_in[:, :, n_zx:], ((0, 0), (0, 0), (0, LANES - N_SSM_HEADS))).astype(BF16)
    w_out_all = ssm_w_out.astype(BF16)
    moe_gu_all = moe_w_gu.astype(BF16)
    moe_down_all = moe_w_down.astype(BF16)

    n_sorted = TOP_K * batch * seq + N_EXPERTS * moe_tm
    sorted_buf = jnp.zeros((n_sorted * SUBLANES, LANES), F32)

    x2 = x.reshape(batch * seq, d).astype(F32)
    for i in range(DEPTH):
        j = i // 2
        gain_i = norm_gain[i].astype(F32)
        if i % 2 == 0:
            x2 = _attn_layer(x2, mod[i], gain_i, bias, w_qkv_all, w_o_all, ffn_gu_all, ffn_down_all,
                             j, attn_q_gain[j], attn_k_gain[j], attn_sinks[j], batch, seq)
        else:
            x2, sorted_buf = _ssm_layer(x2, mod[i], gain_i, w_z_all, w_xbc_all, w_dt_all, w_out_all, moe_gu_all,
                                        moe_down_all, j, ssm_conv_w[j], ssm_conv_b[j], ssm_dt_bias[j],
                                        ssm_a_log[j], ssm_d[j], ssm_norm_w[j], moe_w_router[j],
                                        sorted_buf, batch, seq, moe_tm)
    return x2.reshape(batch, seq, d).astype(x.dtype)
```

```python
import functools
import math

import numpy as np
import jax
import jax.numpy as jnp
from jax import lax
from jax.experimental import pallas as pl
from jax.experimental.pallas import tpu as pltpu

F32 = jnp.float32
BF16 = jnp.bfloat16

D_MODEL = 1024
DEPTH = 4
N_Q_HEADS = 16
N_KV_HEADS = 4
HEAD_DIM = 64
Q_PER_KV = 4
ATTN_BLOCK = 128
N_BUCKETS = 32
MAX_DISTANCE = 128
D_INNER = 2048
N_SSM_HEADS = 32
N_SSM_GROUPS = 4
D_STATE = 128
CONV_WIDTH = 4
SSM_CHUNK = 128
D_BC = 2 * N_SSM_GROUPS * D_STATE
D_FF = 3584
N_EXPERTS = 8
TOP_K = 2
NORM_EPS = 1e-6

LANES = 128
SUBLANES = 8
VMEM_LIMIT = 56 * 1024 * 1024
NEG_INF = float("-inf")


def _cparams(sem):
    return pltpu.CompilerParams(dimension_semantics=sem, vmem_limit_bytes=VMEM_LIMIT)


def _sigmoid(v):
    return 1.0 / (1.0 + jnp.exp(-v))


def _rms_mod(x, gain, scale, shift):
    ms = jnp.mean(x * x, axis=-1, keepdims=True)
    y = x * lax.rsqrt(ms + NORM_EPS)
    return (y * gain) * (1.0 + scale) + shift


def _split2(v):
    hi = v.astype(BF16)
    lo = (v - hi.astype(F32)).astype(BF16)
    return hi, lo


def _split3(v):
    hi = v.astype(BF16)
    r = v - hi.astype(F32)
    mid = r.astype(BF16)
    lo = (r - mid.astype(F32)).astype(BF16)
    return hi, mid, lo


def _dot(a, b):
    return jnp.dot(a, b, preferred_element_type=F32)


def _store_tile_rows(ref, row0, val):
    n = val.shape[0]
    for c in range(val.shape[1] // LANES):
        ref[pl.ds(row0 * SUBLANES + c, n, stride=SUBLANES), :] = val[:, c * LANES:(c + 1) * LANES]


def _load_tile_rows(ref, n):
    return [ref[pl.ds(c, n, stride=SUBLANES), :] for c in range(SUBLANES)]


def _mod_kernel(c_ref, w_ref, b_ref, o_ref):
    c = c_ref[...]
    cond = c * _sigmoid(c)
    o_ref[0] = _dot(cond.astype(BF16), w_ref[0].astype(BF16)) + b_ref[0]


def _adaln_mod(c, ada_w, ada_b):
    depth, d, n = ada_w.shape
    b = c.shape[0]
    tn = 1024
    c_pad = jnp.zeros((SUBLANES, d), F32).at[:b].set(c)
    out = pl.pallas_call(
        _mod_kernel,
        out_shape=jax.ShapeDtypeStruct((depth, SUBLANES, n), F32),
        grid=(depth, n // tn),
        in_specs=[
            pl.BlockSpec((SUBLANES, d), lambda i, j: (0, 0)),
            pl.BlockSpec((1, d, tn), lambda i, j: (i, 0, j)),
            pl.BlockSpec((1, 1, tn), lambda i, j: (i, 0, j)),
        ],
        out_specs=pl.BlockSpec((1, SUBLANES, tn), lambda i, j: (i, 0, j)),
        compiler_params=_cparams(("arbitrary", "arbitrary")),
        name="adaln_mod",
    )(c_pad, ada_w, ada_b.reshape(depth, 1, n))
    return out[:, :b]


def _t5_bucket_table():
    qi = np.arange(ATTN_BLOCK)[:, None]
    kj = np.arange(2 * ATTN_BLOCK)[None, :]
    dist = qi + ATTN_BLOCK - kj
    max_exact = N_BUCKETS // 2
    d = np.maximum(dist, 0)
    df = np.maximum(d, 1).astype(np.float64)
    large = max_exact + (np.log(df / max_exact) / math.log(MAX_DISTANCE / max_exact)
                         * (N_BUCKETS - max_exact)).astype(np.int32)
    large = np.minimum(large, N_BUCKETS - 1)
    bucket = np.where(d < max_exact, d, large)
    valid = (dist >= 0) & (dist < ATTN_BLOCK)
    return np.where(valid, bucket, -1).astype(np.int32)


def _bias_kernel(rb_ref, bkt_ref, o_ref):
    j = pl.program_id(0)
    gp = j // Q_PER_KV
    qi = j % Q_PER_KV
    bkt = bkt_ref[...]
    for a in range(2):
        h = (2 * gp + a) * Q_PER_KV + qi
        acc = jnp.full(bkt.shape, NEG_INF, F32)
        for b in range(N_BUCKETS):
            acc = jnp.where(bkt == b, rb_ref[b, h], acc)
        o_ref[0, a * ATTN_BLOCK:(a + 1) * ATTN_BLOCK, :] = acc


def _bias_table(rel_bias):
    bkt = jnp.asarray(_t5_bucket_table())
    n_slots = N_Q_HEADS // 2
    return pl.pallas_call(
        _bias_kernel,
        out_shape=jax.ShapeDtypeStruct((n_slots, 2 * ATTN_BLOCK, 2 * ATTN_BLOCK), F32),
        grid=(n_slots,),
        in_specs=[
            pl.BlockSpec(memory_space=pltpu.SMEM),
            pl.BlockSpec((ATTN_BLOCK, 2 * ATTN_BLOCK), lambda j: (0, 0)),
        ],
        out_specs=pl.BlockSpec((1, 2 * ATTN_BLOCK, 2 * ATTN_BLOCK), lambda j: (j, 0, 0)),
        compiler_params=_cparams(("arbitrary",)),
        name="t5_bias_table",
    )(rel_bias.astype(F32), bkt)


def _qkv_kernel(x_ref, ms_ref, gain_ref, w_ref, qg_ref, kg_ref, q_ref, k_ref, v_ref):
    tm = x_ref.shape[0]
    ms = ms_ref[0]
    hm = _rms_mod(x_ref[...], gain_ref[...], ms[1:2], ms[0:1]).astype(BF16)
    lo = lax.broadcasted_iota(jnp.int32, (tm, LANES), 1) < HEAD_DIM

    def head_norm(slab, g):
        sq = slab * slab
        s_lo = jnp.sum(jnp.where(lo, sq, 0.0), axis=-1, keepdims=True)
        s_hi = jnp.sum(jnp.where(lo, 0.0, sq), axis=-1, keepdims=True)
        msq = jnp.where(lo, s_lo, s_hi) * (1.0 / HEAD_DIM)
        return slab * lax.rsqrt(msq + NORM_EPS) * g

    nq = N_Q_HEADS * HEAD_DIM
    nk = N_KV_HEADS * HEAD_DIM
    q_all = _dot(hm, w_ref[:, :nq])
    for j in range(nq // LANES):
        sl = slice(j * LANES, (j + 1) * LANES)
        q_ref[:, sl] = (head_norm(q_all[:, sl], qg_ref[...]) * (HEAD_DIM ** -0.5)).astype(BF16)
    k_all = _dot(hm, w_ref[:, nq:nq + nk])
    for j in range(nk // LANES):
        sl = slice(j * LANES, (j + 1) * LANES)
        k_ref[:, sl] = head_norm(k_all[:, sl], kg_ref[...]).astype(BF16)
    v_ref[...] = _dot(hm, w_ref[:, nq + nk:]).astype(BF16)


def _qkv_proj(x2, ms, gain, w_qkv_all, li, qg2, kg2, seq, tm=512):
    t, d = x2.shape
    nq = N_Q_HEADS * HEAD_DIM
    nk = N_KV_HEADS * HEAD_DIM
    tpb = seq // tm
    return pl.pallas_call(
        _qkv_kernel,
        out_shape=(jax.ShapeDtypeStruct((t, nq), BF16),
                   jax.ShapeDtypeStruct((t, nk), BF16),
                   jax.ShapeDtypeStruct((t, nk), BF16)),
        grid=(t // tm,),
        in_specs=[
            pl.BlockSpec((tm, d), lambda i: (i, 0)),
            pl.BlockSpec((1, 2, d), lambda i: (i // tpb, 0, 0)),
            pl.BlockSpec((1, d), lambda i: (0, 0)),
            pl.BlockSpec((None, d, nq + 2 * nk), lambda i: (li, 0, 0)),
            pl.BlockSpec((1, LANES), lambda i: (0, 0)),
            pl.BlockSpec((1, LANES), lambda i: (0, 0)),
        ],
        out_specs=(pl.BlockSpec((tm, nq), lambda i: (i, 0)),
                   pl.BlockSpec((tm, nk), lambda i: (i, 0)),
                   pl.BlockSpec((tm, nk), lambda i: (i, 0))),
        compiler_params=_cparams(("arbitrary",)),
        name="attn_qkv",
    )(x2, ms, gain, w_qkv_all, qg2, kg2)


def _attn_kernel(sink_ref, q_ref, kc_ref, kp_ref, vc_ref, vp_ref, bias_ref, o_ref):
    blk = ATTN_BLOCK
    first = pl.program_id(1) == 0
    lane = lax.broadcasted_iota(jnp.int32, (blk, LANES), 1)
    lo = lane < HEAD_DIM
    mask_lo = jnp.where(lo, 1.0, 0.0).astype(BF16)
    mask_hi = jnp.where(lo, 0.0, 1.0).astype(BF16)
    row = lax.broadcasted_iota(jnp.int32, (2 * blk, 1), 0)
    nt = (((1,), (1,)), ((), ()))
    for gp in range(N_KV_HEADS // 2):
        ksl = slice(gp * LANES, (gp + 1) * LANES)
        kc = kc_ref[:, ksl]
        kp = kp_ref[:, ksl]
        vc = vc_ref[:, ksl]
        vp = vp_ref[:, ksl]
        for qi in range(Q_PER_KV):
            j = gp * Q_PER_KV + qi
            sl = slice(j * LANES, (j + 1) * LANES)
            qs = q_ref[:, sl]
            q2 = jnp.concatenate([qs * mask_lo, qs * mask_hi], axis=0)
            s_p = lax.dot_general(q2, kp, nt, preferred_element_type=F32)
            s_c = lax.dot_general(q2, kc, nt, preferred_element_type=F32)
            bias = bias_ref[j]
            s_p = jnp.where(first, NEG_INF, s_p + bias[:, :blk])
            s_c = s_c + bias[:, blk:]
            h_a = (2 * gp) * Q_PER_KV + qi
            h_b = (2 * gp + 1) * Q_PER_KV + qi
            sink = jnp.where(row < blk, sink_ref[h_a], sink_ref[h_b])
            m = jnp.maximum(jnp.max(jnp.maximum(s_p, s_c), axis=-1, keepdims=True), sink)
            p_p = jnp.exp(s_p - m)
            p_c = jnp.exp(s_c - m)
            den = jnp.sum(p_p + p_c, axis=-1, keepdims=True) + jnp.exp(sink - m)
            pv = _dot(p_p.astype(BF16), vp) + _dot(p_c.astype(BF16), vc)
            pv = pv / den
            o_ref[:, sl] = jnp.where(lo, pv[:blk], pv[blk:]).astype(BF16)


def _attention(q, k, v, bias, sinks, batch, seq):
    t, nq = q.shape
    nk = k.shape[1]
    blk = ATTN_BLOCK
    nb = seq // blk
    cur = lambda b, n, *_: (b * nb + n, 0)
    prev = lambda b, n, *_: (b * nb + jnp.maximum(n - 1, 0), 0)
    return pl.pallas_call(
        _attn_kernel,
        out_shape=jax.ShapeDtypeStruct((t, nq), BF16),
        grid=(batch, nb),
        in_specs=[
            pl.BlockSpec(memory_space=pltpu.SMEM),
            pl.BlockSpec((blk, nq), cur),
            pl.BlockSpec((blk, nk), cur),
            pl.BlockSpec((blk, nk), prev),
            pl.BlockSpec((blk, nk), cur),
            pl.BlockSpec((blk, nk), prev),
            pl.BlockSpec(bias.shape, lambda b, n: (0, 0, 0)),
        ],
        out_specs=pl.BlockSpec((blk, nq), cur),
        compiler_params=_cparams(("arbitrary", "arbitrary")),
        name="swa_attention",
    )(sinks.astype(F32), q, k, k, v, v, bias)


def _swiglu_step(hf, wg_ref, wu_ref, wd_ref, acc_s):
    g = _dot(hf, wg_ref[...])
    u = _dot(hf, wu_ref[...])
    a = (g * _sigmoid(g) * u).astype(BF16)
    acc_s[...] += _dot(a, wd_ref[...])


def _ffn_dense_kernel(x_ref, o_ref, wo_ref, mv_ref, gain_ref, wg_ref, wu_ref, wd_ref,
                      out_ref, hf_s, acc_s):
    j = pl.program_id(1)

    @pl.when(j == 0)
    def _():
        mv = mv_ref[0]
        x1 = x_ref[...] + mv[0:1] * _dot(o_ref[...], wo_ref[...])
        out_ref[...] = x1
        hf_s[...] = _rms_mod(x1, gain_ref[...], mv[2:3], mv[1:2]).astype(BF16)
        acc_s[...] = jnp.zeros_like(acc_s)

    _swiglu_step(hf_s[...], wg_ref, wu_ref, wd_ref, acc_s)

    @pl.when(j == pl.num_programs(1) - 1)
    def _():
        out_ref[...] = out_ref[...] + mv_ref[0][3:4] * acc_s[...]


def _ffn_dense(x2, o, wo_all, mv, gain, w_gu_all, w_down_all, li, seq, tm=1024, tf=512):
    t, d = x2.shape
    f = w_down_all.shape[1]
    nj = f // tf
    tpb = seq // tm
    return pl.pallas_call(
        _ffn_dense_kernel,
        out_shape=jax.ShapeDtypeStruct((t, d), F32),
        grid=(t // tm, nj),
        in_specs=[
            pl.BlockSpec((tm, d), lambda i, j: (i, 0)),
            pl.BlockSpec((tm, o.shape[1]), lambda i, j: (i, 0)),
            pl.BlockSpec((None,) + wo_all.shape[1:], lambda i, j: (li, 0, 0)),
            pl.BlockSpec((1, 4, d), lambda i, j: (i // tpb, 0, 0)),
            pl.BlockSpec((1, d), lambda i, j: (0, 0)),
            pl.BlockSpec((None, d, tf), lambda i, j: (li, 0, j)),
            pl.BlockSpec((None, d, tf), lambda i, j: (li, 0, nj + j)),
            pl.BlockSpec((None, tf, d), lambda i, j: (li, j, 0)),
        ],
        out_specs=pl.BlockSpec((tm, d), lambda i, j: (i, 0)),
        scratch_shapes=[pltpu.VMEM((tm, d), BF16), pltpu.VMEM((tm, d), F32)],
        compiler_params=_cparams(("arbitrary", "arbitrary")),
        name="wo_ffn_dense",
    )(x2, o, wo_all, mv, gain, w_gu_all, w_gu_all, w_down_all)


def _ffn_moe_kernel(te_ref, xs_ref, wg_ref, wu_ref, wd_ref, out_ref, hf_s, acc_s):
    i = pl.program_id(0)
    j = pl.program_id(1)
    used = i < te_ref[pl.num_programs(0)]

    tm = hf_s.shape[0]
    last = pl.num_programs(1) - 1

    def partial_product():
        hf = hf_s[...]
        g = _dot(hf, wg_ref[...].astype(BF16))
        u = _dot(hf, wu_ref[...].astype(BF16))
        a = (g * _sigmoid(g) * u).astype(BF16)
        return _dot(a, wd_ref[...].astype(BF16))

    @pl.when(used & (j == 0))
    def _():
        for c, part in enumerate(_load_tile_rows(xs_ref, tm)):
            hf_s[:, c * LANES:(c + 1) * LANES] = part.astype(BF16)
        acc_s[...] = partial_product()

    @pl.when(used & (j > 0) & (j < last))
    def _():
        acc_s[...] += partial_product()

    @pl.when(used & (j == last))
    def _():
        _store_tile_rows(out_ref, 0, acc_s[...] + partial_product())

    @pl.when(jnp.logical_not(used) & (j == last))
    def _():
        out_ref[...] = jnp.zeros_like(out_ref)


def _ffn_moe(xs, tile_expert, w_gu_all, w_down_all, li, tm, tf=512):
    r = xs.shape[0] // SUBLANES
    d = w_gu_all.shape[2]
    f = w_down_all.shape[2]
    nj = f // tf
    nt = r // tm

    def jj(i, j, te):
        return jnp.where(i < te[nt], j, 0)

    return pl.pallas_call(
        _ffn_moe_kernel,
        out_shape=jax.ShapeDtypeStruct((r * SUBLANES, LANES), F32),
        grid_spec=pltpu.PrefetchScalarGridSpec(
            num_scalar_prefetch=1,
            grid=(nt, nj),
            in_specs=[
                pl.BlockSpec((tm * SUBLANES, LANES), lambda i, j, te: (i, 0)),
                pl.BlockSpec((None, None, d, tf), lambda i, j, te: (li, te[i], 0, jj(i, j, te))),
                pl.BlockSpec((None, None, d, tf), lambda i, j, te: (li, te[i], 0, nj + jj(i, j, te))),
                pl.BlockSpec((None, None, tf, d), lambda i, j, te: (li, te[i], jj(i, j, te), 0)),
            ],
            out_specs=pl.BlockSpec((tm * SUBLANES, LANES), lambda i, j, te: (i, 0)),
            scratch_shapes=[pltpu.VMEM((tm, d), BF16), pltpu.VMEM((tm, d), F32)],
        ),
        compiler_params=_cparams(("arbitrary", "arbitrary")),
        name="moe_expert_ffn",
    )(tile_expert, xs, w_gu_all, w_gu_all, w_down_all)


def _inproj_kernel(x_ref, ms_ref, gain_ref, wz_ref, wx_ref, wdt_ref, cw_ref, cb_ref,
                   z_ref, xbc_ref, dt_ref, hm_s, carry_s, *, tiles_per_seq):
    i = pl.program_id(0)
    j = pl.program_id(1)
    tm = z_ref.shape[0]

    @pl.when((i == 0) & (j == 0))
    def _():
        carry_s[...] = jnp.zeros_like(carry_s)

    @pl.when(j == 0)
    def _():
        ms = ms_ref[0]
        hm = _rms_mod(x_ref[...], gain_ref[...], ms[1:2], ms[0:1]).astype(BF16)
        hm_s[...] = hm
        dt_ref[...] = _dot(hm, wdt_ref[...])

    hm = hm_s[...]
    strip = 2 * LANES
    rowi = lax.broadcasted_iota(jnp.int32, (SUBLANES, strip), 0)
    first = i % tiles_per_seq == 0
    n_z_strips = z_ref.shape[1] // strip
    for k, c0 in enumerate(range(0, xbc_ref.shape[1], strip)):
        cs = slice(c0, c0 + strip)
        res = _dot(hm, wx_ref[:, cs])
        if k < n_z_strips:
            zs = slice(k * strip, (k + 1) * strip)
            z_ref[:, zs] = _dot(hm, wz_ref[:, zs]).astype(BF16)
        history = jnp.where(first, 0.0, carry_s[j, :, cs])
        acc = res * cw_ref[CONV_WIDTH - 1:CONV_WIDTH, cs] + cb_ref[:, cs]
        for s in range(1, CONV_WIDTH):
            down = pltpu.roll(res, s, 0)
            head = jnp.where(rowi < s, pltpu.roll(history, s, 0), down[:SUBLANES])
            shifted = jnp.concatenate([head, down[SUBLANES:]], axis=0)
            acc = acc + shifted * cw_ref[CONV_WIDTH - 1 - s:CONV_WIDTH - s, cs]
        carry_s[j, :, cs] = res[tm - SUBLANES:]
        xbc_ref[:, cs] = (acc * _sigmoid(acc)).astype(BF16)


def _ssm_in_proj(x2, ms, gain, w_z_all, w_xbc_all, w_dt_all, li, conv_w, conv_b, seq, tm=1024, nj=2):
    t, d = x2.shape
    nz = w_z_all.shape[2] // nj
    nx = w_xbc_all.shape[2] // nj
    tpb = seq // tm
    return pl.pallas_call(
        functools.partial(_inproj_kernel, tiles_per_seq=tpb),
        out_shape=(jax.ShapeDtypeStruct((t, nz * nj), BF16), jax.ShapeDtypeStruct((t, nx * nj), BF16),
                   jax.ShapeDtypeStruct((t, LANES), F32)),
        grid=(t // tm, nj),
        in_specs=[
            pl.BlockSpec((tm, d), lambda i, j: (i, 0)),
            pl.BlockSpec((1, 2, d), lambda i, j: (i // tpb, 0, 0)),
            pl.BlockSpec((1, d), lambda i, j: (0, 0)),
            pl.BlockSpec((None, d, nz), lambda i, j: (li, 0, j)),
            pl.BlockSpec((None, d, nx), lambda i, j: (li, 0, j)),
            pl.BlockSpec((None, d, LANES), lambda i, j: (li, 0, 0)),
            pl.BlockSpec((CONV_WIDTH, nx), lambda i, j: (0, j)),
            pl.BlockSpec((1, nx), lambda i, j: (0, j)),
        ],
        out_specs=(pl.BlockSpec((tm, nz), lambda i, j: (i, j)),
                   pl.BlockSpec((tm, nx), lambda i, j: (i, j)),
                   pl.BlockSpec((tm, LANES), lambda i, j: (i, 0))),
        scratch_shapes=[pltpu.VMEM((tm, d), BF16), pltpu.VMEM((nj, SUBLANES, nx), F32)],
        compiler_params=_cparams(("arbitrary", "arbitrary")),
        name="ssm_in_proj",
    )(x2, ms, gain, w_z_all, w_xbc_all, w_dt_all, conv_w.astype(F32), conv_b.astype(F32)[None])


def _ssd_kernel(z_ref, xs_ref, bc_ref, dt_ref, dtb_ref, alog_ref, dskip_ref, nw_ref, r64_ref,
                y_ref, state_s, ybuf_s):
    L = SSM_CHUNK
    gw = D_INNER // N_SSM_GROUPS
    hpg = N_SSM_HEADS // N_SSM_GROUPS

    @pl.when(pl.program_id(1) == 0)
    def _():
        state_s[...] = jnp.zeros_like(state_s)

    dtv = jax.nn.softplus(dt_ref[...] + dtb_ref[...])
    a = -jnp.exp(alog_ref[...])
    dta = dtv * a
    ri = lax.broadcasted_iota(jnp.int32, (L, L), 0)
    ci = lax.broadcasted_iota(jnp.int32, (L, L), 1)
    causal = ri >= ci
    tri = jnp.where(causal, 1.0, 0.0).astype(BF16)
    p0, p1, p2 = _split3(dta)
    acs = _dot(tri, p0) + _dot(tri, p1) + _dot(tri, p2)
    acs_t = acs.T
    eacs = jnp.exp(acs)
    dte = jnp.exp(acs[L - 1:L, :] - acs)
    stacked = jnp.concatenate([dtv, eacs, dte], axis=0)
    s_hi, s_lo = _split2(stacked)
    expd = _dot(s_hi, r64_ref[...]) + _dot(s_lo, r64_ref[...])

    lo = lax.broadcasted_iota(jnp.int32, (L, LANES), 1) < (LANES // 2)
    for g in range(N_SSM_GROUPS):
        gs = slice(g * gw, (g + 1) * gw)
        b_g = bc_ref[:, g * D_STATE:(g + 1) * D_STATE]
        c_b = bc_ref[:, (N_SSM_GROUPS + g) * D_STATE:(N_SSM_GROUPS + g + 1) * D_STATE]
        b_t = b_g.astype(F32).T.astype(BF16)
        cb = _dot(c_b, b_t)
        xs_g = xs_ref[:, gs].astype(F32)
        xd = xs_g * expd[0:L, gs]
        xd_b = xd.astype(BF16)
        st_prev = state_s[g]
        y_off = _dot(c_b, st_prev.astype(BF16)) * expd[L:2 * L, gs]
        xdw = (xd * expd[2 * L:3 * L, gs]).astype(BF16)
        state_s[g] = expd[2 * L - 1:2 * L, gs] * st_prev + _dot(b_t, xdw)
        for jp in range(hpg // 2):
            ps = slice(jp * LANES, (jp + 1) * LANES)
            slab = xd_b[:, ps]
            halves = []
            for half in range(2):
                hh = g * hpg + 2 * jp + half
                diff = acs[:, hh:hh + 1] - acs_t[hh:hh + 1, :]
                dec = jnp.exp(jnp.where(causal, diff, NEG_INF))
                halves.append(_dot((cb * dec).astype(BF16), slab))
            y = jnp.where(lo, halves[0], halves[1]) + y_off[:, ps]
            cols = slice(g * gw + jp * LANES, g * gw + (jp + 1) * LANES)
            ybuf_s[:, cols] = y + dskip_ref[:, cols] * xs_g[:, ps]

    z = z_ref[...].astype(F32)
    yg = ybuf_s[...] * (z * _sigmoid(z))
    ms = jnp.mean(yg * yg, axis=-1, keepdims=True)
    y_ref[...] = (yg * lax.rsqrt(ms + NORM_EPS) * nw_ref[...]).astype(BF16)


def _ssd(z, xbc, dt, dt_bias, a_log, d_skip, norm_w, batch, seq):
    t = z.shape[0]
    L = SSM_CHUNK
    nc = seq // L
    di, dbc = D_INNER, D_BC
    pad = LANES - N_SSM_HEADS
    dtb = jnp.pad(dt_bias.astype(F32), (0, pad))[None]
    alog = jnp.pad(a_log.astype(F32), (0, pad))[None]
    dskip = jnp.repeat(d_skip.astype(F32), di // N_SSM_HEADS)[None]
    nw = norm_w.astype(F32)[None]
    r64 = jnp.asarray((np.arange(LANES)[:, None] == (np.arange(di)[None, :] // (di // N_SSM_HEADS)))
                      .astype(np.float32)).astype(BF16)
    row = lambda b, c: (b * nc + c, 0)
    const = lambda b, c: (0, 0)
    return pl.pallas_call(
        _ssd_kernel,
        out_shape=jax.ShapeDtypeStruct((t, di), BF16),
        grid=(batch, nc),
        in_specs=[
            pl.BlockSpec((L, di), row),
            pl.BlockSpec((L, di), row),
            pl.BlockSpec((L, dbc), lambda b, c: (b * nc + c, di // dbc)),
            pl.BlockSpec((L, LANES), row),
            pl.BlockSpec((1, LANES), const),
            pl.BlockSpec((1, LANES), const),
            pl.BlockSpec((1, di), const),
            pl.BlockSpec((1, di), const),
            pl.BlockSpec((LANES, di), const),
        ],
        out_specs=pl.BlockSpec((L, di), row),
        scratch_shapes=[
            pltpu.VMEM((N_SSM_GROUPS, D_STATE, di // N_SSM_GROUPS), F32),
            pltpu.VMEM((L, di), F32),
        ],
        compiler_params=_cparams(("arbitrary", "arbitrary")),
        name="ssd_scan",
    )(z, xbc, xbc, dt, dtb, alog, dskip, nw, r64)


ROUTER_SUB = 512


def _router_kernel(x_ref, y_ref, wout_ref, mv_ref, gain_ref, wr_ref,
                   x1_ref, hf_ref, rinfo_ref, cnt_ref, carry_s):
    tm = x_ref.shape[0]
    sub = ROUTER_SUB

    @pl.when(pl.program_id(0) == 0)
    def _():
        carry_s[...] = jnp.zeros_like(carry_s)

    mv = mv_ref[0]
    lane = lax.broadcasted_iota(jnp.int32, (sub, LANES), 1)
    lanef = lane.astype(F32)
    ri = lax.broadcasted_iota(jnp.int32, (sub, sub), 0)
    ci = lax.broadcasted_iota(jnp.int32, (sub, sub), 1)
    before = jnp.where(ri > ci, 1.0, 0.0).astype(BF16)
    carry = carry_s[...]
    for h in range(tm // sub):
        rs = slice(h * sub, (h + 1) * sub)
        x1 = x_ref[rs, :] + mv[0:1] * _dot(y_ref[rs, :], wout_ref[...])
        x1_ref[rs, :] = x1
        hf = _rms_mod(x1, gain_ref[...], mv[2:3], mv[1:2])
        _store_tile_rows(hf_ref, h * sub, hf)
        h_hi, h_lo = _split2(hf)
        hw = _dot(h_hi, wr_ref[...])
        logits = hw[:, :LANES] + hw[:, LANES:] + _dot(h_lo, wr_ref[:, :LANES])
        lg = jnp.where(lane < N_EXPERTS, logits, NEG_INF)
        v0 = jnp.max(lg, axis=-1, keepdims=True)
        i0 = jnp.min(jnp.where(lg == v0, lanef, float(LANES)), axis=-1, keepdims=True)
        lg1 = jnp.where(lanef == i0, NEG_INF, lg)
        v1 = jnp.max(lg1, axis=-1, keepdims=True)
        i1 = jnp.min(jnp.where(lg1 == v1, lanef, float(LANES)), axis=-1, keepdims=True)
        e1 = jnp.exp(v1 - v0)
        w0 = 1.0 / (1.0 + e1)
        w1 = e1 / (1.0 + e1)
        oh0 = jnp.where(lanef == i0, 1.0, 0.0)
        oh1 = jnp.where(lanef == i1, 1.0, 0.0)
        pre = _dot(before, jnp.concatenate([oh0, oh1], axis=1).astype(BF16))
        cnt0 = jnp.sum(oh0, axis=0, keepdims=True)
        cnt1 = jnp.sum(oh1, axis=0, keepdims=True)
        rank0 = jnp.sum(oh0 * (pre[:, :LANES] + carry), axis=-1, keepdims=True)
        rank1 = jnp.sum(oh1 * (pre[:, LANES:] + carry + cnt0), axis=-1, keepdims=True)
        carry = carry + cnt0 + cnt1
        rinfo_ref[rs, :] = jnp.where(lane == 0, i0, jnp.where(lane == 1, i1, jnp.where(
            lane == 2, rank0, jnp.where(lane == 3, rank1, jnp.where(
                lane == 4, w0, jnp.where(lane == 5, w1, 0.0))))))
    carry_s[...] = carry
    cnt_ref[...] = carry


def _router(x2, y, wout_all, li, mv, gain, wr, seq, tm=1024):
    t, d = x2.shape
    tpb = seq // tm
    return pl.pallas_call(
        _router_kernel,
        out_shape=(jax.ShapeDtypeStruct((t, d), F32),
                   jax.ShapeDtypeStruct((t * SUBLANES, LANES), F32),
                   jax.ShapeDtypeStruct((t, LANES), F32), jax.ShapeDtypeStruct((1, LANES), F32)),
        grid=(t // tm,),
        in_specs=[
            pl.BlockSpec((tm, d), lambda i: (i, 0)),
            pl.BlockSpec((tm, y.shape[1]), lambda i: (i, 0)),
            pl.BlockSpec((None,) + wout_all.shape[1:], lambda i: (li, 0, 0)),
            pl.BlockSpec((1, 3, d), lambda i: (i // tpb, 0, 0)),
            pl.BlockSpec((1, d), lambda i: (0, 0)),
            pl.BlockSpec((d, 2 * LANES), lambda i: (0, 0)),
        ],
        out_specs=(pl.BlockSpec((tm, d), lambda i: (i, 0)),
                   pl.BlockSpec((tm * SUBLANES, LANES), lambda i: (i, 0)),
                   pl.BlockSpec((tm, LANES), lambda i: (i, 0)),
                   pl.BlockSpec((1, LANES), lambda i: (0, 0))),
        scratch_shapes=[pltpu.VMEM((1, LANES), F32)],
        compiler_params=_cparams(("arbitrary",)),
        name="ssm_out_router",
    )(x2, y, wout_all, mv, gain, wr)


DMA_UNROLL = 8


def _row_copy(src, src_row, dst, dst_row, sem):
    return pltpu.make_async_copy(
        src.at[pl.ds(pl.multiple_of(src_row * SUBLANES, SUBLANES), SUBLANES)],
        dst.at[pl.ds(pl.multiple_of(dst_row * SUBLANES, SUBLANES), SUBLANES)], sem)


def _dispatch_kernel(pos_ref, hf_ref, sorted_in_ref, sorted_ref, sem):
    del sorted_in_ref
    tm = hf_ref.shape[0] // SUBLANES
    base = pl.program_id(0) * (TOP_K * tm)

    def issue(blk, carry):
        for u in range(DMA_UNROLL):
            t = blk * DMA_UNROLL + u
            for k in range(TOP_K):
                _row_copy(hf_ref, t, sorted_ref, pos_ref[base + TOP_K * t + k], sem).start(priority=k)
        return carry

    lax.fori_loop(0, tm // DMA_UNROLL, issue, 0)
    for k in range(TOP_K):
        pltpu.make_async_copy(hf_ref, sorted_ref.at[pl.ds(0, tm * SUBLANES)], sem).wait()


def _dispatch(pos_flat, hf, sorted_init, tm=1024):
    t = hf.shape[0] // SUBLANES
    return pl.pallas_call(
        _dispatch_kernel,
        out_shape=jax.ShapeDtypeStruct(sorted_init.shape, hf.dtype),
        grid_spec=pltpu.PrefetchScalarGridSpec(
            num_scalar_prefetch=1,
            grid=(t // tm,),
            in_specs=[pl.BlockSpec((tm * SUBLANES, LANES), lambda i, pos: (i, 0)),
                      pl.BlockSpec(memory_space=pl.ANY)],
            out_specs=pl.BlockSpec(memory_space=pl.ANY),
            scratch_shapes=[pltpu.SemaphoreType.DMA],
        ),
        input_output_aliases={2: 0},
        compiler_params=_cparams(("arbitrary",)),
        name="moe_dispatch",
    )(pos_flat, hf, sorted_init)


def _combine_kernel(pos_ref, x1_ref, rinfo_ref, g2_ref, ys_ref, out_ref, buf, sem):
    tm, d = x1_ref.shape
    i = pl.program_id(0)
    slot = i % 2

    def issue_tile(tile, dst_slot):
        base = tile * (TOP_K * tm)

        def issue(blk, carry):
            for u in range(DMA_UNROLL):
                t = blk * DMA_UNROLL + u
                for k in range(TOP_K):
                    _row_copy(ys_ref, pos_ref[base + TOP_K * t + k], buf.at[dst_slot, k], t,
                              sem.at[dst_slot]).start(priority=k)
            return carry

        lax.fori_loop(0, tm // DMA_UNROLL, issue, 0)

    @pl.when(i == 0)
    def _():
        issue_tile(0, 0)

    @pl.when(i + 1 < pl.num_programs(0))
    def _():
        issue_tile(i + 1, 1 - slot)

    for k in range(TOP_K):
        pltpu.make_async_copy(ys_ref.at[pl.ds(0, tm * SUBLANES)], buf.at[slot, k], sem.at[slot]).wait()
    rinfo = rinfo_ref[...]
    w0 = rinfo[:, 4:5]
    w1 = rinfo[:, 5:6]
    g2 = g2_ref[0]
    for c in range(d // LANES):
        cs = slice(c * LANES, (c + 1) * LANES)
        ffn = (w0 * buf[slot, 0, pl.ds(c, tm, stride=SUBLANES), :]
               + w1 * buf[slot, 1, pl.ds(c, tm, stride=SUBLANES), :])
        out_ref[:, cs] = x1_ref[:, cs] + g2[:, cs] * ffn


def _combine(pos_flat, x1, rinfo, g2, ys, seq, tm=512):
    t, d = x1.shape
    tpb = seq // tm
    return pl.pallas_call(
        _combine_kernel,
        out_shape=jax.ShapeDtypeStruct((t, d), F32),
        grid_spec=pltpu.PrefetchScalarGridSpec(
            num_scalar_prefetch=1,
            grid=(t // tm,),
            in_specs=[pl.BlockSpec((tm, d), lambda i, pos: (i, 0)),
                      pl.BlockSpec((tm, LANES), lambda i, pos: (i, 0)),
                      pl.BlockSpec((1, 1, d), lambda i, pos: (i // tpb, 0, 0)),
                      pl.BlockSpec(memory_space=pl.ANY)],
            out_specs=pl.BlockSpec((tm, d), lambda i, pos: (i, 0)),
            scratch_shapes=[pltpu.VMEM((2, TOP_K, tm * SUBLANES, LANES), F32),
                            pltpu.SemaphoreType.DMA((2,))],
        ),
        compiler_params=_cparams(("arbitrary",)),
        name="moe_combine",
    )(pos_flat, x1, rinfo, g2, ys)


def _attn_weights(attn_w_qkv, attn_w_o):
    n_att, d, _ = attn_w_qkv.shape
    nq = N_Q_HEADS * HEAD_DIM
    head_grid = (N_KV_HEADS // 2, 2, Q_PER_KV, HEAD_DIM)
    w_q = attn_w_qkv[:, :, :nq].reshape((n_att, d) + head_grid)
    w_q = jnp.swapaxes(w_q, 3, 4).reshape(n_att, d, nq)
    w_qkv_all = jnp.concatenate([w_q, attn_w_qkv[:, :, nq:]], axis=2).astype(BF16)
    w_o = attn_w_o.reshape((n_att,) + head_grid + (attn_w_o.shape[2],))
    w_o_all = jnp.swapaxes(w_o, 2, 3).reshape(n_att, nq, attn_w_o.shape[2]).astype(BF16)
    return w_qkv_all, w_o_all


def _attn_layer(x2, mod_i, gain_i, bias, w_qkv_all, w_o_all, w_gu_all, w_down_all, li,
                q_gain, k_gain, sinks, batch, seq):
    d = D_MODEL
    sh1, sc1, g1, sh2, sc2, g2 = [mod_i[:, k * d:(k + 1) * d] for k in range(6)]
    qg2 = jnp.tile(q_gain.astype(F32), 2)[None]
    kg2 = jnp.tile(k_gain.astype(F32), 2)[None]
    q, k, v = _qkv_proj(x2, jnp.stack([sh1, sc1], axis=1), gain_i[0:1], w_qkv_all, li, qg2, kg2, seq)
    o = _attention(q, k, v, bias, sinks, batch, seq)
    mv = jnp.stack([g1, sh2, sc2, g2], axis=1)
    return _ffn_dense(x2, o, w_o_all, mv, gain_i[1:2], w_gu_all, w_down_all, li, seq)


def _moe_slots(rinfo, cnt, n_tokens, moe_tm):
    counts = cnt[0, :N_EXPERTS].astype(jnp.int32)
    padded = ((counts + moe_tm - 1) // moe_tm) * moe_tm
    ends = jnp.cumsum(padded)
    base = ends - padded
    eidx = rinfo[:, 0:TOP_K].astype(jnp.int32)
    rank = rinfo[:, 2:2 + TOP_K].astype(jnp.int32)
    pos = jnp.sum(jnp.where(eidx[..., None] == jnp.arange(N_EXPERTS), base, 0), axis=-1) + rank
    n_tiles = (TOP_K * n_tokens) // moe_tm + N_EXPERTS
    tile_start = jnp.arange(n_tiles, dtype=jnp.int32) * moe_tm
    tile_expert = jnp.minimum(jnp.sum(tile_start[:, None] >= ends[None, :], axis=-1),
                              N_EXPERTS - 1).astype(jnp.int32)
    n_used = ends[-1] // moe_tm
    tile_expert = jnp.where(tile_start < ends[-1], tile_expert, tile_expert[n_used - 1])
    return pos.reshape(-1), jnp.concatenate([tile_expert, n_used[None].astype(jnp.int32)])


def _ssm_layer(x2, mod_i, gain_i, w_z_all, w_xbc_all, w_dt_all, w_out_all, w_gu_all, w_down_all, li,
               conv_w, conv_b, dt_bias, a_log, d_skip, norm_w, w_router, sorted_init,
               batch, seq, moe_tm):
    d = D_MODEL
    sh1, sc1, g1, sh2, sc2, g2 = [mod_i[:, k * d:(k + 1) * d] for k in range(6)]
    z, xbc, dt = _ssm_in_proj(x2, jnp.stack([sh1, sc1], axis=1), gain_i[0:1], w_z_all, w_xbc_all,
                              w_dt_all, li, conv_w, conv_b, seq)
    y = _ssd(z, xbc, dt, dt_bias, a_log, d_skip, norm_w, batch, seq)

    wr = jnp.pad(w_router.astype(F32), ((0, 0), (0, LANES - N_EXPERTS)))
    wr_hi = wr.astype(BF16)
    wr_lo = (wr - wr_hi.astype(F32)).astype(BF16)
    x1, hf, rinfo, cnt = _router(x2, y, w_out_all, li, jnp.stack([g1, sh2, sc2], axis=1),
                                 gain_i[1:2], jnp.concatenate([wr_hi, wr_lo], axis=1), seq)
    pos_flat, tile_expert = _moe_slots(rinfo, cnt, x2.shape[0], moe_tm)
    xs = _dispatch(pos_flat, hf, sorted_init)
    ys = _ffn_moe(xs, tile_expert, w_gu_all, w_down_all, li, moe_tm)
    return _combine(pos_flat, x1, rinfo, g2[:, None, :], ys, seq), xs


def kernel(x, c, ada_w, ada_b, norm_gain, rel_bias, attn_w_qkv, attn_q_gain, attn_k_gain, attn_sinks, attn_w_o, ssm_w_in, ssm_conv_w, ssm_conv_b, ssm_dt_bias, ssm_a_log, ssm_d, ssm_norm_w, ssm_w_out, ffn_w_gu, ffn_w_down, moe_w_router, moe_w_gu, moe_w_down):
    batch, seq, d = x.shape
    moe_tm = 1024
    mod = _adaln_mod(c.astype(F32), ada_w.astype(F32), ada_b.astype(F32))
    bias = _bias_table(rel_bias)

    n_zx = 2 * D_INNER + D_BC
    w_qkv_all, w_o_all = _attn_weights(attn_w_qkv, attn_w_o)
    ffn_gu_all = ffn_w_gu.astype(BF16)
    ffn_down_all = ffn_w_down.astype(BF16)
    w_z_all = ssm_w_in[:, :, :D_INNER].astype(BF16)
    w_xbc_all = ssm_w_in[:, :, D_INNER:n_zx].astype(BF16)
    w_dt_all = jnp.pad(ssm_w_in[:, :, n_zx:], ((0, 0), (0, 0), (0, LANES - N_SSM_HEADS))).astype(BF16)
    w_out_all = ssm_w_out.astype(BF16)
    moe_gu_all = moe_w_gu.astype(F32)
    moe_down_all = moe_w_down.astype(F32)

    n_sorted = TOP_K * batch * seq + N_EXPERTS * moe_tm
    sorted_buf = jnp.zeros((n_sorted * SUBLANES, LANES), F32)

    x2 = x.reshape(batch * seq, d).astype(F32)
    for i in range(DEPTH):
        j = i // 2
        gain_i = norm_gain[i].astype(F32)
        if i % 2 == 0:
            x2 = _attn_layer(x2, mod[i], gain_i, bias, w_qkv_all, w_o_all, ffn_gu_all, ffn_down_all,
                             j, attn_q_gain[j], attn_k_gain[j], attn_sinks[j], batch, seq)
        else:
            x2, sorted_buf = _ssm_layer(x2, mod[i], gain_i, w_z_all, w_xbc_all, w_dt_all, w_out_all, moe_gu_all,
                                        moe_down_all, j, ssm_conv_w[j], ssm_conv_b[j], ssm_dt_bias[j],
                                        ssm_a_log[j], ssm_d[j], ssm_norm_w[j], moe_w_router[j],
                                        sorted_buf, batch, seq, moe_tm)
    return x2.reshape(batch, seq, d).astype(x.dtype)
```

```python
import functools
import math

import numpy as np
import jax
import jax.numpy as jnp
from jax import lax
from jax.experimental import pallas as pl
from jax.experimental.pallas import tpu as pltpu

F32 = jnp.float32
BF16 = jnp.bfloat16

D_MODEL = 1024
DEPTH = 4
N_Q_HEADS = 16
N_KV_HEADS = 4
HEAD_DIM = 64
Q_PER_KV = 4
ATTN_BLOCK = 128
N_BUCKETS = 32
MAX_DISTANCE = 128
D_INNER = 2048
N_SSM_HEADS = 32
N_SSM_GROUPS = 4
D_STATE = 128
CONV_WIDTH = 4
SSM_CHUNK = 128
D_BC = 2 * N_SSM_GROUPS * D_STATE
D_FF = 3584
N_EXPERTS = 8
TOP_K = 2
NORM_EPS = 1e-6

LANES = 128
SUBLANES = 8
VMEM_LIMIT = 56 * 1024 * 1024
NEG_INF = float("-inf")


def _cparams(sem):
    return pltpu.CompilerParams(dimension_semantics=sem, vmem_limit_bytes=VMEM_LIMIT)


def _sigmoid(v):
    return 1.0 / (1.0 + jnp.exp(-v))


def _rms_mod(x, gain, scale, shift):
    ms = jnp.mean(x * x, axis=-1, keepdims=True)
    y = x * lax.rsqrt(ms + NORM_EPS)
    return (y * gain) * (1.0 + scale) + shift


def _split2(v):
    hi = v.astype(BF16)
    lo = (v - hi.astype(F32)).astype(BF16)
    return hi, lo


def _split3(v):
    hi = v.astype(BF16)
    r = v - hi.astype(F32)
    mid = r.astype(BF16)
    lo = (r - mid.astype(F32)).astype(BF16)
    return hi, mid, lo


def _dot(a, b):
    return jnp.dot(a, b, preferred_element_type=F32)


def _store_tile_rows(ref, row0, val):
    n = val.shape[0]
    for c in range(val.shape[1] // LANES):
        ref[pl.ds(row0 * SUBLANES + c, n, stride=SUBLANES), :] = val[:, c * LANES:(c + 1) * LANES]


def _load_tile_rows(ref, n):
    return [ref[pl.ds(c, n, stride=SUBLANES), :] for c in range(SUBLANES)]


def _mod_kernel(c_ref, w_ref, b_ref, o_ref):
    c = c_ref[...]
    cond = c * _sigmoid(c)
    o_ref[0] = _dot(cond.astype(BF16), w_ref[0].astype(BF16)) + b_ref[0]


def _adaln_mod(c, ada_w, ada_b):
    depth, d, n = ada_w.shape
    b = c.shape[0]
    tn = 1024
    c_pad = jnp.zeros((SUBLANES, d), F32).at[:b].set(c)
    out = pl.pallas_call(
        _mod_kernel,
        out_shape=jax.ShapeDtypeStruct((depth, SUBLANES, n), F32),
        grid=(depth, n // tn),
        in_specs=[
            pl.BlockSpec((SUBLANES, d), lambda i, j: (0, 0)),
            pl.BlockSpec((1, d, tn), lambda i, j: (i, 0, j)),
            pl.BlockSpec((1, 1, tn), lambda i, j: (i, 0, j)),
        ],
        out_specs=pl.BlockSpec((1, SUBLANES, tn), lambda i, j: (i, 0, j)),
        compiler_params=_cparams(("arbitrary", "arbitrary")),
        name="adaln_mod",
    )(c_pad, ada_w, ada_b.reshape(depth, 1, n))
    return out[:, :b]


def _t5_bucket_table():
    qi = np.arange(ATTN_BLOCK)[:, None]
    kj = np.arange(2 * ATTN_BLOCK)[None, :]
    dist = qi + ATTN_BLOCK - kj
    max_exact = N_BUCKETS // 2
    d = np.maximum(dist, 0)
    df = np.maximum(d, 1).astype(np.float64)
    large = max_exact + (np.log(df / max_exact) / math.log(MAX_DISTANCE / max_exact)
                         * (N_BUCKETS - max_exact)).astype(np.int32)
    large = np.minimum(large, N_BUCKETS - 1)
    bucket = np.where(d < max_exact, d, large)
    valid = (dist >= 0) & (dist < ATTN_BLOCK)
    return np.where(valid, bucket, -1).astype(np.int32)


def _bias_kernel(rb_ref, bkt_ref, o_ref):
    j = pl.program_id(0)
    gp = j // Q_PER_KV
    qi = j % Q_PER_KV
    bkt = bkt_ref[...]
    for a in range(2):
        h = (2 * gp + a) * Q_PER_KV + qi
        acc = jnp.full(bkt.shape, NEG_INF, F32)
        for b in range(N_BUCKETS):
            acc = jnp.where(bkt == b, rb_ref[b, h], acc)
        o_ref[0, a * ATTN_BLOCK:(a + 1) * ATTN_BLOCK, :] = acc


def _bias_table(rel_bias):
    bkt = jnp.asarray(_t5_bucket_table())
    n_slots = N_Q_HEADS // 2
    return pl.pallas_call(
        _bias_kernel,
        out_shape=jax.ShapeDtypeStruct((n_slots, 2 * ATTN_BLOCK, 2 * ATTN_BLOCK), F32),
        grid=(n_slots,),
        in_specs=[
            pl.BlockSpec(memory_space=pltpu.SMEM),
            pl.BlockSpec((ATTN_BLOCK, 2 * ATTN_BLOCK), lambda j: (0, 0)),
        ],
        out_specs=pl.BlockSpec((1, 2 * ATTN_BLOCK, 2 * ATTN_BLOCK), lambda j: (j, 0, 0)),
        compiler_params=_cparams(("arbitrary",)),
        name="t5_bias_table",
    )(rel_bias.astype(F32), bkt)


def _qkv_kernel(x_ref, ms_ref, gain_ref, w_ref, qg_ref, kg_ref, q_ref, k_ref, v_ref):
    tm = x_ref.shape[0]
    ms = ms_ref[0]
    hm = _rms_mod(x_ref[...], gain_ref[...], ms[1:2], ms[0:1]).astype(BF16)
    lo = lax.broadcasted_iota(jnp.int32, (tm, LANES), 1) < HEAD_DIM

    def head_norm(slab, g):
        sq = slab * slab
        s_lo = jnp.sum(jnp.where(lo, sq, 0.0), axis=-1, keepdims=True)
        s_hi = jnp.sum(jnp.where(lo, 0.0, sq), axis=-1, keepdims=True)
        msq = jnp.where(lo, s_lo, s_hi) * (1.0 / HEAD_DIM)
        return slab * lax.rsqrt(msq + NORM_EPS) * g

    nq = N_Q_HEADS * HEAD_DIM
    nk = N_KV_HEADS * HEAD_DIM
    q_all = _dot(hm, w_ref[:, :nq])
    for j in range(nq // LANES):
        sl = slice(j * LANES, (j + 1) * LANES)
        q_ref[:, sl] = (head_norm(q_all[:, sl], qg_ref[...]) * (HEAD_DIM ** -0.5)).astype(BF16)
    k_all = _dot(hm, w_ref[:, nq:nq + nk])
    for j in range(nk // LANES):
        sl = slice(j * LANES, (j + 1) * LANES)
        k_ref[:, sl] = head_norm(k_all[:, sl], kg_ref[...]).astype(BF16)
    v_ref[...] = _dot(hm, w_ref[:, nq + nk:]).astype(BF16)


def _qkv_proj(x2, ms, gain, w_qkv_all, li, qg2, kg2, seq, tm=512):
    t, d = x2.shape
    nq = N_Q_HEADS * HEAD_DIM
    nk = N_KV_HEADS * HEAD_DIM
    tpb = seq // tm
    return pl.pallas_call(
        _qkv_kernel,
        out_shape=(jax.ShapeDtypeStruct((t, nq), BF16),
                   jax.ShapeDtypeStruct((t, nk), BF16),
                   jax.ShapeDtypeStruct((t, nk), BF16)),
        grid=(t // tm,),
        in_specs=[
            pl.BlockSpec((tm, d), lambda i: (i, 0)),
            pl.BlockSpec((1, 2, d), lambda i: (i // tpb, 0, 0)),
            pl.BlockSpec((1, d), lambda i: (0, 0)),
            pl.BlockSpec((None, d, nq + 2 * nk), lambda i: (li, 0, 0)),
            pl.BlockSpec((1, LANES), lambda i: (0, 0)),
            pl.BlockSpec((1, LANES), lambda i: (0, 0)),
        ],
        out_specs=(pl.BlockSpec((tm, nq), lambda i: (i, 0)),
                   pl.BlockSpec((tm, nk), lambda i: (i, 0)),
                   pl.BlockSpec((tm, nk), lambda i: (i, 0))),
        compiler_params=_cparams(("arbitrary",)),
        name="attn_qkv",
    )(x2, ms, gain, w_qkv_all, qg2, kg2)


def _attn_kernel(sink_ref, q_ref, kc_ref, kp_ref, vc_ref, vp_ref, bias_ref, o_ref):
    blk = ATTN_BLOCK
    first = pl.program_id(1) == 0
    lane = lax.broadcasted_iota(jnp.int32, (blk, LANES), 1)
    lo = lane < HEAD_DIM
    mask_lo = jnp.where(lo, 1.0, 0.0).astype(BF16)
    mask_hi = jnp.where(lo, 0.0, 1.0).astype(BF16)
    row = lax.broadcasted_iota(jnp.int32, (2 * blk, 1), 0)
    nt = (((1,), (1,)), ((), ()))
    for gp in range(N_KV_HEADS // 2):
        ksl = slice(gp * LANES, (gp + 1) * LANES)
        kc = kc_ref[:, ksl]
        kp = kp_ref[:, ksl]
        vc = vc_ref[:, ksl]
        vp = vp_ref[:, ksl]
        for qi in range(Q_PER_KV):
            j = gp * Q_PER_KV + qi
            sl = slice(j * LANES, (j + 1) * LANES)
            qs = q_ref[:, sl]
            q2 = jnp.concatenate([qs * mask_lo, qs * mask_hi], axis=0)
            s_p = lax.dot_general(q2, kp, nt, preferred_element_type=F32)
            s_c = lax.dot_general(q2, kc, nt, preferred_element_type=F32)
            bias = bias_ref[j]
            s_p = jnp.where(first, NEG_INF, s_p + bias[:, :blk])
            s_c = s_c + bias[:, blk:]
            h_a = (2 * gp) * Q_PER_KV + qi
            h_b = (2 * gp + 1) * Q_PER_KV + qi
            sink = jnp.where(row < blk, sink_ref[h_a], sink_ref[h_b])
            m = jnp.maximum(jnp.max(jnp.maximum(s_p, s_c), axis=-1, keepdims=True), sink)
            p_p = jnp.exp(s_p - m)
            p_c = jnp.exp(s_c - m)
            den = jnp.sum(p_p + p_c, axis=-1, keepdims=True) + jnp.exp(sink - m)
            pv = _dot(p_p.astype(BF16), vp) + _dot(p_c.astype(BF16), vc)
            pv = pv / den
            o_ref[:, sl] = jnp.where(lo, pv[:blk], pv[blk:]).astype(BF16)


def _attention(q, k, v, bias, sinks, batch, seq):
    t, nq = q.shape
    nk = k.shape[1]
    blk = ATTN_BLOCK
    nb = seq // blk
    cur = lambda b, n, *_: (b * nb + n, 0)
    prev = lambda b, n, *_: (b * nb + jnp.maximum(n - 1, 0), 0)
    return pl.pallas_call(
        _attn_kernel,
        out_shape=jax.ShapeDtypeStruct((t, nq), BF16),
        grid=(batch, nb),
        in_specs=[
            pl.BlockSpec(memory_space=pltpu.SMEM),
            pl.BlockSpec((blk, nq), cur),
            pl.BlockSpec((blk, nk), cur),
            pl.BlockSpec((blk, nk), prev),
            pl.BlockSpec((blk, nk), cur),
            pl.BlockSpec((blk, nk), prev),
            pl.BlockSpec(bias.shape, lambda b, n: (0, 0, 0)),
        ],
        out_specs=pl.BlockSpec((blk, nq), cur),
        compiler_params=_cparams(("arbitrary", "arbitrary")),
        name="swa_attention",
    )(sinks.astype(F32), q, k, k, v, v, bias)


def _swiglu_step(hf, wg_ref, wu_ref, wd_ref, acc_s):
    g = _dot(hf, wg_ref[...])
    u = _dot(hf, wu_ref[...])
    a = (g * _sigmoid(g) * u).astype(BF16)
    acc_s[...] += _dot(a, wd_ref[...])


def _ffn_dense_kernel(x_ref, o_ref, wo_ref, mv_ref, gain_ref, wg_ref, wu_ref, wd_ref,
                      out_ref, hf_s, acc_s):
    j = pl.program_id(1)

    @pl.when(j == 0)
    def _():
        mv = mv_ref[0]
        x1 = x_ref[...] + mv[0:1] * _dot(o_ref[...], wo_ref[...])
        out_ref[...] = x1
        hf_s[...] = _rms_mod(x1, gain_ref[...], mv[2:3], mv[1:2]).astype(BF16)
        acc_s[...] = jnp.zeros_like(acc_s)

    _swiglu_step(hf_s[...], wg_ref, wu_ref, wd_ref, acc_s)

    @pl.when(j == pl.num_programs(1) - 1)
    def _():
        out_ref[...] = out_ref[...] + mv_ref[0][3:4] * acc_s[...]


def _ffn_dense(x2, o, wo_all, mv, gain, w_gu_all, w_down_all, li, seq, tm=1024, tf=512):
    t, d = x2.shape
    f = w_down_all.shape[1]
    nj = f // tf
    tpb = seq // tm
    return pl.pallas_call(
        _ffn_dense_kernel,
        out_shape=jax.ShapeDtypeStruct((t, d), F32),
        grid=(t // tm, nj),
        in_specs=[
            pl.BlockSpec((tm, d), lambda i, j: (i, 0)),
            pl.BlockSpec((tm, o.shape[1]), lambda i, j: (i, 0)),
            pl.BlockSpec((None,) + wo_all.shape[1:], lambda i, j: (li, 0, 0)),
            pl.BlockSpec((1, 4, d), lambda i, j: (i // tpb, 0, 0)),
            pl.BlockSpec((1, d), lambda i, j: (0, 0)),
            pl.BlockSpec((None, d, tf), lambda i, j: (li, 0, j)),
            pl.BlockSpec((None, d, tf), lambda i, j: (li, 0, nj + j)),
            pl.BlockSpec((None, tf, d), lambda i, j: (li, j, 0)),
        ],
        out_specs=pl.BlockSpec((tm, d), lambda i, j: (i, 0)),
        scratch_shapes=[pltpu.VMEM((tm, d), BF16), pltpu.VMEM((tm, d), F32)],
        compiler_params=_cparams(("arbitrary", "arbitrary")),
        name="wo_ffn_dense",
    )(x2, o, wo_all, mv, gain, w_gu_all, w_gu_all, w_down_all)


def _ffn_moe_kernel(te_ref, xs_ref, wg_ref, wu_ref, wd_ref, out_ref, hf_s, acc_s):
    i = pl.program_id(0)
    j = pl.program_id(1)
    used = i < te_ref[pl.num_programs(0)]

    tm = hf_s.shape[0]
    last = pl.num_programs(1) - 1

    def partial_product():
        hf = hf_s[...]
        g = _dot(hf, wg_ref[...].astype(BF16))
        u = _dot(hf, wu_ref[...].astype(BF16))
        a = (g * _sigmoid(g) * u).astype(BF16)
        return _dot(a, wd_ref[...].astype(BF16))

    @pl.when(used & (j == 0))
    def _():
        for c, part in enumerate(_load_tile_rows(xs_ref, tm)):
            hf_s[:, c * LANES:(c + 1) * LANES] = part.astype(BF16)
        acc_s[...] = partial_product()

    @pl.when(used & (j > 0) & (j < last))
    def _():
        acc_s[...] += partial_product()

    @pl.when(used & (j == last))
    def _():
        _store_tile_rows(out_ref, 0, acc_s[...] + partial_product())

    @pl.when(jnp.logical_not(used) & (j == last))
    def _():
        out_ref[...] = jnp.zeros_like(out_ref)


def _ffn_moe(xs, tile_expert, w_gu_all, w_down_all, li, tm, tf=512):
    r = xs.shape[0] // SUBLANES
    d = w_gu_all.shape[2]
    f = w_down_all.shape[2]
    nj = f // tf
    nt = r // tm

    def jj(i, j, te):
        return jnp.where(i < te[nt], j, 0)

    return pl.pallas_call(
        _ffn_moe_kernel,
        out_shape=jax.ShapeDtypeStruct((r * SUBLANES, LANES), F32),
        grid_spec=pltpu.PrefetchScalarGridSpec(
            num_scalar_prefetch=1,
            grid=(nt, nj),
            in_specs=[
                pl.BlockSpec((tm * SUBLANES, LANES), lambda i, j, te: (i, 0)),
                pl.BlockSpec((None, None, d, tf), lambda i, j, te: (li, te[i], 0, jj(i, j, te))),
                pl.BlockSpec((None, None, d, tf), lambda i, j, te: (li, te[i], 0, nj + jj(i, j, te))),
                pl.BlockSpec((None, None, tf, d), lambda i, j, te: (li, te[i], jj(i, j, te), 0)),
            ],
            out_specs=pl.BlockSpec((tm * SUBLANES, LANES), lambda i, j, te: (i, 0)),
            scratch_shapes=[pltpu.VMEM((tm, d), BF16), pltpu.VMEM((tm, d), F32)],
        ),
        compiler_params=_cparams(("arbitrary", "arbitrary")),
        name="moe_expert_ffn",
    )(tile_expert, xs, w_gu_all, w_gu_all, w_down_all)


def _inproj_kernel(x_ref, ms_ref, gain_ref, wz_ref, wx_ref, wdt_ref, cw_ref, cb_ref,
                   z_ref, xbc_ref, dt_ref, hm_s, carry_s, *, tiles_per_seq):
    i = pl.program_id(0)
    j = pl.program_id(1)
    tm = z_ref.shape[0]

    @pl.when((i == 0) & (j == 0))
    def _():
        carry_s[...] = jnp.zeros_like(carry_s)

    @pl.when(j == 0)
    def _():
        ms = ms_ref[0]
        hm = _rms_mod(x_ref[...], gain_ref[...], ms[1:2], ms[0:1]).astype(BF16)
        hm_s[...] = hm
        dt_ref[...] = _dot(hm, wdt_ref[...])

    hm = hm_s[...]
    strip = 2 * LANES
    rowi = lax.broadcasted_iota(jnp.int32, (SUBLANES, strip), 0)
    first = i % tiles_per_seq == 0
    n_z_strips = z_ref.shape[1] // strip
    for k, c0 in enumerate(range(0, xbc_ref.shape[1], strip)):
        cs = slice(c0, c0 + strip)
        res = _dot(hm, wx_ref[:, cs])
        if k < n_z_strips:
            zs = slice(k * strip, (k + 1) * strip)
            z_ref[:, zs] = _dot(hm, wz_ref[:, zs]).astype(BF16)
        history = jnp.where(first, 0.0, carry_s[j, :, cs])
        acc = res * cw_ref[CONV_WIDTH - 1:CONV_WIDTH, cs] + cb_ref[:, cs]
        for s in range(1, CONV_WIDTH):
            down = pltpu.roll(res, s, 0)
            head = jnp.where(rowi < s, pltpu.roll(history, s, 0), down[:SUBLANES])
            shifted = jnp.concatenate([head, down[SUBLANES:]], axis=0)
            acc = acc + shifted * cw_ref[CONV_WIDTH - 1 - s:CONV_WIDTH - s, cs]
        carry_s[j, :, cs] = res[tm - SUBLANES:]
        xbc_ref[:, cs] = (acc * _sigmoid(acc)).astype(BF16)


def _ssm_in_proj(x2, ms, gain, w_z_all, w_xbc_all, w_dt_all, li, conv_w, conv_b, seq, tm=1024, nj=2):
    t, d = x2.shape
    nz = w_z_all.shape[2] // nj
    nx = w_xbc_all.shape[2] // nj
    tpb = seq // tm
    return pl.pallas_call(
        functools.partial(_inproj_kernel, tiles_per_seq=tpb),
        out_shape=(jax.ShapeDtypeStruct((t, nz * nj), BF16), jax.ShapeDtypeStruct((t, nx * nj), BF16),
                   jax.ShapeDtypeStruct((t, LANES), F32)),
        grid=(t // tm, nj),
        in_specs=[
            pl.BlockSpec((tm, d), lambda i, j: (i, 0)),
            pl.BlockSpec((1, 2, d), lambda i, j: (i // tpb, 0, 0)),
            pl.BlockSpec((1, d), lambda i, j: (0, 0)),
            pl.BlockSpec((None, d, nz), lambda i, j: (li, 0, j)),
            pl.BlockSpec((None, d, nx), lambda i, j: (li, 0, j)),
            pl.BlockSpec((None, d, LANES), lambda i, j: (li, 0, 0)),
            pl.BlockSpec((CONV_WIDTH, nx), lambda i, j: (0, j)),
            pl.BlockSpec((1, nx), lambda i, j: (0, j)),
        ],
        out_specs=(pl.BlockSpec((tm, nz), lambda i, j: (i, j)),
                   pl.BlockSpec((tm, nx), lambda i, j: (i, j)),
                   pl.BlockSpec((tm, LANES), lambda i, j: (i, 0))),
        scratch_shapes=[pltpu.VMEM((tm, d), BF16), pltpu.VMEM((nj, SUBLANES, nx), F32)],
        compiler_params=_cparams(("arbitrary", "arbitrary")),
        name="ssm_in_proj",
    )(x2, ms, gain, w_z_all, w_xbc_all, w_dt_all, conv_w.astype(F32), conv_b.astype(F32)[None])


def _ssd_kernel(z_ref, xs_ref, bc_ref, dt_ref, dtb_ref, alog_ref, dskip_ref, nw_ref, r64_ref,
                y_ref, state_s, ybuf_s):
    L = SSM_CHUNK
    gw = D_INNER // N_SSM_GROUPS
    hpg = N_SSM_HEADS // N_SSM_GROUPS

    @pl.when(pl.program_id(1) == 0)
    def _():
        state_s[...] = jnp.zeros_like(state_s)

    dtv = jax.nn.softplus(dt_ref[...] + dtb_ref[...])
    a = -jnp.exp(alog_ref[...])
    dta = dtv * a
    ri = lax.broadcasted_iota(jnp.int32, (L, L), 0)
    ci = lax.broadcasted_iota(jnp.int32, (L, L), 1)
    causal = ri >= ci
    tri = jnp.where(causal, 1.0, 0.0).astype(BF16)
    p0, p1, p2 = _split3(dta)
    acs = _dot(tri, p0) + _dot(tri, p1) + _dot(tri, p2)
    acs_t = acs.T
    eacs = jnp.exp(acs)
    dte = jnp.exp(acs[L - 1:L, :] - acs)
    stacked = jnp.concatenate([dtv, eacs, dte], axis=0)
    s_hi, s_lo = _split2(stacked)
    expd = _dot(s_hi, r64_ref[...]) + _dot(s_lo, r64_ref[...])

    lo = lax.broadcasted_iota(jnp.int32, (L, LANES), 1) < (LANES // 2)
    for g in range(N_SSM_GROUPS):
        gs = slice(g * gw, (g + 1) * gw)
        b_g = bc_ref[:, g * D_STATE:(g + 1) * D_STATE]
        c_b = bc_ref[:, (N_SSM_GROUPS + g) * D_STATE:(N_SSM_GROUPS + g + 1) * D_STATE]
        b_t = b_g.astype(F32).T.astype(BF16)
        cb = _dot(c_b, b_t)
        xs_g = xs_ref[:, gs].astype(F32)
        xd = xs_g * expd[0:L, gs]
        xd_b = xd.astype(BF16)
        st_prev = state_s[g]
        y_off = _dot(c_b, st_prev.astype(BF16)) * expd[L:2 * L, gs]
        xdw = (xd * expd[2 * L:3 * L, gs]).astype(BF16)
        state_s[g] = expd[2 * L - 1:2 * L, gs] * st_prev + _dot(b_t, xdw)
        for jp in range(hpg // 2):
            ps = slice(jp * LANES, (jp + 1) * LANES)
            slab = xd_b[:, ps]
            halves = []
            for half in range(2):
                hh = g * hpg + 2 * jp + half
                diff = acs[:, hh:hh + 1] - acs_t[hh:hh + 1, :]
                dec = jnp.exp(jnp.where(causal, diff, NEG_INF))
                halves.append(_dot((cb * dec).astype(BF16), slab))
            y = jnp.where(lo, halves[0], halves[1]) + y_off[:, ps]
            cols = slice(g * gw + jp * LANES, g * gw + (jp + 1) * LANES)
            ybuf_s[:, cols] = y + dskip_ref[:, cols] * xs_g[:, ps]

    z = z_ref[...].astype(F32)
    yg = ybuf_s[...] * (z * _sigmoid(z))
    ms = jnp.mean(yg * yg, axis=-1, keepdims=True)
    y_ref[...] = (yg * lax.rsqrt(ms + NORM_EPS) * nw_ref[...]).astype(BF16)


def _ssd(z, xbc, dt, dt_bias, a_log, d_skip, norm_w, batch, seq):
    t = z.shape[0]
    L = SSM_CHUNK
    nc = seq // L
    di, dbc = D_INNER, D_BC
    pad = LANES - N_SSM_HEADS
    dtb = jnp.pad(dt_bias.astype(F32), (0, pad))[None]
    alog = jnp.pad(a_log.astype(F32), (0, pad))[None]
    dskip = jnp.repeat(d_skip.astype(F32), di // N_SSM_HEADS)[None]
    nw = norm_w.astype(F32)[None]
    r64 = jnp.asarray((np.arange(LANES)[:, None] == (np.arange(di)[None, :] // (di // N_SSM_HEADS)))
                      .astype(np.float32)).astype(BF16)
    row = lambda b, c: (b * nc + c, 0)
    const = lambda b, c: (0, 0)
    return pl.pallas_call(
        _ssd_kernel,
        out_shape=jax.ShapeDtypeStruct((t, di), BF16),
        grid=(batch, nc),
        in_specs=[
            pl.BlockSpec((L, di), row),
            pl.BlockSpec((L, di), row),
            pl.BlockSpec((L, dbc), lambda b, c: (b * nc + c, di // dbc)),
            pl.BlockSpec((L, LANES), row),
            pl.BlockSpec((1, LANES), const),
            pl.BlockSpec((1, LANES), const),
            pl.BlockSpec((1, di), const),
            pl.BlockSpec((1, di), const),
            pl.BlockSpec((LANES, di), const),
        ],
        out_specs=pl.BlockSpec((L, di), row),
        scratch_shapes=[
            pltpu.VMEM((N_SSM_GROUPS, D_STATE, di // N_SSM_GROUPS), F32),
            pltpu.VMEM((L, di), F32),
        ],
        compiler_params=_cparams(("arbitrary", "arbitrary")),
        name="ssd_scan",
    )(z, xbc, xbc, dt, dtb, alog, dskip, nw, r64)


ROUTER_SUB = 512


def _router_kernel(x_ref, y_ref, wout_ref, mv_ref, gain_ref, wr_ref,
                   x1_ref, hf_ref, rinfo_ref, cnt_ref, carry_s):
    tm = x_ref.shape[0]
    sub = ROUTER_SUB

    @pl.when(pl.program_id(0) == 0)
    def _():
        carry_s[...] = jnp.zeros_like(carry_s)

    mv = mv_ref[0]
    lane = lax.broadcasted_iota(jnp.int32, (sub, LANES), 1)
    lanef = lane.astype(F32)
    ri = lax.broadcasted_iota(jnp.int32, (sub, sub), 0)
    ci = lax.broadcasted_iota(jnp.int32, (sub, sub), 1)
    before = jnp.where(ri > ci, 1.0, 0.0).astype(BF16)
    carry = carry_s[...]
    for h in range(tm // sub):
        rs = slice(h * sub, (h + 1) * sub)
        x1 = x_ref[rs, :] + mv[0:1] * _dot(y_ref[rs, :], wout_ref[...])
        x1_ref[rs, :] = x1
        hf = _rms_mod(x1, gain_ref[...], mv[2:3], mv[1:2])
        _store_tile_rows(hf_ref, h * sub, hf)
        h_hi, h_lo = _split2(hf)
        hw = _dot(h_hi, wr_ref[...])
        logits = hw[:, :LANES] + hw[:, LANES:] + _dot(h_lo, wr_ref[:, :LANES])
        lg = jnp.where(lane < N_EXPERTS, logits, NEG_INF)
        v0 = jnp.max(lg, axis=-1, keepdims=True)
        i0 = jnp.min(jnp.where(lg == v0, lanef, float(LANES)), axis=-1, keepdims=True)
        lg1 = jnp.where(lanef == i0, NEG_INF, lg)
        v1 = jnp.max(lg1, axis=-1, keepdims=True)
        i1 = jnp.min(jnp.where(lg1 == v1, lanef, float(LANES)), axis=-1, keepdims=True)
        e1 = jnp.exp(v1 - v0)
        w0 = 1.0 / (1.0 + e1)
        w1 = e1 / (1.0 + e1)
        oh0 = jnp.where(lanef == i0, 1.0, 0.0)
        oh1 = jnp.where(lanef == i1, 1.0, 0.0)
        pre = _dot(before, jnp.concatenate([oh0, oh1], axis=1).astype(BF16))
        cnt0 = jnp.sum(oh0, axis=0, keepdims=True)
        cnt1 = jnp.sum(oh1, axis=0, keepdims=True)
        rank0 = jnp.sum(oh0 * (pre[:, :LANES] + carry), axis=-1, keepdims=True)
        rank1 = jnp.sum(oh1 * (pre[:, LANES:] + carry + cnt0), axis=-1, keepdims=True)
        carry = carry + cnt0 + cnt1
        rinfo_ref[rs, :] = jnp.where(lane == 0, i0, jnp.where(lane == 1, i1, jnp.where(
            lane == 2, rank0, jnp.where(lane == 3, rank1, jnp.where(
                lane == 4, w0, jnp.where(lane == 5, w1, 0.0))))))
    carry_s[...] = carry
    cnt_ref[...] = carry


def _router(x2, y, wout_all, li, mv, gain, wr, seq, tm=1024):
    t, d = x2.shape
    tpb = seq // tm
    return pl.pallas_call(
        _router_kernel,
        out_shape=(jax.ShapeDtypeStruct((t, d), F32),
                   jax.ShapeDtypeStruct((t * SUBLANES, LANES), F32),
                   jax.ShapeDtypeStruct((t, LANES), F32), jax.ShapeDtypeStruct((1, LANES), F32)),
        grid=(t // tm,),
        in_specs=[
            pl.BlockSpec((tm, d), lambda i: (i, 0)),
            pl.BlockSpec((tm, y.shape[1]), lambda i: (i, 0)),
            pl.BlockSpec((None,) + wout_all.shape[1:], lambda i: (li, 0, 0)),
            pl.BlockSpec((1, 3, d), lambda i: (i // tpb, 0, 0)),
            pl.BlockSpec((1, d), lambda i: (0, 0)),
            pl.BlockSpec((d, 2 * LANES), lambda i: (0, 0)),
        ],
        out_specs=(pl.BlockSpec((tm, d), lambda i: (i, 0)),
                   pl.BlockSpec((tm * SUBLANES, LANES), lambda i: (i, 0)),
                   pl.BlockSpec((tm, LANES), lambda i: (i, 0)),
                   pl.BlockSpec((1, LANES), lambda i: (0, 0))),
        scratch_shapes=[pltpu.VMEM((1, LANES), F32)],
        compiler_params=_cparams(("arbitrary",)),
        name="ssm_out_router",
    )(x2, y, wout_all, mv, gain, wr)


DMA_UNROLL = 8


def _row_copy(src, src_row, dst, dst_row, sem):
    return pltpu.make_async_copy(
        src.at[pl.ds(pl.multiple_of(src_row * SUBLANES, SUBLANES), SUBLANES)],
        dst.at[pl.ds(pl.multiple_of(dst_row * SUBLANES, SUBLANES), SUBLANES)], sem)


def _dispatch_kernel(pos_ref, hf_ref, sorted_ref, zero_s, sem, fill_sem):
    i = pl.program_id(0)
    tm = hf_ref.shape[0] // SUBLANES
    base = i * (TOP_K * tm)

    def issue(blk, carry):
        for u in range(DMA_UNROLL):
            t = blk * DMA_UNROLL + u
            for k in range(TOP_K):
                _row_copy(hf_ref, t, sorted_ref, pos_ref[base + TOP_K * t + k], sem).start(priority=k)
        return carry

    lax.fori_loop(0, tm // DMA_UNROLL, issue, 0)

    @pl.when(i == pl.num_programs(0) - 1)
    def _():
        zero_s[...] = jnp.zeros_like(zero_s)
        ranges = pl.num_programs(0) * (TOP_K * tm)
        for e in range(N_EXPERTS + 1):
            lo = pos_ref[ranges + 2 * e]
            hi = pos_ref[ranges + 2 * e + 1]

            def fill(r, carry):
                _row_copy(zero_s, 0, sorted_ref, r, fill_sem).start()
                return carry

            def drain(r, carry):
                _row_copy(zero_s, 0, sorted_ref, r, fill_sem).wait()
                return carry

            lax.fori_loop(lo, hi, fill, 0)
            lax.fori_loop(lo, hi, drain, 0)

    for k in range(TOP_K):
        pltpu.make_async_copy(hf_ref, sorted_ref.at[pl.ds(0, tm * SUBLANES)], sem).wait()


def _dispatch(pos_and_ranges, hf, n_sorted, tm=1024):
    t = hf.shape[0] // SUBLANES
    return pl.pallas_call(
        _dispatch_kernel,
        out_shape=jax.ShapeDtypeStruct((n_sorted * SUBLANES, LANES), hf.dtype),
        grid_spec=pltpu.PrefetchScalarGridSpec(
            num_scalar_prefetch=1,
            grid=(t // tm,),
            in_specs=[pl.BlockSpec((tm * SUBLANES, LANES), lambda i, pos: (i, 0))],
            out_specs=pl.BlockSpec(memory_space=pl.ANY),
            scratch_shapes=[pltpu.VMEM((SUBLANES, LANES), hf.dtype), pltpu.SemaphoreType.DMA,
                            pltpu.SemaphoreType.DMA],
        ),
        compiler_params=_cparams(("arbitrary",)),
        name="moe_dispatch",
    )(pos_and_ranges, hf)


def _combine_kernel(pos_ref, x1_ref, rinfo_ref, g2_ref, ys_ref, out_ref, buf, sem):
    tm, d = x1_ref.shape
    i = pl.program_id(0)
    slot = i % 2

    def issue_tile(tile, dst_slot):
        base = tile * (TOP_K * tm)

        def issue(blk, carry):
            for u in range(DMA_UNROLL):
                t = blk * DMA_UNROLL + u
                for k in range(TOP_K):
                    _row_copy(ys_ref, pos_ref[base + TOP_K * t + k], buf.at[dst_slot, k], t,
                              sem.at[dst_slot]).start(priority=k)
            return carry

        lax.fori_loop(0, tm // DMA_UNROLL, issue, 0)

    @pl.when(i == 0)
    def _():
        issue_tile(0, 0)

    @pl.when(i + 1 < pl.num_programs(0))
    def _():
        issue_tile(i + 1, 1 - slot)

    for k in range(TOP_K):
        pltpu.make_async_copy(ys_ref.at[pl.ds(0, tm * SUBLANES)], buf.at[slot, k], sem.at[slot]).wait()
    rinfo = rinfo_ref[...]
    w0 = rinfo[:, 4:5]
    w1 = rinfo[:, 5:6]
    g2 = g2_ref[0]
    for c in range(d // LANES):
        cs = slice(c * LANES, (c + 1) * LANES)
        ffn = (w0 * buf[slot, 0, pl.ds(c, tm, stride=SUBLANES), :]
               + w1 * buf[slot, 1, pl.ds(c, tm, stride=SUBLANES), :])
        out_ref[:, cs] = x1_ref[:, cs] + g2[:, cs] * ffn


def _combine(pos_flat, x1, rinfo, g2, ys, seq, tm=512):
    t, d = x1.shape
    tpb = seq // tm
    return pl.pallas_call(
        _combine_kernel,
        out_shape=jax.ShapeDtypeStruct((t, d), F32),
        grid_spec=pltpu.PrefetchScalarGridSpec(
            num_scalar_prefetch=1,
            grid=(t // tm,),
            in_specs=[pl.BlockSpec((tm, d), lambda i, pos: (i, 0)),
                      pl.BlockSpec((tm, LANES), lambda i, pos: (i, 0)),
                      pl.BlockSpec((1, 1, d), lambda i, pos: (i // tpb, 0, 0)),
                      pl.BlockSpec(memory_space=pl.ANY)],
            out_specs=pl.BlockSpec((tm, d), lambda i, pos: (i, 0)),
            scratch_shapes=[pltpu.VMEM((2, TOP_K, tm * SUBLANES, LANES), F32),
                            pltpu.SemaphoreType.DMA((2,))],
        ),
        compiler_params=_cparams(("arbitrary",)),
        name="moe_combine",
    )(pos_flat, x1, rinfo, g2, ys)


def _attn_weights(attn_w_qkv, attn_w_o):
    n_att, d, _ = attn_w_qkv.shape
    nq = N_Q_HEADS * HEAD_DIM
    head_grid = (N_KV_HEADS // 2, 2, Q_PER_KV, HEAD_DIM)
    w_q = attn_w_qkv[:, :, :nq].reshape((n_att, d) + head_grid)
    w_q = jnp.swapaxes(w_q, 3, 4).reshape(n_att, d, nq)
    w_qkv_all = jnp.concatenate([w_q, attn_w_qkv[:, :, nq:]], axis=2).astype(BF16)
    w_o = attn_w_o.reshape((n_att,) + head_grid + (attn_w_o.shape[2],))
    w_o_all = jnp.swapaxes(w_o, 2, 3).reshape(n_att, nq, attn_w_o.shape[2]).astype(BF16)
    return w_qkv_all, w_o_all


def _attn_layer(x2, mod_i, gain_i, bias, w_qkv_all, w_o_all, w_gu_all, w_down_all, li,
                q_gain, k_gain, sinks, batch, seq):
    d = D_MODEL
    sh1, sc1, g1, sh2, sc2, g2 = [mod_i[:, k * d:(k + 1) * d] for k in range(6)]
    qg2 = jnp.tile(q_gain.astype(F32), 2)[None]
    kg2 = jnp.tile(k_gain.astype(F32), 2)[None]
    q, k, v = _qkv_proj(x2, jnp.stack([sh1, sc1], axis=1), gain_i[0:1], w_qkv_all, li, qg2, kg2, seq)
    o = _attention(q, k, v, bias, sinks, batch, seq)
    mv = jnp.stack([g1, sh2, sc2, g2], axis=1)
    return _ffn_dense(x2, o, w_o_all, mv, gain_i[1:2], w_gu_all, w_down_all, li, seq)


def _moe_slots(rinfo, cnt, n_tokens, moe_tm):
    counts = cnt[0, :N_EXPERTS].astype(jnp.int32)
    padded = ((counts + moe_tm - 1) // moe_tm) * moe_tm
    ends = jnp.cumsum(padded)
    base = ends - padded
    eidx = rinfo[:, 0:TOP_K].astype(jnp.int32)
    rank = rinfo[:, 2:2 + TOP_K].astype(jnp.int32)
    pos = jnp.sum(jnp.where(eidx[..., None] == jnp.arange(N_EXPERTS), base, 0), axis=-1) + rank
    n_tiles = (TOP_K * n_tokens) // moe_tm + N_EXPERTS
    tile_start = jnp.arange(n_tiles, dtype=jnp.int32) * moe_tm
    tile_expert = jnp.minimum(jnp.sum(tile_start[:, None] >= ends[None, :], axis=-1),
                              N_EXPERTS - 1).astype(jnp.int32)
    n_used = ends[-1] // moe_tm
    tile_expert = jnp.where(tile_start < ends[-1], tile_expert, tile_expert[n_used - 1])
    unused = jnp.stack([jnp.append(base + counts, ends[-1]),
                        jnp.append(ends, n_tiles * moe_tm)], axis=1).astype(jnp.int32)
    return (pos.reshape(-1), unused.reshape(-1),
            jnp.concatenate([tile_expert, n_used[None].astype(jnp.int32)]))


def _ssm_layer(x2, mod_i, gain_i, w_z_all, w_xbc_all, w_dt_all, w_out_all, w_gu_all, w_down_all, li,
               conv_w, conv_b, dt_bias, a_log, d_skip, norm_w, w_router, batch, seq, moe_tm):
    d = D_MODEL
    sh1, sc1, g1, sh2, sc2, g2 = [mod_i[:, k * d:(k + 1) * d] for k in range(6)]
    z, xbc, dt = _ssm_in_proj(x2, jnp.stack([sh1, sc1], axis=1), gain_i[0:1], w_z_all, w_xbc_all,
                              w_dt_all, li, conv_w, conv_b, seq)
    y = _ssd(z, xbc, dt, dt_bias, a_log, d_skip, norm_w, batch, seq)

    wr = jnp.pad(w_router.astype(F32), ((0, 0), (0, LANES - N_EXPERTS)))
    wr_hi = wr.astype(BF16)
    wr_lo = (wr - wr_hi.astype(F32)).astype(BF16)
    x1, hf, rinfo, cnt = _router(x2, y, w_out_all, li, jnp.stack([g1, sh2, sc2], axis=1),
                                 gain_i[1:2], jnp.concatenate([wr_hi, wr_lo], axis=1), seq)
    n_tokens = x2.shape[0]
    pos_flat, unused, tile_expert = _moe_slots(rinfo, cnt, n_tokens, moe_tm)
    xs = _dispatch(jnp.concatenate([pos_flat, unused]), hf, TOP_K * n_tokens + N_EXPERTS * moe_tm)
    ys = _ffn_moe(xs, tile_expert, w_gu_all, w_down_all, li, moe_tm)
    return _combine(pos_flat, x1, rinfo, g2[:, None, :], ys, seq)


def kernel(x, c, ada_w, ada_b, norm_gain, rel_bias, attn_w_qkv, attn_q_gain, attn_k_gain, attn_sinks, attn_w_o, ssm_w_in, ssm_conv_w, ssm_conv_b, ssm_dt_bias, ssm_a_log, ssm_d, ssm_norm_w, ssm_w_out, ffn_w_gu, ffn_w_down, moe_w_router, moe_w_gu, moe_w_down):
    batch, seq, d = x.shape
    moe_tm = 1024
    mod = _adaln_mod(c.astype(F32), ada_w.astype(F32), ada_b.astype(F32))
    bias = _bias_table(rel_bias)

    n_zx = 2 * D_INNER + D_BC
    w_qkv_all, w_o_all = _attn_weights(attn_w_qkv, attn_w_o)
    ffn_gu_all = ffn_w_gu.astype(BF16)
    ffn_down_all = ffn_w_down.astype(BF16)
    w_z_all = ssm_w_in[:, :, :D_INNER].astype(BF16)
    w_xbc_all = ssm_w_in[:, :, D_INNER:n_zx].astype(BF16)
    w_dt_all = jnp.pad(ssm_w_in[:, :, n_zx:], ((0, 0), (0, 0), (0, LANES - N_SSM_HEADS))).astype(BF16)
    w_out_all = ssm_w_out.astype(BF16)
    moe_gu_all = moe_w_gu.astype(F32)
    moe_down_all = moe_w_down.astype(F32)

    x2 = x.reshape(batch * seq, d).astype(F32)
    for i in range(DEPTH):
        j = i // 2
        gain_i = norm_gain[i].astype(F32)
        if i % 2 == 0:
            x2 = _attn_layer(x2, mod[i], gain_i, bias, w_qkv_all, w_o_all, ffn_gu_all, ffn_down_all,
                             j, attn_q_gain[j], attn_k_gain[j], attn_sinks[j], batch, seq)
        else:
            x2 = _ssm_layer(x2, mod[i], gain_i, w_z_all, w_xbc_all, w_dt_all, w_out_all, moe_gu_all,
                            moe_down_all, j, ssm_conv_w[j], ssm_conv_b[j], ssm_dt_bias[j],
                            ssm_a_log[j], ssm_d[j], ssm_norm_w[j], moe_w_router[j], batch, seq, moe_tm)
    return x2.reshape(batch, seq, d).astype(x.dtype)
```

```python
import functools
import math

import numpy as np
import jax
import jax.numpy as jnp
from jax import lax
from jax.experimental import pallas as pl
from jax.experimental.pallas import tpu as pltpu

F32 = jnp.float32
BF16 = jnp.bfloat16

D_MODEL = 1024
DEPTH = 4
N_Q_HEADS = 16
N_KV_HEADS = 4
HEAD_DIM = 64
Q_PER_KV = 4
ATTN_BLOCK = 128
N_BUCKETS = 32
MAX_DISTANCE = 128
D_INNER = 2048
N_SSM_HEADS = 32
N_SSM_GROUPS = 4
D_STATE = 128
CONV_WIDTH = 4
SSM_CHUNK = 128
D_BC = 2 * N_SSM_GROUPS * D_STATE
D_FF = 3584
N_EXPERTS = 8
TOP_K = 2
NORM_EPS = 1e-6

LANES = 128
SUBLANES = 8
VMEM_LIMIT = 56 * 1024 * 1024
NEG_INF = float("-inf")


def _cparams(sem):
    return pltpu.CompilerParams(dimension_semantics=sem, vmem_limit_bytes=VMEM_LIMIT)


def _sigmoid(v):
    return 1.0 / (1.0 + jnp.exp(-v))


def _rms_mod(x, gain, scale, shift):
    ms = jnp.mean(x * x, axis=-1, keepdims=True)
    y = x * lax.rsqrt(ms + NORM_EPS)
    return (y * gain) * (1.0 + scale) + shift


def _split2(v):
    hi = v.astype(BF16)
    lo = (v - hi.astype(F32)).astype(BF16)
    return hi, lo


def _split3(v):
    hi = v.astype(BF16)
    r = v - hi.astype(F32)
    mid = r.astype(BF16)
    lo = (r - mid.astype(F32)).astype(BF16)
    return hi, mid, lo


def _dot(a, b):
    return jnp.dot(a, b, preferred_element_type=F32)


def _store_tile_rows(ref, row0, val):
    n = val.shape[0]
    for c in range(val.shape[1] // LANES):
        ref[pl.ds(row0 * SUBLANES + c, n, stride=SUBLANES), :] = val[:, c * LANES:(c + 1) * LANES]


def _load_tile_rows(ref, n):
    return [ref[pl.ds(c, n, stride=SUBLANES), :] for c in range(SUBLANES)]


def _mod_kernel(c_ref, w_ref, b_ref, o_ref):
    c = c_ref[...]
    cond = c * _sigmoid(c)
    o_ref[0] = _dot(cond.astype(BF16), w_ref[0].astype(BF16)) + b_ref[0]


def _adaln_mod(c, ada_w, ada_b):
    depth, d, n = ada_w.shape
    b = c.shape[0]
    tn = 1024
    c_pad = jnp.zeros((SUBLANES, d), F32).at[:b].set(c)
    out = pl.pallas_call(
        _mod_kernel,
        out_shape=jax.ShapeDtypeStruct((depth, SUBLANES, n), F32),
        grid=(depth, n // tn),
        in_specs=[
            pl.BlockSpec((SUBLANES, d), lambda i, j: (0, 0)),
            pl.BlockSpec((1, d, tn), lambda i, j: (i, 0, j)),
            pl.BlockSpec((1, 1, tn), lambda i, j: (i, 0, j)),
        ],
        out_specs=pl.BlockSpec((1, SUBLANES, tn), lambda i, j: (i, 0, j)),
        compiler_params=_cparams(("arbitrary", "arbitrary")),
        name="adaln_mod",
    )(c_pad, ada_w, ada_b.reshape(depth, 1, n))
    return out[:, :b]


def _t5_bucket_table():
    qi = np.arange(ATTN_BLOCK)[:, None]
    kj = np.arange(2 * ATTN_BLOCK)[None, :]
    dist = qi + ATTN_BLOCK - kj
    max_exact = N_BUCKETS // 2
    d = np.maximum(dist, 0)
    df = np.maximum(d, 1).astype(np.float64)
    large = max_exact + (np.log(df / max_exact) / math.log(MAX_DISTANCE / max_exact)
                         * (N_BUCKETS - max_exact)).astype(np.int32)
    large = np.minimum(large, N_BUCKETS - 1)
    bucket = np.where(d < max_exact, d, large)
    valid = (dist >= 0) & (dist < ATTN_BLOCK)
    return np.where(valid, bucket, -1).astype(np.int32)


def _bias_kernel(rb_ref, bkt_ref, o_ref):
    j = pl.program_id(0)
    gp = j // Q_PER_KV
    qi = j % Q_PER_KV
    bkt = bkt_ref[...]
    for a in range(2):
        h = (2 * gp + a) * Q_PER_KV + qi
        acc = jnp.full(bkt.shape, NEG_INF, F32)
        for b in range(N_BUCKETS):
            acc = jnp.where(bkt == b, rb_ref[b, h], acc)
        o_ref[0, a * ATTN_BLOCK:(a + 1) * ATTN_BLOCK, :] = acc


def _bias_table(rel_bias):
    bkt = jnp.asarray(_t5_bucket_table())
    n_slots = N_Q_HEADS // 2
    return pl.pallas_call(
        _bias_kernel,
        out_shape=jax.ShapeDtypeStruct((n_slots, 2 * ATTN_BLOCK, 2 * ATTN_BLOCK), F32),
        grid=(n_slots,),
        in_specs=[
            pl.BlockSpec(memory_space=pltpu.SMEM),
            pl.BlockSpec((ATTN_BLOCK, 2 * ATTN_BLOCK), lambda j: (0, 0)),
        ],
        out_specs=pl.BlockSpec((1, 2 * ATTN_BLOCK, 2 * ATTN_BLOCK), lambda j: (j, 0, 0)),
        compiler_params=_cparams(("arbitrary",)),
        name="t5_bias_table",
    )(rel_bias.astype(F32), bkt)


def _qkv_kernel(x_ref, ms_ref, gain_ref, w_ref, qg_ref, kg_ref, q_ref, k_ref, v_ref):
    tm = x_ref.shape[0]
    ms = ms_ref[0]
    hm = _rms_mod(x_ref[...], gain_ref[...], ms[1:2], ms[0:1]).astype(BF16)
    lo = lax.broadcasted_iota(jnp.int32, (tm, LANES), 1) < HEAD_DIM

    def head_norm(slab, g):
        sq = slab * slab
        s_lo = jnp.sum(jnp.where(lo, sq, 0.0), axis=-1, keepdims=True)
        s_hi = jnp.sum(jnp.where(lo, 0.0, sq), axis=-1, keepdims=True)
        msq = jnp.where(lo, s_lo, s_hi) * (1.0 / HEAD_DIM)
        return slab * lax.rsqrt(msq + NORM_EPS) * g

    nq = N_Q_HEADS * HEAD_DIM
    nk = N_KV_HEADS * HEAD_DIM
    q_all = _dot(hm, w_ref[:, :nq])
    for j in range(nq // LANES):
        sl = slice(j * LANES, (j + 1) * LANES)
        q_ref[:, sl] = (head_norm(q_all[:, sl], qg_ref[...]) * (HEAD_DIM ** -0.5)).astype(BF16)
    k_all = _dot(hm, w_ref[:, nq:nq + nk])
    for j in range(nk // LANES):
        sl = slice(j * LANES, (j + 1) * LANES)
        k_ref[:, sl] = head_norm(k_all[:, sl], kg_ref[...]).astype(BF16)
    v_ref[...] = _dot(hm, w_ref[:, nq + nk:]).astype(BF16)


def _qkv_proj(x2, ms, gain, w_qkv_all, li, qg2, kg2, seq, tm=512):
    t, d = x2.shape
    nq = N_Q_HEADS * HEAD_DIM
    nk = N_KV_HEADS * HEAD_DIM
    tpb = seq // tm
    return pl.pallas_call(
        _qkv_kernel,
        out_shape=(jax.ShapeDtypeStruct((t, nq), BF16),
                   jax.ShapeDtypeStruct((t, nk), BF16),
                   jax.ShapeDtypeStruct((t, nk), BF16)),
        grid=(t // tm,),
        in_specs=[
            pl.BlockSpec((tm, d), lambda i: (i, 0)),
            pl.BlockSpec((1, 2, d), lambda i: (i // tpb, 0, 0)),
            pl.BlockSpec((1, d), lambda i: (0, 0)),
            pl.BlockSpec((None, d, nq + 2 * nk), lambda i: (li, 0, 0)),
            pl.BlockSpec((1, LANES), lambda i: (0, 0)),
            pl.BlockSpec((1, LANES), lambda i: (0, 0)),
        ],
        out_specs=(pl.BlockSpec((tm, nq), lambda i: (i, 0)),
                   pl.BlockSpec((tm, nk), lambda i: (i, 0)),
                   pl.BlockSpec((tm, nk), lambda i: (i, 0))),
        compiler_params=_cparams(("arbitrary",)),
        name="attn_qkv",
    )(x2, ms, gain, w_qkv_all, qg2, kg2)


def _attn_kernel(sink_ref, q_ref, kc_ref, kp_ref, vc_ref, vp_ref, bias_ref, o_ref):
    blk = ATTN_BLOCK
    first = pl.program_id(1) == 0
    lane = lax.broadcasted_iota(jnp.int32, (blk, LANES), 1)
    lo = lane < HEAD_DIM
    mask_lo = jnp.where(lo, 1.0, 0.0).astype(BF16)
    mask_hi = jnp.where(lo, 0.0, 1.0).astype(BF16)
    row = lax.broadcasted_iota(jnp.int32, (2 * blk, 1), 0)
    nt = (((1,), (1,)), ((), ()))
    for gp in range(N_KV_HEADS // 2):
        ksl = slice(gp * LANES, (gp + 1) * LANES)
        kc = kc_ref[:, ksl]
        kp = kp_ref[:, ksl]
        vc = vc_ref[:, ksl]
        vp = vp_ref[:, ksl]
        for qi in range(Q_PER_KV):
            j = gp * Q_PER_KV + qi
            sl = slice(j * LANES, (j + 1) * LANES)
            qs = q_ref[:, sl]
            q2 = jnp.concatenate([qs * mask_lo, qs * mask_hi], axis=0)
            s_p = lax.dot_general(q2, kp, nt, preferred_element_type=F32)
            s_c = lax.dot_general(q2, kc, nt, preferred_element_type=F32)
            bias = bias_ref[j]
            s_p = jnp.where(first, NEG_INF, s_p + bias[:, :blk])
            s_c = s_c + bias[:, blk:]
            h_a = (2 * gp) * Q_PER_KV + qi
            h_b = (2 * gp + 1) * Q_PER_KV + qi
            sink = jnp.where(row < blk, sink_ref[h_a], sink_ref[h_b])
            m = jnp.maximum(jnp.max(jnp.maximum(s_p, s_c), axis=-1, keepdims=True), sink)
            p_p = jnp.exp(s_p - m)
            p_c = jnp.exp(s_c - m)
            den = jnp.sum(p_p + p_c, axis=-1, keepdims=True) + jnp.exp(sink - m)
            pv = _dot(p_p.astype(BF16), vp) + _dot(p_c.astype(BF16), vc)
            pv = pv / den
            o_ref[:, sl] = jnp.where(lo, pv[:blk], pv[blk:]).astype(BF16)


def _attention(q, k, v, bias, sinks, batch, seq):
    t, nq = q.shape
    nk = k.shape[1]
    blk = ATTN_BLOCK
    nb = seq // blk
    cur = lambda b, n, *_: (b * nb + n, 0)
    prev = lambda b, n, *_: (b * nb + jnp.maximum(n - 1, 0), 0)
    return pl.pallas_call(
        _attn_kernel,
        out_shape=jax.ShapeDtypeStruct((t, nq), BF16),
        grid=(batch, nb),
        in_specs=[
            pl.BlockSpec(memory_space=pltpu.SMEM),
            pl.BlockSpec((blk, nq), cur),
            pl.BlockSpec((blk, nk), cur),
            pl.BlockSpec((blk, nk), prev),
            pl.BlockSpec((blk, nk), cur),
            pl.BlockSpec((blk, nk), prev),
            pl.BlockSpec(bias.shape, lambda b, n: (0, 0, 0)),
        ],
        out_specs=pl.BlockSpec((blk, nq), cur),
        compiler_params=_cparams(("arbitrary", "arbitrary")),
        name="swa_attention",
    )(sinks.astype(F32), q, k, k, v, v, bias)


def _swiglu_partial(hf_s, wg_ref, wu_ref, wd_ref):
    hf = hf_s[...]
    g = _dot(hf, wg_ref[...].astype(BF16))
    u = _dot(hf, wu_ref[...].astype(BF16))
    a = (g * _sigmoid(g) * u).astype(BF16)
    return _dot(a, wd_ref[...].astype(BF16))


def _ffn_dense_kernel(x_ref, o_ref, wo_ref, mv_ref, gain_ref, wg_ref, wu_ref, wd_ref,
                      out_ref, hf_s, acc_s):
    j = pl.program_id(1)
    last = pl.num_programs(1) - 1

    @pl.when(j == 0)
    def _():
        mv = mv_ref[0]
        x1 = x_ref[...] + mv[0:1] * _dot(o_ref[...], wo_ref[...])
        out_ref[...] = x1
        hf_s[...] = _rms_mod(x1, gain_ref[...], mv[2:3], mv[1:2]).astype(BF16)
        acc_s[...] = _swiglu_partial(hf_s, wg_ref, wu_ref, wd_ref)

    @pl.when((j > 0) & (j < last))
    def _():
        acc_s[...] += _swiglu_partial(hf_s, wg_ref, wu_ref, wd_ref)

    @pl.when(j == last)
    def _():
        ffn = acc_s[...] + _swiglu_partial(hf_s, wg_ref, wu_ref, wd_ref)
        out_ref[...] = out_ref[...] + mv_ref[0][3:4] * ffn


def _ffn_dense(x2, o, wo_all, mv, gain, w_gu_all, w_down_all, li, seq, tm=1024, tf=512):
    t, d = x2.shape
    f = w_down_all.shape[1]
    nj = f // tf
    tpb = seq // tm
    return pl.pallas_call(
        _ffn_dense_kernel,
        out_shape=jax.ShapeDtypeStruct((t, d), F32),
        grid=(t // tm, nj),
        in_specs=[
            pl.BlockSpec((tm, d), lambda i, j: (i, 0)),
            pl.BlockSpec((tm, o.shape[1]), lambda i, j: (i, 0)),
            pl.BlockSpec((None,) + wo_all.shape[1:], lambda i, j: (li, 0, 0)),
            pl.BlockSpec((1, 4, d), lambda i, j: (i // tpb, 0, 0)),
            pl.BlockSpec((1, d), lambda i, j: (0, 0)),
            pl.BlockSpec((None, d, tf), lambda i, j: (li, 0, j)),
            pl.BlockSpec((None, d, tf), lambda i, j: (li, 0, nj + j)),
            pl.BlockSpec((None, tf, d), lambda i, j: (li, j, 0)),
        ],
        out_specs=pl.BlockSpec((tm, d), lambda i, j: (i, 0)),
        scratch_shapes=[pltpu.VMEM((tm, d), BF16), pltpu.VMEM((tm, d), F32)],
        compiler_params=_cparams(("arbitrary", "arbitrary")),
        name="wo_ffn_dense",
    )(x2, o, wo_all, mv, gain, w_gu_all, w_gu_all, w_down_all)


def _ffn_moe_kernel(te_ref, xs_ref, wg_ref, wu_ref, wd_ref, out_ref, hf_s, acc_s):
    i = pl.program_id(0)
    j = pl.program_id(1)
    used = i < te_ref[pl.num_programs(0)]

    tm = hf_s.shape[0]
    last = pl.num_programs(1) - 1

    @pl.when(used & (j == 0))
    def _():
        for c, part in enumerate(_load_tile_rows(xs_ref, tm)):
            hf_s[:, c * LANES:(c + 1) * LANES] = part.astype(BF16)
        acc_s[...] = _swiglu_partial(hf_s, wg_ref, wu_ref, wd_ref)

    @pl.when(used & (j > 0) & (j < last))
    def _():
        acc_s[...] += _swiglu_partial(hf_s, wg_ref, wu_ref, wd_ref)

    @pl.when(used & (j == last))
    def _():
        _store_tile_rows(out_ref, 0, acc_s[...] + _swiglu_partial(hf_s, wg_ref, wu_ref, wd_ref))

    @pl.when(jnp.logical_not(used) & (j == last))
    def _():
        out_ref[...] = jnp.zeros_like(out_ref)


def _ffn_moe(xs, tile_expert, w_gu_all, w_down_all, li, tm, tf=512):
    r = xs.shape[0] // SUBLANES
    d = w_gu_all.shape[2]
    f = w_down_all.shape[2]
    nj = f // tf
    nt = r // tm

    def jj(i, j, te):
        return jnp.where(i < te[nt], j, 0)

    return pl.pallas_call(
        _ffn_moe_kernel,
        out_shape=jax.ShapeDtypeStruct((r * SUBLANES, LANES), F32),
        grid_spec=pltpu.PrefetchScalarGridSpec(
            num_scalar_prefetch=1,
            grid=(nt, nj),
            in_specs=[
                pl.BlockSpec((tm * SUBLANES, LANES), lambda i, j, te: (i, 0)),
                pl.BlockSpec((None, None, d, tf), lambda i, j, te: (li, te[i], 0, jj(i, j, te))),
                pl.BlockSpec((None, None, d, tf), lambda i, j, te: (li, te[i], 0, nj + jj(i, j, te))),
                pl.BlockSpec((None, None, tf, d), lambda i, j, te: (li, te[i], jj(i, j, te), 0)),
            ],
            out_specs=pl.BlockSpec((tm * SUBLANES, LANES), lambda i, j, te: (i, 0)),
            scratch_shapes=[pltpu.VMEM((tm, d), BF16), pltpu.VMEM((tm, d), F32)],
        ),
        compiler_params=_cparams(("arbitrary", "arbitrary")),
        name="moe_expert_ffn",
    )(tile_expert, xs, w_gu_all, w_gu_all, w_down_all)


def _inproj_kernel(x_ref, ms_ref, gain_ref, wz_ref, wx_ref, wdt_ref, cw_ref, cb_ref,
                   z_ref, xbc_ref, dt_ref, hm_s, carry_s, *, tiles_per_seq):
    i = pl.program_id(0)
    j = pl.program_id(1)
    tm = z_ref.shape[0]

    @pl.when((i == 0) & (j == 0))
    def _():
        carry_s[...] = jnp.zeros_like(carry_s)

    @pl.when(j == 0)
    def _():
        ms = ms_ref[0]
        hm = _rms_mod(x_ref[...], gain_ref[...], ms[1:2], ms[0:1]).astype(BF16)
        hm_s[...] = hm
        dt_ref[...] = _dot(hm, wdt_ref[...])

    hm = hm_s[...]
    strip = 2 * LANES
    rowi = lax.broadcasted_iota(jnp.int32, (SUBLANES, strip), 0)
    first = i % tiles_per_seq == 0
    n_z_strips = z_ref.shape[1] // strip
    for k, c0 in enumerate(range(0, xbc_ref.shape[1], strip)):
        cs = slice(c0, c0 + strip)
        res = _dot(hm, wx_ref[:, cs])
        if k < n_z_strips:
            zs = slice(k * strip, (k + 1) * strip)
            z_ref[:, zs] = _dot(hm, wz_ref[:, zs]).astype(BF16)
        history = jnp.where(first, 0.0, carry_s[j, :, cs])
        acc = res * cw_ref[CONV_WIDTH - 1:CONV_WIDTH, cs] + cb_ref[:, cs]
        for s in range(1, CONV_WIDTH):
            down = pltpu.roll(res, s, 0)
            head = jnp.where(rowi < s, pltpu.roll(history, s, 0), down[:SUBLANES])
            shifted = jnp.concatenate([head, down[SUBLANES:]], axis=0)
            acc = acc + shifted * cw_ref[CONV_WIDTH - 1 - s:CONV_WIDTH - s, cs]
        carry_s[j, :, cs] = res[tm - SUBLANES:]
        xbc_ref[:, cs] = (acc * _sigmoid(acc)).astype(BF16)


def _ssm_in_proj(x2, ms, gain, w_z_all, w_xbc_all, w_dt_all, li, conv_w, conv_b, seq, tm=1024, nj=2):
    t, d = x2.shape
    nz = w_z_all.shape[2] // nj
    nx = w_xbc_all.shape[2] // nj
    tpb = seq // tm
    return pl.pallas_call(
        functools.partial(_inproj_kernel, tiles_per_seq=tpb),
        out_shape=(jax.ShapeDtypeStruct((t, nz * nj), BF16), jax.ShapeDtypeStruct((t, nx * nj), BF16),
                   jax.ShapeDtypeStruct((t, LANES), F32)),
        grid=(t // tm, nj),
        in_specs=[
            pl.BlockSpec((tm, d), lambda i, j: (i, 0)),
            pl.BlockSpec((1, 2, d), lambda i, j: (i // tpb, 0, 0)),
            pl.BlockSpec((1, d), lambda i, j: (0, 0)),
            pl.BlockSpec((None, d, nz), lambda i, j: (li, 0, j)),
            pl.BlockSpec((None, d, nx), lambda i, j: (li, 0, j)),
            pl.BlockSpec((None, d, LANES), lambda i, j: (li, 0, 0)),
            pl.BlockSpec((CONV_WIDTH, nx), lambda i, j: (0, j)),
            pl.BlockSpec((1, nx), lambda i, j: (0, j)),
        ],
        out_specs=(pl.BlockSpec((tm, nz), lambda i, j: (i, j)),
                   pl.BlockSpec((tm, nx), lambda i, j: (i, j)),
                   pl.BlockSpec((tm, LANES), lambda i, j: (i, 0))),
        scratch_shapes=[pltpu.VMEM((tm, d), BF16), pltpu.VMEM((nj, SUBLANES, nx), F32)],
        compiler_params=_cparams(("arbitrary", "arbitrary")),
        name="ssm_in_proj",
    )(x2, ms, gain, w_z_all, w_xbc_all, w_dt_all, conv_w.astype(F32), conv_b.astype(F32)[None])


def _ssd_kernel(z_ref, xs_ref, bc_ref, dt_ref, dtb_ref, alog_ref, dskip_ref, nw_ref, r64_ref,
                y_ref, state_s, ybuf_s):
    L = SSM_CHUNK
    gw = D_INNER // N_SSM_GROUPS
    hpg = N_SSM_HEADS // N_SSM_GROUPS

    @pl.when(pl.program_id(1) == 0)
    def _():
        state_s[...] = jnp.zeros_like(state_s)

    dtv = jax.nn.softplus(dt_ref[...] + dtb_ref[...])
    a = -jnp.exp(alog_ref[...])
    dta = dtv * a
    ri = lax.broadcasted_iota(jnp.int32, (L, L), 0)
    ci = lax.broadcasted_iota(jnp.int32, (L, L), 1)
    causal = ri >= ci
    tri = jnp.where(causal, 1.0, 0.0).astype(BF16)
    p0, p1, p2 = _split3(dta)
    acs = _dot(tri, p0) + _dot(tri, p1) + _dot(tri, p2)
    acs_t = acs.T
    eacs = jnp.exp(acs)
    dte = jnp.exp(acs[L - 1:L, :] - acs)
    stacked = jnp.concatenate([dtv, eacs, dte], axis=0)
    s_hi, s_lo = _split2(stacked)
    expd = _dot(s_hi, r64_ref[...]) + _dot(s_lo, r64_ref[...])

    lo = lax.broadcasted_iota(jnp.int32, (L, LANES), 1) < (LANES // 2)
    for g in range(N_SSM_GROUPS):
        gs = slice(g * gw, (g + 1) * gw)
        b_g = bc_ref[:, g * D_STATE:(g + 1) * D_STATE]
        c_b = bc_ref[:, (N_SSM_GROUPS + g) * D_STATE:(N_SSM_GROUPS + g + 1) * D_STATE]
        b_t = b_g.astype(F32).T.astype(BF16)
        cb = _dot(c_b, b_t)
        xs_g = xs_ref[:, gs].astype(F32)
        xd = xs_g * expd[0:L, gs]
        xd_b = xd.astype(BF16)
        st_prev = state_s[g]
        y_off = _dot(c_b, st_prev.astype(BF16)) * expd[L:2 * L, gs]
        xdw = (xd * expd[2 * L:3 * L, gs]).astype(BF16)
        state_s[g] = expd[2 * L - 1:2 * L, gs] * st_prev + _dot(b_t, xdw)
        for jp in range(hpg // 2):
            ps = slice(jp * LANES, (jp + 1) * LANES)
            slab = xd_b[:, ps]
            halves = []
            for half in range(2):
                hh = g * hpg + 2 * jp + half
                diff = acs[:, hh:hh + 1] - acs_t[hh:hh + 1, :]
                dec = jnp.exp(jnp.where(causal, diff, NEG_INF))
                halves.append(_dot((cb * dec).astype(BF16), slab))
            y = jnp.where(lo, halves[0], halves[1]) + y_off[:, ps]
            cols = slice(g * gw + jp * LANES, g * gw + (jp + 1) * LANES)
            ybuf_s[:, cols] = y + dskip_ref[:, cols] * xs_g[:, ps]

    z = z_ref[...].astype(F32)
    yg = ybuf_s[...] * (z * _sigmoid(z))
    ms = jnp.mean(yg * yg, axis=-1, keepdims=True)
    y_ref[...] = (yg * lax.rsqrt(ms + NORM_EPS) * nw_ref[...]).astype(BF16)


def _ssd(z, xbc, dt, dt_bias, a_log, d_skip, norm_w, batch, seq):
    t = z.shape[0]
    L = SSM_CHUNK
    nc = seq // L
    di, dbc = D_INNER, D_BC
    pad = LANES - N_SSM_HEADS
    dtb = jnp.pad(dt_bias.astype(F32), (0, pad))[None]
    alog = jnp.pad(a_log.astype(F32), (0, pad))[None]
    dskip = jnp.repeat(d_skip.astype(F32), di // N_SSM_HEADS)[None]
    nw = norm_w.astype(F32)[None]
    r64 = jnp.asarray((np.arange(LANES)[:, None] == (np.arange(di)[None, :] // (di // N_SSM_HEADS)))
                      .astype(np.float32)).astype(BF16)
    row = lambda b, c: (b * nc + c, 0)
    const = lambda b, c: (0, 0)
    return pl.pallas_call(
        _ssd_kernel,
        out_shape=jax.ShapeDtypeStruct((t, di), BF16),
        grid=(batch, nc),
        in_specs=[
            pl.BlockSpec((L, di), row),
            pl.BlockSpec((L, di), row),
            pl.BlockSpec((L, dbc), lambda b, c: (b * nc + c, di // dbc)),
            pl.BlockSpec((L, LANES), row),
            pl.BlockSpec((1, LANES), const),
            pl.BlockSpec((1, LANES), const),
            pl.BlockSpec((1, di), const),
            pl.BlockSpec((1, di), const),
            pl.BlockSpec((LANES, di), const),
        ],
        out_specs=pl.BlockSpec((L, di), row),
        scratch_shapes=[
            pltpu.VMEM((N_SSM_GROUPS, D_STATE, di // N_SSM_GROUPS), F32),
            pltpu.VMEM((L, di), F32),
        ],
        compiler_params=_cparams(("arbitrary", "arbitrary")),
        name="ssd_scan",
    )(z, xbc, xbc, dt, dtb, alog, dskip, nw, r64)


ROUTER_SUB = 512


def _router_kernel(x_ref, y_ref, wout_ref, mv_ref, gain_ref, wr_ref,
                   x1_ref, hf_ref, rinfo_ref, cnt_ref, carry_s):
    tm = x_ref.shape[0]
    sub = ROUTER_SUB

    @pl.when(pl.program_id(0) == 0)
    def _():
        carry_s[...] = jnp.zeros_like(carry_s)

    mv = mv_ref[0]
    lane = lax.broadcasted_iota(jnp.int32, (sub, LANES), 1)
    lanef = lane.astype(F32)
    ri = lax.broadcasted_iota(jnp.int32, (sub, sub), 0)
    ci = lax.broadcasted_iota(jnp.int32, (sub, sub), 1)
    before = jnp.where(ri > ci, 1.0, 0.0).astype(BF16)
    carry = carry_s[...]
    for h in range(tm // sub):
        rs = slice(h * sub, (h + 1) * sub)
        x1 = x_ref[rs, :] + mv[0:1] * _dot(y_ref[rs, :], wout_ref[...])
        x1_ref[rs, :] = x1
        hf = _rms_mod(x1, gain_ref[...], mv[2:3], mv[1:2])
        _store_tile_rows(hf_ref, h * sub, hf)
        h_hi, h_lo = _split2(hf)
        hw = _dot(h_hi, wr_ref[...])
        logits = hw[:, :LANES] + hw[:, LANES:] + _dot(h_lo, wr_ref[:, :LANES])
        lg = jnp.where(lane < N_EXPERTS, logits, NEG_INF)
        v0 = jnp.max(lg, axis=-1, keepdims=True)
        i0 = jnp.min(jnp.where(lg == v0, lanef, float(LANES)), axis=-1, keepdims=True)
        lg1 = jnp.where(lanef == i0, NEG_INF, lg)
        v1 = jnp.max(lg1, axis=-1, keepdims=True)
        i1 = jnp.min(jnp.where(lg1 == v1, lanef, float(LANES)), axis=-1, keepdims=True)
        e1 = jnp.exp(v1 - v0)
        w0 = 1.0 / (1.0 + e1)
        w1 = e1 / (1.0 + e1)
        oh0 = jnp.where(lanef == i0, 1.0, 0.0)
        oh1 = jnp.where(lanef == i1, 1.0, 0.0)
        pre = _dot(before, jnp.concatenate([oh0, oh1], axis=1).astype(BF16))
        cnt0 = jnp.sum(oh0, axis=0, keepdims=True)
        cnt1 = jnp.sum(oh1, axis=0, keepdims=True)
        rank0 = jnp.sum(oh0 * (pre[:, :LANES] + carry), axis=-1, keepdims=True)
        rank1 = jnp.sum(oh1 * (pre[:, LANES:] + carry + cnt0), axis=-1, keepdims=True)
        carry = carry + cnt0 + cnt1
        rinfo_ref[rs, :] = jnp.where(lane == 0, i0, jnp.where(lane == 1, i1, jnp.where(
            lane == 2, rank0, jnp.where(lane == 3, rank1, jnp.where(
                lane == 4, w0, jnp.where(lane == 5, w1, 0.0))))))
    carry_s[...] = carry
    cnt_ref[...] = carry


def _router(x2, y, wout_all, li, mv, gain, wr, seq, tm=1024):
    t, d = x2.shape
    tpb = seq // tm
    return pl.pallas_call(
        _router_kernel,
        out_shape=(jax.ShapeDtypeStruct((t, d), F32),
                   jax.ShapeDtypeStruct((t * SUBLANES, LANES), F32),
                   jax.ShapeDtypeStruct((t, LANES), F32), jax.ShapeDtypeStruct((1, LANES), F32)),
        grid=(t // tm,),
        in_specs=[
            pl.BlockSpec((tm, d), lambda i: (i, 0)),
            pl.BlockSpec((tm, y.shape[1]), lambda i: (i, 0)),
            pl.BlockSpec((None,) + wout_all.shape[1:], lambda i: (li, 0, 0)),
            pl.BlockSpec((1, 3, d), lambda i: (i // tpb, 0, 0)),
            pl.BlockSpec((1, d), lambda i: (0, 0)),
            pl.BlockSpec((d, 2 * LANES), lambda i: (0, 0)),
        ],
        out_specs=(pl.BlockSpec((tm, d), lambda i: (i, 0)),
                   pl.BlockSpec((tm * SUBLANES, LANES), lambda i: (i, 0)),
                   pl.BlockSpec((tm, LANES), lambda i: (i, 0)),
                   pl.BlockSpec((1, LANES), lambda i: (0, 0))),
        scratch_shapes=[pltpu.VMEM((1, LANES), F32)],
        compiler_params=_cparams(("arbitrary",)),
        name="ssm_out_router",
    )(x2, y, wout_all, mv, gain, wr)


DMA_UNROLL = 8


def _row_copy(src, src_row, dst, dst_row, sem):
    return pltpu.make_async_copy(
        src.at[pl.ds(pl.multiple_of(src_row * SUBLANES, SUBLANES), SUBLANES)],
        dst.at[pl.ds(pl.multiple_of(dst_row * SUBLANES, SUBLANES), SUBLANES)], sem)


def _dispatch_kernel(pos_ref, hf_ref, sorted_in_ref, sorted_ref, sem):
    del sorted_in_ref
    tm = hf_ref.shape[0] // SUBLANES
    base = pl.program_id(0) * (TOP_K * tm)

    def issue(blk, carry):
        for u in range(DMA_UNROLL):
            t = blk * DMA_UNROLL + u
            for k in range(TOP_K):
                _row_copy(hf_ref, t, sorted_ref, pos_ref[base + TOP_K * t + k], sem).start(priority=k)
        return carry

    lax.fori_loop(0, tm // DMA_UNROLL, issue, 0)
    for k in range(TOP_K):
        pltpu.make_async_copy(hf_ref, sorted_ref.at[pl.ds(0, tm * SUBLANES)], sem).wait()


def _dispatch(pos_flat, hf, sorted_init, tm=1024):
    t = hf.shape[0] // SUBLANES
    return pl.pallas_call(
        _dispatch_kernel,
        out_shape=jax.ShapeDtypeStruct(sorted_init.shape, hf.dtype),
        grid_spec=pltpu.PrefetchScalarGridSpec(
            num_scalar_prefetch=1,
            grid=(t // tm,),
            in_specs=[pl.BlockSpec((tm * SUBLANES, LANES), lambda i, pos: (i, 0)),
                      pl.BlockSpec(memory_space=pl.ANY)],
            out_specs=pl.BlockSpec(memory_space=pl.ANY),
            scratch_shapes=[pltpu.SemaphoreType.DMA],
        ),
        input_output_aliases={2: 0},
        compiler_params=_cparams(("arbitrary",)),
        name="moe_dispatch",
    )(pos_flat, hf, sorted_init)


def _combine_kernel(pos_ref, x1_ref, rinfo_ref, g2_ref, ys_ref, out_ref, buf, sem):
    tm, d = x1_ref.shape
    i = pl.program_id(0)
    slot = i % 2

    def issue_tile(tile, dst_slot):
        base = tile * (TOP_K * tm)

        def issue(blk, carry):
            for u in range(DMA_UNROLL):
                t = blk * DMA_UNROLL + u
                for k in range(TOP_K):
                    _row_copy(ys_ref, pos_ref[base + TOP_K * t + k], buf.at[dst_slot, k], t,
                              sem.at[dst_slot]).start(priority=k)
            return carry

        lax.fori_loop(0, tm // DMA_UNROLL, issue, 0)

    @pl.when(i == 0)
    def _():
        issue_tile(0, 0)

    @pl.when(i + 1 < pl.num_programs(0))
    def _():
        issue_tile(i + 1, 1 - slot)

    for k in range(TOP_K):
        pltpu.make_async_copy(ys_ref.at[pl.ds(0, tm * SUBLANES)], buf.at[slot, k], sem.at[slot]).wait()
    rinfo = rinfo_ref[...]
    w0 = rinfo[:, 4:5]
    w1 = rinfo[:, 5:6]
    g2 = g2_ref[0]
    for c in range(d // LANES):
        cs = slice(c * LANES, (c + 1) * LANES)
        ffn = (w0 * buf[slot, 0, pl.ds(c, tm, stride=SUBLANES), :]
               + w1 * buf[slot, 1, pl.ds(c, tm, stride=SUBLANES), :])
        out_ref[:, cs] = x1_ref[:, cs] + g2[:, cs] * ffn


def _combine(pos_flat, x1, rinfo, g2, ys, seq, tm=512):
    t, d = x1.shape
    tpb = seq // tm
    return pl.pallas_call(
        _combine_kernel,
        out_shape=jax.ShapeDtypeStruct((t, d), F32),
        grid_spec=pltpu.PrefetchScalarGridSpec(
            num_scalar_prefetch=1,
            grid=(t // tm,),
            in_specs=[pl.BlockSpec((tm, d), lambda i, pos: (i, 0)),
                      pl.BlockSpec((tm, LANES), lambda i, pos: (i, 0)),
                      pl.BlockSpec((1, 1, d), lambda i, pos: (i // tpb, 0, 0)),
                      pl.BlockSpec(memory_space=pl.ANY)],
            out_specs=pl.BlockSpec((tm, d), lambda i, pos: (i, 0)),
            scratch_shapes=[pltpu.VMEM((2, TOP_K, tm * SUBLANES, LANES), F32),
                            pltpu.SemaphoreType.DMA((2,))],
        ),
        compiler_params=_cparams(("arbitrary",)),
        name="moe_combine",
    )(pos_flat, x1, rinfo, g2, ys)


def _attn_weights(attn_w_qkv, attn_w_o):
    n_att, d, _ = attn_w_qkv.shape
    nq = N_Q_HEADS * HEAD_DIM
    head_grid = (N_KV_HEADS // 2, 2, Q_PER_KV, HEAD_DIM)
    w_q = attn_w_qkv[:, :, :nq].reshape((n_att, d) + head_grid)
    w_q = jnp.swapaxes(w_q, 3, 4).reshape(n_att, d, nq)
    w_qkv_all = jnp.concatenate([w_q, attn_w_qkv[:, :, nq:]], axis=2).astype(BF16)
    w_o = attn_w_o.reshape((n_att,) + head_grid + (attn_w_o.shape[2],))
    w_o_all = jnp.swapaxes(w_o, 2, 3).reshape(n_att, nq, attn_w_o.shape[2]).astype(BF16)
    return w_qkv_all, w_o_all


def _attn_layer(x2, mod_i, gain_i, bias, w_qkv_all, w_o_all, w_gu_all, w_down_all, li,
                q_gain, k_gain, sinks, batch, seq):
    qg2 = jnp.tile(q_gain.astype(F32), 2)[None]
    kg2 = jnp.tile(k_gain.astype(F32), 2)[None]
    q, k, v = _qkv_proj(x2, mod_i[:, 0:2], gain_i[0:1], w_qkv_all, li, qg2, kg2, seq)
    o = _attention(q, k, v, bias, sinks, batch, seq)
    return _ffn_dense(x2, o, w_o_all, mod_i[:, 2:6], gain_i[1:2], w_gu_all, w_down_all, li, seq)


def _moe_slots(rinfo, cnt, n_tokens, moe_tm):
    counts = cnt[0, :N_EXPERTS].astype(jnp.int32)
    padded = ((counts + moe_tm - 1) // moe_tm) * moe_tm
    ends = jnp.cumsum(padded)
    base = ends - padded
    eidx = rinfo[:, 0:TOP_K].astype(jnp.int32)
    rank = rinfo[:, 2:2 + TOP_K].astype(jnp.int32)
    pos = jnp.sum(jnp.where(eidx[..., None] == jnp.arange(N_EXPERTS), base, 0), axis=-1) + rank
    n_tiles = (TOP_K * n_tokens) // moe_tm + N_EXPERTS
    tile_start = jnp.arange(n_tiles, dtype=jnp.int32) * moe_tm
    tile_expert = jnp.minimum(jnp.sum(tile_start[:, None] >= ends[None, :], axis=-1),
                              N_EXPERTS - 1).astype(jnp.int32)
    n_used = ends[-1] // moe_tm
    tile_expert = jnp.where(tile_start < ends[-1], tile_expert, tile_expert[n_used - 1])
    return pos.reshape(-1), jnp.concatenate([tile_expert, n_used[None].astype(jnp.int32)])


def _ssm_layer(x2, mod_i, gain_i, w_z_all, w_xbc_all, w_dt_all, w_out_all, w_gu_all, w_down_all, li,
               conv_w, conv_b, dt_bias, a_log, d_skip, norm_w, w_router, sorted_init,
               batch, seq, moe_tm):
    z, xbc, dt = _ssm_in_proj(x2, mod_i[:, 0:2], gain_i[0:1], w_z_all, w_xbc_all,
                              w_dt_all, li, conv_w, conv_b, seq)
    y = _ssd(z, xbc, dt, dt_bias, a_log, d_skip, norm_w, batch, seq)

    wr = jnp.pad(w_router.astype(F32), ((0, 0), (0, LANES - N_EXPERTS)))
    wr_hi = wr.astype(BF16)
    wr_lo = (wr - wr_hi.astype(F32)).astype(BF16)
    x1, hf, rinfo, cnt = _router(x2, y, w_out_all, li, mod_i[:, 2:5],
                                 gain_i[1:2], jnp.concatenate([wr_hi, wr_lo], axis=1), seq)
    pos_flat, tile_expert = _moe_slots(rinfo, cnt, x2.shape[0], moe_tm)
    xs = _dispatch(pos_flat, hf, sorted_init)
    ys = _ffn_moe(xs, tile_expert, w_gu_all, w_down_all, li, moe_tm)
    return _combine(pos_flat, x1, rinfo, mod_i[:, 5:6], ys, seq), xs


def kernel(x, c, ada_w, ada_b, norm_gain, rel_bias, attn_w_qkv, attn_q_gain, attn_k_gain, attn_sinks, attn_w_o, ssm_w_in, ssm_conv_w, ssm_conv_b, ssm_dt_bias, ssm_a_log, ssm_d, ssm_norm_w, ssm_w_out, ffn_w_gu, ffn_w_down, moe_w_router, moe_w_gu, moe_w_down):
    batch, seq, d = x.shape
    moe_tm = 1024
    mod = _adaln_mod(c.astype(F32), ada_w.astype(F32), ada_b.astype(F32))
    mod = mod.reshape(DEPTH, batch, 6, d)
    bias = _bias_table(rel_bias)

    n_zx = 2 * D_INNER + D_BC
    w_qkv_all, w_o_all = _attn_weights(attn_w_qkv, attn_w_o)
    ffn_gu_all = ffn_w_gu.astype(BF16)
    ffn_down_all = ffn_w_down.astype(BF16)
    w_z_all = ssm_w_in[:, :, :D_INNER].astype(BF16)
    w_xbc_all = ssm_w_in[:, :, D_INNER:n_zx].astype(BF16)
    w_dt_all = jnp.pad(ssm_w_in[:, :, n_zx:], ((0, 0), (0, 0), (0, LANES - N_SSM_HEADS))).astype(BF16)
    w_out_all = ssm_w_out.astype(BF16)
    moe_gu_all = moe_w_gu.astype(F32)
    moe_down_all = moe_w_down.astype(F32)

    n_sorted = TOP_K * batch * seq + N_EXPERTS * moe_tm
    sorted_buf = jnp.zeros((n_sorted * SUBLANES, LANES), F32)

    x2 = x.reshape(batch * seq, d).astype(F32)
    for i in range(DEPTH):
        j = i // 2
        gain_i = norm_gain[i].astype(F32)
        if i % 2 == 0:
            x2 = _attn_layer(x2, mod[i], gain_i, bias, w_qkv_all, w_o_all, ffn_gu_all, ffn_down_all,
                             j, attn_q_gain[j], attn_k_gain[j], attn_sinks[j], batch, seq)
        else:
            x2, sorted_buf = _ssm_layer(x2, mod[i], gain_i, w_z_all, w_xbc_all, w_dt_all, w_out_all, moe_gu_all,
                                        moe_down_all, j, ssm_conv_w[j], ssm_conv_b[j], ssm_dt_bias[j],
                                        ssm_a_log[j], ssm_d[j], ssm_norm_w[j], moe_w_router[j],
                                        sorted_buf, batch, seq, moe_tm)
    return x2.reshape(batch, seq, d).astype(x.dtype)
```

```python
import functools
import math

import numpy as np
import jax
import jax.numpy as jnp
from jax import lax
from jax.experimental import pallas as pl
from jax.experimental.pallas import tpu as pltpu

F32 = jnp.float32
BF16 = jnp.bfloat16

D_MODEL = 1024
DEPTH = 4
N_Q_HEADS = 16
N_KV_HEADS = 4
HEAD_DIM = 64
Q_PER_KV = 4
ATTN_BLOCK = 128
N_BUCKETS = 32
MAX_DISTANCE = 128
D_INNER = 2048
N_SSM_HEADS = 32
N_SSM_GROUPS = 4
D_STATE = 128
CONV_WIDTH = 4
SSM_CHUNK = 128
D_BC = 2 * N_SSM_GROUPS * D_STATE
D_FF = 3584
N_EXPERTS = 8
TOP_K = 2
NORM_EPS = 1e-6

LANES = 128
SUBLANES = 8
VMEM_LIMIT = 56 * 1024 * 1024
NEG_INF = float("-inf")


def _cparams(sem):
    return pltpu.CompilerParams(dimension_semantics=sem, vmem_limit_bytes=VMEM_LIMIT)


def _sigmoid(v):
    return 1.0 / (1.0 + jnp.exp(-v))


def _rms_mod(x, gain, scale, shift):
    ms = jnp.mean(x * x, axis=-1, keepdims=True)
    y = x * lax.rsqrt(ms + NORM_EPS)
    return (y * gain) * (1.0 + scale) + shift


def _split2(v):
    hi = v.astype(BF16)
    lo = (v - hi.astype(F32)).astype(BF16)
    return hi, lo


def _split3(v):
    hi = v.astype(BF16)
    r = v - hi.astype(F32)
    mid = r.astype(BF16)
    lo = (r - mid.astype(F32)).astype(BF16)
    return hi, mid, lo


def _dot(a, b):
    return jnp.dot(a, b, preferred_element_type=F32)


def _store_tile_rows(ref, row0, val):
    n = val.shape[0]
    for c in range(val.shape[1] // LANES):
        ref[pl.ds(row0 * SUBLANES + c, n, stride=SUBLANES), :] = val[:, c * LANES:(c + 1) * LANES]


def _load_tile_rows(ref, n):
    return [ref[pl.ds(c, n, stride=SUBLANES), :] for c in range(SUBLANES)]


def _mod_kernel(c_ref, w_ref, b_ref, o_ref):
    c = c_ref[...]
    cond = c * _sigmoid(c)
    o_ref[0] = _dot(cond.astype(BF16), w_ref[0].astype(BF16)) + b_ref[0]


def _adaln_mod(c, ada_w, ada_b):
    depth, d, n = ada_w.shape
    b = c.shape[0]
    tn = 1024
    c_pad = jnp.zeros((SUBLANES, d), F32).at[:b].set(c)
    out = pl.pallas_call(
        _mod_kernel,
        out_shape=jax.ShapeDtypeStruct((depth, SUBLANES, n), F32),
        grid=(depth, n // tn),
        in_specs=[
            pl.BlockSpec((SUBLANES, d), lambda i, j: (0, 0)),
            pl.BlockSpec((1, d, tn), lambda i, j: (i, 0, j)),
            pl.BlockSpec((1, 1, tn), lambda i, j: (i, 0, j)),
        ],
        out_specs=pl.BlockSpec((1, SUBLANES, tn), lambda i, j: (i, 0, j)),
        compiler_params=_cparams(("arbitrary", "arbitrary")),
        name="adaln_mod",
    )(c_pad, ada_w, ada_b.reshape(depth, 1, n))
    return out[:, :b]


def _t5_bucket_table():
    qi = np.arange(ATTN_BLOCK)[:, None]
    kj = np.arange(2 * ATTN_BLOCK)[None, :]
    dist = qi + ATTN_BLOCK - kj
    max_exact = N_BUCKETS // 2
    d = np.maximum(dist, 0)
    df = np.maximum(d, 1).astype(np.float64)
    large = max_exact + (np.log(df / max_exact) / math.log(MAX_DISTANCE / max_exact)
                         * (N_BUCKETS - max_exact)).astype(np.int32)
    large = np.minimum(large, N_BUCKETS - 1)
    bucket = np.where(d < max_exact, d, large)
    valid = (dist >= 0) & (dist < ATTN_BLOCK)
    return np.where(valid, bucket, -1).astype(np.int32)


def _bias_kernel(rb_ref, bkt_ref, o_ref):
    j = pl.program_id(0)
    gp = j // Q_PER_KV
    qi = j % Q_PER_KV
    bkt = bkt_ref[...]
    for a in range(2):
        h = (2 * gp + a) * Q_PER_KV + qi
        acc = jnp.full(bkt.shape, NEG_INF, F32)
        for b in range(N_BUCKETS):
            acc = jnp.where(bkt == b, rb_ref[b, h], acc)
        o_ref[0, a * ATTN_BLOCK:(a + 1) * ATTN_BLOCK, :] = acc


def _bias_table(rel_bias):
    bkt = jnp.asarray(_t5_bucket_table())
    n_slots = N_Q_HEADS // 2
    return pl.pallas_call(
        _bias_kernel,
        out_shape=jax.ShapeDtypeStruct((n_slots, 2 * ATTN_BLOCK, 2 * ATTN_BLOCK), F32),
        grid=(n_slots,),
        in_specs=[
            pl.BlockSpec(memory_space=pltpu.SMEM),
            pl.BlockSpec((ATTN_BLOCK, 2 * ATTN_BLOCK), lambda j: (0, 0)),
        ],
        out_specs=pl.BlockSpec((1, 2 * ATTN_BLOCK, 2 * ATTN_BLOCK), lambda j: (j, 0, 0)),
        compiler_params=_cparams(("arbitrary",)),
        name="t5_bias_table",
    )(rel_bias.astype(F32), bkt)


def _qkv_kernel(x_ref, ms_ref, gain_ref, w_ref, qg_ref, kg_ref, q_ref, k_ref, v_ref):
    tm = x_ref.shape[0]
    ms = ms_ref[0]
    hm = _rms_mod(x_ref[...], gain_ref[...], ms[1:2], ms[0:1]).astype(BF16)
    lo = lax.broadcasted_iota(jnp.int32, (tm, LANES), 1) < HEAD_DIM

    def head_norm(slab, g):
        sq = slab * slab
        s_lo = jnp.sum(jnp.where(lo, sq, 0.0), axis=-1, keepdims=True)
        s_hi = jnp.sum(jnp.where(lo, 0.0, sq), axis=-1, keepdims=True)
        msq = jnp.where(lo, s_lo, s_hi) * (1.0 / HEAD_DIM)
        return slab * lax.rsqrt(msq + NORM_EPS) * g

    nq = N_Q_HEADS * HEAD_DIM
    nk = N_KV_HEADS * HEAD_DIM
    q_all = _dot(hm, w_ref[:, :nq])
    for j in range(nq // LANES):
        sl = slice(j * LANES, (j + 1) * LANES)
        q_ref[:, sl] = (head_norm(q_all[:, sl], qg_ref[...]) * (HEAD_DIM ** -0.5)).astype(BF16)
    k_all = _dot(hm, w_ref[:, nq:nq + nk])
    for j in range(nk // LANES):
        sl = slice(j * LANES, (j + 1) * LANES)
        k_ref[:, sl] = head_norm(k_all[:, sl], kg_ref[...]).astype(BF16)
    v_ref[...] = _dot(hm, w_ref[:, nq + nk:]).astype(BF16)


def _qkv_proj(x2, ms, gain, w_qkv_all, li, qg2, kg2, seq, tm=1024):
    t, d = x2.shape
    nq = N_Q_HEADS * HEAD_DIM
    nk = N_KV_HEADS * HEAD_DIM
    tpb = seq // tm
    return pl.pallas_call(
        _qkv_kernel,
        out_shape=(jax.ShapeDtypeStruct((t, nq), BF16),
                   jax.ShapeDtypeStruct((t, nk), BF16),
                   jax.ShapeDtypeStruct((t, nk), BF16)),
        grid=(t // tm,),
        in_specs=[
            pl.BlockSpec((tm, d), lambda i: (i, 0)),
            pl.BlockSpec((1, 2, d), lambda i: (i // tpb, 0, 0)),
            pl.BlockSpec((1, d), lambda i: (0, 0)),
            pl.BlockSpec((None, d, nq + 2 * nk), lambda i: (li, 0, 0)),
            pl.BlockSpec((1, LANES), lambda i: (0, 0)),
            pl.BlockSpec((1, LANES), lambda i: (0, 0)),
        ],
        out_specs=(pl.BlockSpec((tm, nq), lambda i: (i, 0)),
                   pl.BlockSpec((tm, nk), lambda i: (i, 0)),
                   pl.BlockSpec((tm, nk), lambda i: (i, 0))),
        compiler_params=_cparams(("arbitrary",)),
        name="attn_qkv",
    )(x2, ms, gain, w_qkv_all, qg2, kg2)


def _attn_kernel(sink_ref, q_ref, kc_ref, kp_ref, vc_ref, vp_ref, bias_ref, o_ref):
    blk = ATTN_BLOCK
    first = pl.program_id(1) == 0
    lane = lax.broadcasted_iota(jnp.int32, (blk, LANES), 1)
    lo = lane < HEAD_DIM
    mask_lo = jnp.where(lo, 1.0, 0.0).astype(BF16)
    mask_hi = jnp.where(lo, 0.0, 1.0).astype(BF16)
    row = lax.broadcasted_iota(jnp.int32, (2 * blk, 1), 0)
    nt = (((1,), (1,)), ((), ()))
    for gp in range(N_KV_HEADS // 2):
        ksl = slice(gp * LANES, (gp + 1) * LANES)
        kc = kc_ref[:, ksl]
        kp = kp_ref[:, ksl]
        vc = vc_ref[:, ksl]
        vp = vp_ref[:, ksl]
        for qi in range(Q_PER_KV):
            j = gp * Q_PER_KV + qi
            sl = slice(j * LANES, (j + 1) * LANES)
            qs = q_ref[:, sl]
            q2 = jnp.concatenate([qs * mask_lo, qs * mask_hi], axis=0)
            s_p = lax.dot_general(q2, kp, nt, preferred_element_type=F32)
            s_c = lax.dot_general(q2, kc, nt, preferred_element_type=F32)
            bias = bias_ref[j]
            s_p = jnp.where(first, NEG_INF, s_p + bias[:, :blk])
            s_c = s_c + bias[:, blk:]
            h_a = (2 * gp) * Q_PER_KV + qi
            h_b = (2 * gp + 1) * Q_PER_KV + qi
            sink = jnp.where(row < blk, sink_ref[h_a], sink_ref[h_b])
            m = jnp.maximum(jnp.max(jnp.maximum(s_p, s_c), axis=-1, keepdims=True), sink)
            p_p = jnp.exp(s_p - m)
            p_c = jnp.exp(s_c - m)
            den = jnp.sum(p_p + p_c, axis=-1, keepdims=True) + jnp.exp(sink - m)
            pv = _dot(p_p.astype(BF16), vp) + _dot(p_c.astype(BF16), vc)
            pv = pv / den
            o_ref[:, sl] = jnp.where(lo, pv[:blk], pv[blk:]).astype(BF16)


def _attention(q, k, v, bias, sinks, batch, seq):
    t, nq = q.shape
    nk = k.shape[1]
    blk = ATTN_BLOCK
    nb = seq // blk
    cur = lambda b, n, *_: (b * nb + n, 0)
    prev = lambda b, n, *_: (b * nb + jnp.maximum(n - 1, 0), 0)
    return pl.pallas_call(
        _attn_kernel,
        out_shape=jax.ShapeDtypeStruct((t, nq), BF16),
        grid=(batch, nb),
        in_specs=[
            pl.BlockSpec(memory_space=pltpu.SMEM),
            pl.BlockSpec((blk, nq), cur),
            pl.BlockSpec((blk, nk), cur),
            pl.BlockSpec((blk, nk), prev),
            pl.BlockSpec((blk, nk), cur),
            pl.BlockSpec((blk, nk), prev),
            pl.BlockSpec(bias.shape, lambda b, n: (0, 0, 0)),
        ],
        out_specs=pl.BlockSpec((blk, nq), cur),
        compiler_params=_cparams(("arbitrary", "arbitrary")),
        name="swa_attention",
    )(sinks.astype(F32), q, k, k, v, v, bias)


def _swiglu_partial(hf_s, wg_ref, wu_ref, wd_ref):
    hf = hf_s[...]
    g = _dot(hf, wg_ref[...].astype(BF16))
    u = _dot(hf, wu_ref[...].astype(BF16))
    a = (g * _sigmoid(g) * u).astype(BF16)
    return _dot(a, wd_ref[...].astype(BF16))


def _ffn_dense_kernel(x_ref, o_ref, wo_ref, mv_ref, gain_ref, wg_ref, wu_ref, wd_ref,
                      out_ref, hf_s, acc_s):
    j = pl.program_id(1)
    last = pl.num_programs(1) - 1

    @pl.when(j == 0)
    def _():
        mv = mv_ref[0]
        x1 = x_ref[...] + mv[0:1] * _dot(o_ref[...], wo_ref[...])
        out_ref[...] = x1
        hf_s[...] = _rms_mod(x1, gain_ref[...], mv[2:3], mv[1:2]).astype(BF16)
        acc_s[...] = _swiglu_partial(hf_s, wg_ref, wu_ref, wd_ref)

    @pl.when((j > 0) & (j < last))
    def _():
        acc_s[...] += _swiglu_partial(hf_s, wg_ref, wu_ref, wd_ref)

    @pl.when(j == last)
    def _():
        ffn = acc_s[...] + _swiglu_partial(hf_s, wg_ref, wu_ref, wd_ref)
        out_ref[...] = out_ref[...] + mv_ref[0][3:4] * ffn


def _ffn_dense(x2, o, wo_all, mv, gain, w_gu_all, w_down_all, li, seq, tm=1024, tf=512):
    t, d = x2.shape
    f = w_down_all.shape[1]
    nj = f // tf
    tpb = seq // tm
    return pl.pallas_call(
        _ffn_dense_kernel,
        out_shape=jax.ShapeDtypeStruct((t, d), F32),
        grid=(t // tm, nj),
        in_specs=[
            pl.BlockSpec((tm, d), lambda i, j: (i, 0)),
            pl.BlockSpec((tm, o.shape[1]), lambda i, j: (i, 0)),
            pl.BlockSpec((None,) + wo_all.shape[1:], lambda i, j: (li, 0, 0)),
            pl.BlockSpec((1, 4, d), lambda i, j: (i // tpb, 0, 0)),
            pl.BlockSpec((1, d), lambda i, j: (0, 0)),
            pl.BlockSpec((None, d, tf), lambda i, j: (li, 0, j)),
            pl.BlockSpec((None, d, tf), lambda i, j: (li, 0, nj + j)),
            pl.BlockSpec((None, tf, d), lambda i, j: (li, j, 0)),
        ],
        out_specs=pl.BlockSpec((tm, d), lambda i, j: (i, 0)),
        scratch_shapes=[pltpu.VMEM((tm, d), BF16), pltpu.VMEM((tm, d), F32)],
        compiler_params=_cparams(("arbitrary", "arbitrary")),
        name="wo_ffn_dense",
    )(x2, o, wo_all, mv, gain, w_gu_all, w_gu_all, w_down_all)


def _ffn_moe_kernel(te_ref, xs_ref, wg_ref, wu_ref, wd_ref, out_ref, hf_s, acc_s):
    i = pl.program_id(0)
    j = pl.program_id(1)
    used = i < te_ref[pl.num_programs(0)]

    tm = hf_s.shape[0]
    last = pl.num_programs(1) - 1

    @pl.when(used & (j == 0))
    def _():
        for c, part in enumerate(_load_tile_rows(xs_ref, tm)):
            hf_s[:, c * LANES:(c + 1) * LANES] = part.astype(BF16)
        acc_s[...] = _swiglu_partial(hf_s, wg_ref, wu_ref, wd_ref)

    @pl.when(used & (j > 0) & (j < last))
    def _():
        acc_s[...] += _swiglu_partial(hf_s, wg_ref, wu_ref, wd_ref)

    @pl.when(used & (j == last))
    def _():
        _store_tile_rows(out_ref, 0, acc_s[...] + _swiglu_partial(hf_s, wg_ref, wu_ref, wd_ref))

    @pl.when(jnp.logical_not(used) & (j == last))
    def _():
        out_ref[...] = jnp.zeros_like(out_ref)


def _ffn_moe(xs, tile_expert, w_gu_all, w_down_all, li, tm, tf=512):
    r = xs.shape[0] // SUBLANES
    d = w_gu_all.shape[2]
    f = w_down_all.shape[2]
    nj = f // tf
    nt = r // tm

    def jj(i, j, te):
        return jnp.where(i < te[nt], j, 0)

    return pl.pallas_call(
        _ffn_moe_kernel,
        out_shape=jax.ShapeDtypeStruct((r * SUBLANES, LANES), F32),
        grid_spec=pltpu.PrefetchScalarGridSpec(
            num_scalar_prefetch=1,
            grid=(nt, nj),
            in_specs=[
                pl.BlockSpec((tm * SUBLANES, LANES), lambda i, j, te: (i, 0)),
                pl.BlockSpec((None, None, d, tf), lambda i, j, te: (li, te[i], 0, jj(i, j, te))),
                pl.BlockSpec((None, None, d, tf), lambda i, j, te: (li, te[i], 0, nj + jj(i, j, te))),
                pl.BlockSpec((None, None, tf, d), lambda i, j, te: (li, te[i], jj(i, j, te), 0)),
            ],
            out_specs=pl.BlockSpec((tm * SUBLANES, LANES), lambda i, j, te: (i, 0)),
            scratch_shapes=[pltpu.VMEM((tm, d), BF16), pltpu.VMEM((tm, d), F32)],
        ),
        compiler_params=_cparams(("arbitrary", "arbitrary")),
        name="moe_expert_ffn",
    )(tile_expert, xs, w_gu_all, w_gu_all, w_down_all)


def _inproj_kernel(x_ref, ms_ref, gain_ref, wz_ref, wx_ref, wdt_ref, cw_ref, cb_ref,
                   z_ref, xbc_ref, dt_ref, hm_s, carry_s, *, tiles_per_seq):
    i = pl.program_id(0)
    j = pl.program_id(1)
    tm = z_ref.shape[0]

    @pl.when((i == 0) & (j == 0))
    def _():
        carry_s[...] = jnp.zeros_like(carry_s)

    @pl.when(j == 0)
    def _():
        ms = ms_ref[0]
        hm = _rms_mod(x_ref[...], gain_ref[...], ms[1:2], ms[0:1]).astype(BF16)
        hm_s[...] = hm
        dt_ref[...] = _dot(hm, wdt_ref[...])

    hm = hm_s[...]
    strip = 2 * LANES
    rowi = lax.broadcasted_iota(jnp.int32, (SUBLANES, strip), 0)
    first = i % tiles_per_seq == 0
    n_z_strips = z_ref.shape[1] // strip
    for k, c0 in enumerate(range(0, xbc_ref.shape[1], strip)):
        cs = slice(c0, c0 + strip)
        res = _dot(hm, wx_ref[:, cs])
        if k < n_z_strips:
            zs = slice(k * strip, (k + 1) * strip)
            z_ref[:, zs] = _dot(hm, wz_ref[:, zs]).astype(BF16)
        history = jnp.where(first, 0.0, carry_s[j, :, cs])
        acc = res * cw_ref[CONV_WIDTH - 1:CONV_WIDTH, cs] + cb_ref[:, cs]
        for s in range(1, CONV_WIDTH):
            down = pltpu.roll(res, s, 0)
            head = jnp.where(rowi < s, pltpu.roll(history, s, 0), down[:SUBLANES])
            shifted = jnp.concatenate([head, down[SUBLANES:]], axis=0)
            acc = acc + shifted * cw_ref[CONV_WIDTH - 1 - s:CONV_WIDTH - s, cs]
        carry_s[j, :, cs] = res[tm - SUBLANES:]
        xbc_ref[:, cs] = (acc * _sigmoid(acc)).astype(BF16)


def _ssm_in_proj(x2, ms, gain, w_z_all, w_xbc_all, w_dt_all, li, conv_w, conv_b, seq, tm=1024, nj=2):
    t, d = x2.shape
    nz = w_z_all.shape[2] // nj
    nx = w_xbc_all.shape[2] // nj
    tpb = seq // tm
    return pl.pallas_call(
        functools.partial(_inproj_kernel, tiles_per_seq=tpb),
        out_shape=(jax.ShapeDtypeStruct((t, nz * nj), BF16), jax.ShapeDtypeStruct((t, nx * nj), BF16),
                   jax.ShapeDtypeStruct((t, LANES), F32)),
        grid=(t // tm, nj),
        in_specs=[
            pl.BlockSpec((tm, d), lambda i, j: (i, 0)),
            pl.BlockSpec((1, 2, d), lambda i, j: (i // tpb, 0, 0)),
            pl.BlockSpec((1, d), lambda i, j: (0, 0)),
            pl.BlockSpec((None, d, nz), lambda i, j: (li, 0, j)),
            pl.BlockSpec((None, d, nx), lambda i, j: (li, 0, j)),
            pl.BlockSpec((None, d, LANES), lambda i, j: (li, 0, 0)),
            pl.BlockSpec((CONV_WIDTH, nx), lambda i, j: (0, j)),
            pl.BlockSpec((1, nx), lambda i, j: (0, j)),
        ],
        out_specs=(pl.BlockSpec((tm, nz), lambda i, j: (i, j)),
                   pl.BlockSpec((tm, nx), lambda i, j: (i, j)),
                   pl.BlockSpec((tm, LANES), lambda i, j: (i, 0))),
        scratch_shapes=[pltpu.VMEM((tm, d), BF16), pltpu.VMEM((nj, SUBLANES, nx), F32)],
        compiler_params=_cparams(("arbitrary", "arbitrary")),
        name="ssm_in_proj",
    )(x2, ms, gain, w_z_all, w_xbc_all, w_dt_all, conv_w.astype(F32), conv_b.astype(F32)[None])


def _ssd_kernel(z_ref, xs_ref, bc_ref, dt_ref, dtb_ref, alog_ref, dskip_ref, nw_ref, r64_ref,
                y_ref, state_s, ybuf_s):
    L = SSM_CHUNK
    gw = D_INNER // N_SSM_GROUPS
    hpg = N_SSM_HEADS // N_SSM_GROUPS

    @pl.when(pl.program_id(1) == 0)
    def _():
        state_s[...] = jnp.zeros_like(state_s)

    dtv = jax.nn.softplus(dt_ref[...] + dtb_ref[...])
    a = -jnp.exp(alog_ref[...])
    dta = dtv * a
    ri = lax.broadcasted_iota(jnp.int32, (L, L), 0)
    ci = lax.broadcasted_iota(jnp.int32, (L, L), 1)
    causal = ri >= ci
    tri = jnp.where(causal, 1.0, 0.0).astype(BF16)
    p0, p1, p2 = _split3(dta)
    acs = _dot(tri, p0) + _dot(tri, p1) + _dot(tri, p2)
    acs_t = acs.T
    eacs = jnp.exp(acs)
    dte = jnp.exp(acs[L - 1:L, :] - acs)
    stacked = jnp.concatenate([dtv, eacs, dte], axis=0)
    s_hi, s_lo = _split2(stacked)
    expd = _dot(s_hi, r64_ref[...]) + _dot(s_lo, r64_ref[...])

    lo = lax.broadcasted_iota(jnp.int32, (L, LANES), 1) < (LANES // 2)
    for g in range(N_SSM_GROUPS):
        gs = slice(g * gw, (g + 1) * gw)
        b_g = bc_ref[:, g * D_STATE:(g + 1) * D_STATE]
        c_b = bc_ref[:, (N_SSM_GROUPS + g) * D_STATE:(N_SSM_GROUPS + g + 1) * D_STATE]
        b_t = b_g.astype(F32).T.astype(BF16)
        cb = _dot(c_b, b_t)
        xs_g = xs_ref[:, gs].astype(F32)
        xd = xs_g * expd[0:L, gs]
        xd_b = xd.astype(BF16)
        st_prev = state_s[g]
        y_off = _dot(c_b, st_prev.astype(BF16)) * expd[L:2 * L, gs]
        xdw = (xd * expd[2 * L:3 * L, gs]).astype(BF16)
        state_s[g] = expd[2 * L - 1:2 * L, gs] * st_prev + _dot(b_t, xdw)
        for jp in range(hpg // 2):
            ps = slice(jp * LANES, (jp + 1) * LANES)
            slab = xd_b[:, ps]
            halves = []
            for half in range(2):
                hh = g * hpg + 2 * jp + half
                diff = acs[:, hh:hh + 1] - acs_t[hh:hh + 1, :]
                dec = jnp.exp(jnp.where(causal, diff, NEG_INF))
                halves.append(_dot((cb * dec).astype(BF16), slab))
            y = jnp.where(lo, halves[0], halves[1]) + y_off[:, ps]
            cols = slice(g * gw + jp * LANES, g * gw + (jp + 1) * LANES)
            ybuf_s[:, cols] = y + dskip_ref[:, cols] * xs_g[:, ps]

    z = z_ref[...].astype(F32)
    yg = ybuf_s[...] * (z * _sigmoid(z))
    ms = jnp.mean(yg * yg, axis=-1, keepdims=True)
    y_ref[...] = (yg * lax.rsqrt(ms + NORM_EPS) * nw_ref[...]).astype(BF16)


def _ssd(z, xbc, dt, dt_bias, a_log, d_skip, norm_w, batch, seq):
    t = z.shape[0]
    L = SSM_CHUNK
    nc = seq // L
    di, dbc = D_INNER, D_BC
    pad = LANES - N_SSM_HEADS
    dtb = jnp.pad(dt_bias.astype(F32), (0, pad))[None]
    alog = jnp.pad(a_log.astype(F32), (0, pad))[None]
    dskip = jnp.repeat(d_skip.astype(F32), di // N_SSM_HEADS)[None]
    nw = norm_w.astype(F32)[None]
    r64 = jnp.asarray((np.arange(LANES)[:, None] == (np.arange(di)[None, :] // (di // N_SSM_HEADS)))
                      .astype(np.float32)).astype(BF16)
    row = lambda b, c: (b * nc + c, 0)
    const = lambda b, c: (0, 0)
    return pl.pallas_call(
        _ssd_kernel,
        out_shape=jax.ShapeDtypeStruct((t, di), BF16),
        grid=(batch, nc),
        in_specs=[
            pl.BlockSpec((L, di), row),
            pl.BlockSpec((L, di), row),
            pl.BlockSpec((L, dbc), lambda b, c: (b * nc + c, di // dbc)),
            pl.BlockSpec((L, LANES), row),
            pl.BlockSpec((1, LANES), const),
            pl.BlockSpec((1, LANES), const),
            pl.BlockSpec((1, di), const),
            pl.BlockSpec((1, di), const),
            pl.BlockSpec((LANES, di), const),
        ],
        out_specs=pl.BlockSpec((L, di), row),
        scratch_shapes=[
            pltpu.VMEM((N_SSM_GROUPS, D_STATE, di // N_SSM_GROUPS), F32),
            pltpu.VMEM((L, di), F32),
        ],
        compiler_params=_cparams(("arbitrary", "arbitrary")),
        name="ssd_scan",
    )(z, xbc, xbc, dt, dtb, alog, dskip, nw, r64)


ROUTER_SUB = 512


def _router_kernel(x_ref, y_ref, wout_ref, mv_ref, gain_ref, wr_ref,
                   x1_ref, hf_ref, rinfo_ref, cnt_ref, carry_s):
    tm = x_ref.shape[0]
    sub = ROUTER_SUB

    @pl.when(pl.program_id(0) == 0)
    def _():
        carry_s[...] = jnp.zeros_like(carry_s)

    mv = mv_ref[0]
    lane = lax.broadcasted_iota(jnp.int32, (sub, LANES), 1)
    lanef = lane.astype(F32)
    ri = lax.broadcasted_iota(jnp.int32, (sub, sub), 0)
    ci = lax.broadcasted_iota(jnp.int32, (sub, sub), 1)
    before = jnp.where(ri > ci, 1.0, 0.0).astype(BF16)
    carry = carry_s[...]
    for h in range(tm // sub):
        rs = slice(h * sub, (h + 1) * sub)
        x1 = x_ref[rs, :] + mv[0:1] * _dot(y_ref[rs, :], wout_ref[...])
        x1_ref[rs, :] = x1
        hf = _rms_mod(x1, gain_ref[...], mv[2:3], mv[1:2])
        _store_tile_rows(hf_ref, h * sub, hf)
        h_hi, h_lo = _split2(hf)
        hw = _dot(h_hi, wr_ref[...])
        logits = hw[:, :LANES] + hw[:, LANES:] + _dot(h_lo, wr_ref[:, :LANES])
        lg = jnp.where(lane < N_EXPERTS, logits, NEG_INF)
        v0 = jnp.max(lg, axis=-1, keepdims=True)
        i0 = jnp.min(jnp.where(lg == v0, lanef, float(LANES)), axis=-1, keepdims=True)
        lg1 = jnp.where(lanef == i0, NEG_INF, lg)
        v1 = jnp.max(lg1, axis=-1, keepdims=True)
        i1 = jnp.min(jnp.where(lg1 == v1, lanef, float(LANES)), axis=-1, keepdims=True)
        e1 = jnp.exp(v1 - v0)
        w0 = 1.0 / (1.0 + e1)
        w1 = e1 / (1.0 + e1)
        oh0 = jnp.where(lanef == i0, 1.0, 0.0)
        oh1 = jnp.where(lanef == i1, 1.0, 0.0)
        pre = _dot(before, jnp.concatenate([oh0, oh1], axis=1).astype(BF16))
        cnt0 = jnp.sum(oh0, axis=0, keepdims=True)
        cnt1 = jnp.sum(oh1, axis=0, keepdims=True)
        rank0 = jnp.sum(oh0 * (pre[:, :LANES] + carry), axis=-1, keepdims=True)
        rank1 = jnp.sum(oh1 * (pre[:, LANES:] + carry + cnt0), axis=-1, keepdims=True)
        carry = carry + cnt0 + cnt1
        rinfo_ref[rs, :] = jnp.where(lane == 0, i0, jnp.where(lane == 1, i1, jnp.where(
            lane == 2, rank0, jnp.where(lane == 3, rank1, jnp.where(
                lane == 4, w0, jnp.where(lane == 5, w1, 0.0))))))
    carry_s[...] = carry
    cnt_ref[...] = carry


def _router(x2, y, wout_all, li, mv, gain, wr, seq, tm=1024):
    t, d = x2.shape
    tpb = seq // tm
    return pl.pallas_call(
        _router_kernel,
        out_shape=(jax.ShapeDtypeStruct((t, d), F32),
                   jax.ShapeDtypeStruct((t * SUBLANES, LANES), F32),
                   jax.ShapeDtypeStruct((t, LANES), F32), jax.ShapeDtypeStruct((1, LANES), F32)),
        grid=(t // tm,),
        in_specs=[
            pl.BlockSpec((tm, d), lambda i: (i, 0)),
            pl.BlockSpec((tm, y.shape[1]), lambda i: (i, 0)),
            pl.BlockSpec((None,) + wout_all.shape[1:], lambda i: (li, 0, 0)),
            pl.BlockSpec((1, 3, d), lambda i: (i // tpb, 0, 0)),
            pl.BlockSpec((1, d), lambda i: (0, 0)),
            pl.BlockSpec((d, 2 * LANES), lambda i: (0, 0)),
        ],
        out_specs=(pl.BlockSpec((tm, d), lambda i: (i, 0)),
                   pl.BlockSpec((tm * SUBLANES, LANES), lambda i: (i, 0)),
                   pl.BlockSpec((tm, LANES), lambda i: (i, 0)),
                   pl.BlockSpec((1, LANES), lambda i: (0, 0))),
        scratch_shapes=[pltpu.VMEM((1, LANES), F32)],
        compiler_params=_cparams(("arbitrary",)),
        name="ssm_out_router",
    )(x2, y, wout_all, mv, gain, wr)


DMA_UNROLL = 8


def _row_copy(src, src_row, dst, dst_row, sem):
    return pltpu.make_async_copy(
        src.at[pl.ds(pl.multiple_of(src_row * SUBLANES, SUBLANES), SUBLANES)],
        dst.at[pl.ds(pl.multiple_of(dst_row * SUBLANES, SUBLANES), SUBLANES)], sem)


def _dispatch_kernel(pos_ref, hf_ref, sorted_in_ref, sorted_ref, sem):
    del sorted_in_ref
    tm = hf_ref.shape[0] // SUBLANES
    base = pl.program_id(0) * (TOP_K * tm)

    def issue(blk, carry):
        for u in range(DMA_UNROLL):
            t = blk * DMA_UNROLL + u
            for k in range(TOP_K):
                _row_copy(hf_ref, t, sorted_ref, pos_ref[base + TOP_K * t + k], sem).start(priority=k)
        return carry

    lax.fori_loop(0, tm // DMA_UNROLL, issue, 0)
    for k in range(TOP_K):
        pltpu.make_async_copy(hf_ref, sorted_ref.at[pl.ds(0, tm * SUBLANES)], sem).wait()


def _dispatch(pos_flat, hf, sorted_init, tm=1024):
    t = hf.shape[0] // SUBLANES
    return pl.pallas_call(
        _dispatch_kernel,
        out_shape=jax.ShapeDtypeStruct(sorted_init.shape, hf.dtype),
        grid_spec=pltpu.PrefetchScalarGridSpec(
            num_scalar_prefetch=1,
            grid=(t // tm,),
            in_specs=[pl.BlockSpec((tm * SUBLANES, LANES), lambda i, pos: (i, 0)),
                      pl.BlockSpec(memory_space=pl.ANY)],
            out_specs=pl.BlockSpec(memory_space=pl.ANY),
            scratch_shapes=[pltpu.SemaphoreType.DMA],
        ),
        input_output_aliases={2: 0},
        compiler_params=_cparams(("arbitrary",)),
        name="moe_dispatch",
    )(pos_flat, hf, sorted_init)


def _combine_kernel(pos_ref, x1_ref, rinfo_ref, g2_ref, ys_ref, out_ref, buf, sem):
    tm, d = x1_ref.shape
    i = pl.program_id(0)
    slot = i % 2

    def issue_tile(tile, dst_slot):
        base = tile * (TOP_K * tm)

        def issue(blk, carry):
            for u in range(DMA_UNROLL):
                t = blk * DMA_UNROLL + u
                for k in range(TOP_K):
                    _row_copy(ys_ref, pos_ref[base + TOP_K * t + k], buf.at[dst_slot, k], t,
                              sem.at[dst_slot]).start(priority=k)
            return carry

        lax.fori_loop(0, tm // DMA_UNROLL, issue, 0)

    @pl.when(i == 0)
    def _():
        issue_tile(0, 0)

    @pl.when(i + 1 < pl.num_programs(0))
    def _():
        issue_tile(i + 1, 1 - slot)

    for k in range(TOP_K):
        pltpu.make_async_copy(ys_ref.at[pl.ds(0, tm * SUBLANES)], buf.at[slot, k], sem.at[slot]).wait()
    rinfo = rinfo_ref[...]
    w0 = rinfo[:, 4:5]
    w1 = rinfo[:, 5:6]
    g2 = g2_ref[0]
    for c in range(d // LANES):
        cs = slice(c * LANES, (c + 1) * LANES)
        ffn = (w0 * buf[slot, 0, pl.ds(c, tm, stride=SUBLANES), :]
               + w1 * buf[slot, 1, pl.ds(c, tm, stride=SUBLANES), :])
        out_ref[:, cs] = x1_ref[:, cs] + g2[:, cs] * ffn


def _combine(pos_flat, x1, rinfo, g2, ys, seq, tm=1024):
    t, d = x1.shape
    tpb = seq // tm
    return pl.pallas_call(
        _combine_kernel,
        out_shape=jax.ShapeDtypeStruct((t, d), F32),
        grid_spec=pltpu.PrefetchScalarGridSpec(
            num_scalar_prefetch=1,
            grid=(t // tm,),
            in_specs=[pl.BlockSpec((tm, d), lambda i, pos: (i, 0)),
                      pl.BlockSpec((tm, LANES), lambda i, pos: (i, 0)),
                      pl.BlockSpec((1, 1, d), lambda i, pos: (i // tpb, 0, 0)),
                      pl.BlockSpec(memory_space=pl.ANY)],
            out_specs=pl.BlockSpec((tm, d), lambda i, pos: (i, 0)),
            scratch_shapes=[pltpu.VMEM((2, TOP_K, tm * SUBLANES, LANES), F32),
                            pltpu.SemaphoreType.DMA((2,))],
        ),
        compiler_params=_cparams(("arbitrary",)),
        name="moe_combine",
    )(pos_flat, x1, rinfo, g2, ys)


def _attn_weights(attn_w_qkv, attn_w_o):
    n_att, d, _ = attn_w_qkv.shape
    nq = N_Q_HEADS * HEAD_DIM
    head_grid = (N_KV_HEADS // 2, 2, Q_PER_KV, HEAD_DIM)
    w_q = attn_w_qkv[:, :, :nq].reshape((n_att, d) + head_grid)
    w_q = jnp.swapaxes(w_q, 3, 4).reshape(n_att, d, nq)
    w_qkv_all = jnp.concatenate([w_q, attn_w_qkv[:, :, nq:]], axis=2).astype(BF16)
    w_o = attn_w_o.reshape((n_att,) + head_grid + (attn_w_o.shape[2],))
    w_o_all = jnp.swapaxes(w_o, 2, 3).reshape(n_att, nq, attn_w_o.shape[2]).astype(BF16)
    return w_qkv_all, w_o_all


def _attn_layer(x2, mod_i, gain_i, bias, w_qkv_all, w_o_all, w_gu_all, w_down_all, li,
                q_gain, k_gain, sinks, batch, seq):
    qg2 = jnp.tile(q_gain.astype(F32), 2)[None]
    kg2 = jnp.tile(k_gain.astype(F32), 2)[None]
    q, k, v = _qkv_proj(x2, mod_i[:, 0:2], gain_i[0:1], w_qkv_all, li, qg2, kg2, seq)
    o = _attention(q, k, v, bias, sinks, batch, seq)
    return _ffn_dense(x2, o, w_o_all, mod_i[:, 2:6], gain_i[1:2], w_gu_all, w_down_all, li, seq)


def _moe_slots(rinfo, cnt, n_tokens, moe_tm):
    counts = cnt[0, :N_EXPERTS].astype(jnp.int32)
    padded = ((counts + moe_tm - 1) // moe_tm) * moe_tm
    ends = jnp.cumsum(padded)
    base = ends - padded
    eidx = rinfo[:, 0:TOP_K].astype(jnp.int32)
    rank = rinfo[:, 2:2 + TOP_K].astype(jnp.int32)
    pos = jnp.sum(jnp.where(eidx[..., None] == jnp.arange(N_EXPERTS), base, 0), axis=-1) + rank
    n_tiles = (TOP_K * n_tokens) // moe_tm + N_EXPERTS
    tile_start = jnp.arange(n_tiles, dtype=jnp.int32) * moe_tm
    tile_expert = jnp.minimum(jnp.sum(tile_start[:, None] >= ends[None, :], axis=-1),
                              N_EXPERTS - 1).astype(jnp.int32)
    n_used = ends[-1] // moe_tm
    tile_expert = jnp.where(tile_start < ends[-1], tile_expert, tile_expert[n_used - 1])
    return pos.reshape(-1), jnp.concatenate([tile_expert, n_used[None].astype(jnp.int32)])


def _ssm_layer(x2, mod_i, gain_i, w_z_all, w_xbc_all, w_dt_all, w_out_all, w_gu_all, w_down_all, li,
               conv_w, conv_b, dt_bias, a_log, d_skip, norm_w, w_router, sorted_init,
               batch, seq, moe_tm):
    z, xbc, dt = _ssm_in_proj(x2, mod_i[:, 0:2], gain_i[0:1], w_z_all, w_xbc_all,
                              w_dt_all, li, conv_w, conv_b, seq)
    y = _ssd(z, xbc, dt, dt_bias, a_log, d_skip, norm_w, batch, seq)

    wr = jnp.pad(w_router.astype(F32), ((0, 0), (0, LANES - N_EXPERTS)))
    wr_hi = wr.astype(BF16)
    wr_lo = (wr - wr_hi.astype(F32)).astype(BF16)
    x1, hf, rinfo, cnt = _router(x2, y, w_out_all, li, mod_i[:, 2:5],
                                 gain_i[1:2], jnp.concatenate([wr_hi, wr_lo], axis=1), seq)
    pos_flat, tile_expert = _moe_slots(rinfo, cnt, x2.shape[0], moe_tm)
    xs = _dispatch(pos_flat, hf, sorted_init)
    ys = _ffn_moe(xs, tile_expert, w_gu_all, w_down_all, li, moe_tm)
    return _combine(pos_flat, x1, rinfo, mod_i[:, 5:6], ys, seq), xs


def kernel(x, c, ada_w, ada_b, norm_gain, rel_bias, attn_w_qkv, attn_q_gain, attn_k_gain, attn_sinks, attn_w_o, ssm_w_in, ssm_conv_w, ssm_conv_b, ssm_dt_bias, ssm_a_log, ssm_d, ssm_norm_w, ssm_w_out, ffn_w_gu, ffn_w_down, moe_w_router, moe_w_gu, moe_w_down):
    batch, seq, d = x.shape
    moe_tm = 1024
    mod = _adaln_mod(c.astype(F32), ada_w.astype(F32), ada_b.astype(F32))
    mod = mod.reshape(DEPTH, batch, 6, d)
    bias = _bias_table(rel_bias)

    n_zx = 2 * D_INNER + D_BC
    w_qkv_all, w_o_all = _attn_weights(attn_w_qkv, attn_w_o)
    ffn_gu_all = ffn_w_gu.astype(BF16)
    ffn_down_all = ffn_w_down.astype(BF16)
    w_z_all = ssm_w_in[:, :, :D_INNER].astype(BF16)
    w_xbc_all = ssm_w_in[:, :, D_INNER:n_zx].astype(BF16)
    w_dt_all = jnp.pad(ssm_w_in[:, :, n_zx:], ((0, 0), (0, 0), (0, LANES - N_SSM_HEADS))).astype(BF16)
    w_out_all = ssm_w_out.astype(BF16)
    moe_gu_all = moe_w_gu.astype(F32)
    moe_down_all = moe_w_down.astype(F32)

    n_sorted = TOP_K * batch * seq + N_EXPERTS * moe_tm
    sorted_buf = jnp.zeros((n_sorted * SUBLANES, LANES), F32)

    x2 = x.reshape(batch * seq, d).astype(F32)
    for i in range(DEPTH):
        j = i // 2
        gain_i = norm_gain[i].astype(F32)
        if i % 2 == 0:
            x2 = _attn_layer(x2, mod[i], gain_i, bias, w_qkv_all, w_o_all, ffn_gu_all, ffn_down_all,
                             j, attn_q_gain[j], attn_k_gain[j], attn_sinks[j], batch, seq)
        else:
            x2, sorted_buf = _ssm_layer(x2, mod[i], gain_i, w_z_all, w_xbc_all, w_dt_all, w_out_all, moe_gu_all,
                                        moe_down_all, j, ssm_conv_w[j], ssm_conv_b[j], ssm_dt_bias[j],
                                        ssm_a_log[j], ssm_d[j], ssm_norm_w[j], moe_w_router[j],
                                        sorted_buf, batch, seq, moe_tm)
    return x2.reshape(batch, seq, d).astype(x.dtype)
```
